```python
import math
import jax, jax.numpy as jnp
from jax import lax
import numpy as np

D_MODEL = 1024
BATCH = 8
SEQ = 2048
DEPTH = 2

D_PLE = 256
D_POOL = 512
D_CONV = D_MODEL - D_POOL
POOL_WINDOWS = (2, 4, 8, 16)
N_POOL_GROUPS = len(POOL_WINDOWS)
POOL_GROUP_DIM = D_POOL // N_POOL_GROUPS
CONV_WIDTH = 31
D_IN_PROJ = D_POOL + 2 * D_CONV
D_FF_DENSE = 2816
N_EXPERTS = 8
TOP_K = 2
D_FF_EXPERT = 3584
N_DENSE = (DEPTH + 1) // 2
N_MOE = DEPTH // 2
DEEPNORM_ALPHA = (2.0 * DEPTH) ** 0.25
DEEPNORM_BETA = (8.0 * DEPTH) ** -0.25
LN_EPS = 1e-5

kernel_name = "hybrid_pool_conformer_moe_deepnorm"


def layer_norm(x, g, b):
    xf = x.astype(jnp.float32)
    mu = jnp.mean(xf, axis=-1, keepdims=True)
    var = jnp.mean(jnp.square(xf - mu), axis=-1, keepdims=True)
    y = (xf - mu) * lax.rsqrt(var + LN_EPS)
    return (y * g.astype(jnp.float32) + b.astype(jnp.float32)).astype(x.dtype)


def causal_multiscale_pool(u):
    B, S, _ = u.shape
    uf = u.astype(jnp.float32).reshape(B, S, N_POOL_GROUPS, POOL_GROUP_DIM)
    cs = jnp.cumsum(uf, axis=1)
    pos = jnp.arange(S, dtype=jnp.float32)[None, :, None]
    outs = []
    for gi, w in enumerate(POOL_WINDOWS):
        c = cs[:, :, gi]
        prev = jnp.pad(c, ((0, 0), (w, 0), (0, 0)))[:, :S]
        count = jnp.minimum(pos + 1.0, float(w))
        outs.append((c - prev) / count - uf[:, :, gi])
    return jnp.stack(outs, axis=2).astype(u.dtype)


def pool_mixer(u, w_grp, scale):
    d = causal_multiscale_pool(u)
    y = jnp.einsum('bsgc,gcd->bsgd', d, w_grp)
    return y.reshape(u.shape) * scale


def conformer_conv(a, gate, w_dw, b_dw, g_ln, b_ln, w_pw):
    v = a * jax.nn.sigmoid(gate)
    vp = jnp.pad(v, ((0, 0), (CONV_WIDTH - 1, 0), (0, 0)))
    y = lax.conv_general_dilated(
        vp, w_dw[:, None, :].astype(v.dtype), window_strides=(1,), padding='VALID',
        dimension_numbers=('NWC', 'WIO', 'NWC'), feature_group_count=D_CONV) + b_dw
    y = jax.nn.silu(layer_norm(y, g_ln, b_ln))
    return y @ w_pw


def swiglu(h, w1, w3, w2):
    return (jax.nn.silu(h @ w1) * (h @ w3)) @ w2


def moe_swiglu(h, w_router, w1, w3, w2):
    B, S, D = h.shape
    t = h.reshape(B * S, D)
    logits = (t @ w_router).astype(jnp.float32)
    top_vals, top_idx = lax.top_k(logits, TOP_K)
    top_w = jax.nn.softmax(top_vals, axis=-1)
    gates = jnp.sum(jax.nn.one_hot(top_idx, N_EXPERTS, dtype=jnp.float32) * top_w[..., None],
                    axis=1).astype(h.dtype)
    out = jnp.zeros_like(t)
    for e in range(N_EXPERTS):
        out = out + gates[:, e:e + 1] * swiglu(t, w1[e], w3[e], w2[e])
    return out.reshape(B, S, D)


def setup_inputs(seed: int = 0) -> dict:
    key = jax.random.key(seed)
    ks = iter(jax.random.split(key, 32))
    f32 = jnp.float32

    def nrm(shape, scale):
        return jax.random.normal(next(ks), shape, f32) * scale

    def gain(shape):
        return 1.0 + nrm(shape, 0.02)

    L = DEPTH
    return {
        "x": nrm((BATCH, SEQ, D_MODEL), 1.0),
        "p": nrm((DEPTH, BATCH, SEQ, D_PLE), 1.0),
        "w_in": nrm((L, D_MODEL, D_IN_PROJ), D_MODEL ** -0.5),
        "pool_w": nrm((L, N_POOL_GROUPS, POOL_GROUP_DIM, POOL_GROUP_DIM), POOL_GROUP_DIM ** -0.5),
        "pool_scale": gain((L, D_POOL)),
        "conv_w": nrm((L, CONV_WIDTH, D_CONV), CONV_WIDTH ** -0.5),
        "conv_b": nrm((L, D_CONV), 0.02),
        "conv_ln_g": gain((L, D_CONV)),
        "conv_ln_b": nrm((L, D_CONV), 0.02),
        "conv_pw": nrm((L, D_CONV, D_CONV), D_CONV ** -0.5),
        "w_out": nrm((L, D_MODEL, D_MODEL), DEEPNORM_BETA * D_MODEL ** -0.5),
        "ln1_g": gain((L, D_MODEL)),
        "ln1_b": nrm((L, D_MODEL), 0.02),
        "dense_w1": nrm((N_DENSE, D_MODEL, D_FF_DENSE), D_MODEL ** -0.5),
        "dense_w3": nrm((N_DENSE, D_MODEL, D_FF_DENSE), D_MODEL ** -0.5),
        "dense_w2": nrm((N_DENSE, D_FF_DENSE, D_MODEL), DEEPNORM_BETA * D_FF_DENSE ** -0.5),
        "router_w": nrm((N_MOE, D_MODEL, N_EXPERTS), D_MODEL ** -0.5),
        "exp_w1": nrm((N_MOE, N_EXPERTS, D_MODEL, D_FF_EXPERT), D_MODEL ** -0.5),
        "exp_w3": nrm((N_MOE, N_EXPERTS, D_MODEL, D_FF_EXPERT), D_MODEL ** -0.5),
        "exp_w2": nrm((N_MOE, N_EXPERTS, D_FF_EXPERT, D_MODEL), DEEPNORM_BETA * D_FF_EXPERT ** -0.5),
        "ln2_g": gain((L, D_MODEL)),
        "ln2_b": nrm((L, D_MODEL), 0.02),
        "ple_gate_w": nrm((L, D_MODEL, D_MODEL), D_MODEL ** -0.5),
        "ple_w": nrm((L, D_PLE, D_MODEL), DEEPNORM_BETA * D_PLE ** -0.5),
    }


def reference(x, p, w_in, pool_w, pool_scale, conv_w, conv_b, conv_ln_g, conv_ln_b, conv_pw,
              w_out, ln1_g, ln1_b, dense_w1, dense_w3, dense_w2, router_w, exp_w1, exp_w3,
              exp_w2, ln2_g, ln2_b, ple_gate_w, ple_w):
    for i in range(DEPTH):
        u = x @ w_in[i]
        u_pool = u[..., :D_POOL]
        u_val = u[..., D_POOL:D_POOL + D_CONV]
        u_gate = u[..., D_POOL + D_CONV:]
        y_pool = pool_mixer(u_pool, pool_w[i], pool_scale[i])
        y_conv = conformer_conv(u_val, u_gate, conv_w[i], conv_b[i],
                                conv_ln_g[i], conv_ln_b[i], conv_pw[i])
        mix = jnp.concatenate([y_pool, y_conv], axis=-1) @ w_out[i]
        x = layer_norm(DEEPNORM_ALPHA * x + mix, ln1_g[i], ln1_b[i])
        j = i // 2
        if i % 2 == 0:
            f = swiglu(x, dense_w1[j], dense_w3[j], dense_w2[j])
        else:
            f = moe_swiglu(x, router_w[j], exp_w1[j], exp_w3[j], exp_w2[j])
        x = layer_norm(DEEPNORM_ALPHA * x + f, ln2_g[i], ln2_b[i])
        x = x + jax.nn.sigmoid(x @ ple_gate_w[i]) * (p[i] @ ple_w[i])
    return x
```

```python
import functools

import jax
import jax.numpy as jnp
from jax import lax
from jax.experimental import pallas as pl
from jax.experimental.pallas import tpu as pltpu

D_MODEL = 1024
D_PLE = 256
D_POOL = 512
D_CONV = D_MODEL - D_POOL
POOL_WINDOWS = (2, 4, 8, 16)
POOL_GROUP_DIM = D_POOL // len(POOL_WINDOWS)
CONV_WIDTH = 31
D_IN_PROJ = D_POOL + 2 * D_CONV
N_EXPERTS = 8
TOP_K = 2
LN_EPS = 1e-5

F32 = jnp.float32
BF16 = jnp.bfloat16

SUBLANES = 8
LANES = 128
VMEM_LIMIT_BYTES = 56 * 1024 * 1024

HALO = 32
TS_MIX = 512
RB_CONV = 64
TM_DENSE = 512
FF_CHUNK_DENSE = 1024
TR_ROUTE = 512
TM_GROUP = 1024
FF_CHUNK_MOE = 512
TG_GATHER = 512
TC_COMBINE = 256


def _sigmoid(z):
    return 1.0 / (1.0 + jnp.exp(-z))


def _layer_norm(h, g, b):
    mu = jnp.mean(h, axis=-1, keepdims=True)
    c = h - mu
    var = jnp.mean(c * c, axis=-1, keepdims=True)
    return c * lax.rsqrt(var + LN_EPS) * g + b


def _dot(a, b):
    return jnp.dot(a, b, preferred_element_type=F32)


def _mixer_kernel(alpha, x_ref, w_in_ref, pool_w_ref, pool_scale_ref, conv_w_ref, conv_b_ref,
                  cln_g_ref, cln_b_ref, conv_pw_ref, w_out_ref, ln_g_ref, ln_b_ref,
                  o_ref, p_scr, v_scr, y_scr):
    ts = x_ref.shape[0]
    s = pl.program_id(1)

    @pl.when(s == 0)
    def _():
        p_scr[0:HALO, :] = jnp.zeros((HALO, D_POOL), F32)
        v_scr[0:HALO, :] = jnp.zeros((HALO, D_CONV), F32)

    x = x_ref[...]
    u = _dot(x.astype(BF16), w_in_ref[...])
    p_scr[HALO:, :] = u[:, :D_POOL]
    v_scr[HALO:, :] = u[:, D_POOL:D_POOL + D_CONV] * _sigmoid(u[:, D_POOL + D_CONV:])

    pos = s * ts + lax.broadcasted_iota(jnp.int32, (ts, POOL_GROUP_DIM), 0)
    y_parts = []
    for g, w in enumerate(POOL_WINDOWS):
        lanes = slice(g * POOL_GROUP_DIM, (g + 1) * POOL_GROUP_DIM)
        pe = p_scr[:, lanes]
        acc = pe
        step = 1
        while step < w:
            acc = acc + pltpu.roll(acc, step, axis=0)
            step *= 2
        cnt = jnp.minimum(pos + 1, w).astype(F32)
        d = acc[HALO:, :] / cnt - pe[HALO:, :]
        yg = _dot(d.astype(BF16), pool_w_ref[g]) * pool_scale_ref[:, lanes]
        y_parts.append(yg.astype(BF16))

    n_lane_chunks = D_CONV // LANES

    def conv_block(rb, carry):
        r0 = pl.multiple_of(rb * RB_CONV, RB_CONV)
        cols = []
        for lc in range(n_lane_chunks):
            lanes = slice(lc * LANES, (lc + 1) * LANES)
            e = v_scr[pl.ds(r0, RB_CONV + HALO), lanes]
            acc = jnp.zeros((RB_CONV, LANES), F32)
            for b in range(SUBLANES):
                eb = e if b == 0 else pltpu.roll(e, b, axis=0)
                for a in range(HALO // SUBLANES):
                    dist = SUBLANES * a + b
                    if dist > CONV_WIDTH - 1:
                        continue
                    k = CONV_WIDTH - 1 - dist
                    lo = HALO - SUBLANES * a
                    acc = acc + eb[lo:lo + RB_CONV, :] * conv_w_ref[k:k + 1, lanes]
            cols.append(acc)
        y = jnp.concatenate(cols, axis=-1) + conv_b_ref[...]
        z = _layer_norm(y, cln_g_ref[...], cln_b_ref[...])
        y_scr[pl.ds(r0, RB_CONV), :] = (z * _sigmoid(z)).astype(BF16)
        return carry

    lax.fori_loop(0, ts // RB_CONV, conv_block, 0)
    y_conv = _dot(y_scr[...], conv_pw_ref[...])
    y_parts.append(y_conv.astype(BF16))

    mix = _dot(jnp.concatenate(y_parts, axis=-1), w_out_ref[...])
    o_ref[...] = _layer_norm(alpha * x + mix, ln_g_ref[...], ln_b_ref[...])

    p_scr[0:HALO, :] = p_scr[ts:ts + HALO, :]
    v_scr[0:HALO, :] = v_scr[ts:ts + HALO, :]


def _const_spec(shape):
    nd = len(shape)
    return pl.BlockSpec(shape, lambda *_: (0,) * nd, pipeline_mode=pl.Buffered(1))


def _mixer(alpha, x, w_in, pool_w, pool_scale, conv_w, conv_b, cln_g, cln_b, conv_pw, w_out, ln_g, ln_b):
    B, S, D = x.shape
    ts = TS_MIX
    row = lambda v: v.reshape(1, -1)
    consts = [w_in.astype(BF16), pool_w.astype(BF16), row(pool_scale), conv_w, row(conv_b),
              row(cln_g), row(cln_b), conv_pw.astype(BF16), w_out.astype(BF16), row(ln_g), row(ln_b)]
    return pl.pallas_call(
        functools.partial(_mixer_kernel, alpha),
        grid=(B, S // ts),
        in_specs=[pl.BlockSpec((None, ts, D), lambda b, s: (b, s, 0))] + [_const_spec(c.shape) for c in consts],
        out_specs=pl.BlockSpec((None, ts, D), lambda b, s: (b, s, 0)),
        out_shape=jax.ShapeDtypeStruct((B, S, D), F32),
        scratch_shapes=[pltpu.VMEM((HALO + ts, D_POOL), F32),
                        pltpu.VMEM((HALO + ts, D_CONV), F32),
                        pltpu.VMEM((ts, D_CONV), BF16)],
        compiler_params=pltpu.CompilerParams(
            dimension_semantics=("arbitrary", "arbitrary"), vmem_limit_bytes=VMEM_LIMIT_BYTES),
        name="mixer",
    )(x, *consts)


def _norm_and_embed(alpha, x, f, p, ln_g, ln_b, gate_w, ple_w):
    h = _layer_norm(alpha * x + f, ln_g, ln_b)
    gate = _sigmoid(_dot(h.astype(BF16), gate_w))
    return h + gate * _dot(p.astype(BF16), ple_w)


def _dense_ffn_kernel(alpha, x_ref, p_ref, w1_ref, w3_ref, w2_ref, ln_g_ref, ln_b_ref, gate_w_ref, ple_w_ref,
                      o_ref, acc_ref):
    x = x_ref[...]
    xb = x.astype(BF16)
    ff = w1_ref.shape[1]
    for c0 in range(0, ff, FF_CHUNK_DENSE):
        c1 = min(c0 + FF_CHUNK_DENSE, ff)
        a = _dot(xb, w1_ref[:, c0:c1])
        b = _dot(xb, w3_ref[:, c0:c1])
        h = (a * _sigmoid(a) * b).astype(BF16)
        part = _dot(h, w2_ref[c0:c1, :])
        if c0 == 0:
            acc_ref[...] = part
        else:
            acc_ref[...] += part
    o_ref[...] = _norm_and_embed(alpha, x, acc_ref[...], p_ref[...], ln_g_ref[...], ln_b_ref[...],
                                 gate_w_ref[...], ple_w_ref[...])


def _dense_layer(alpha, x, p, w1, w3, w2, ln_g, ln_b, gate_w, ple_w):
    T, D = x.shape
    tm = TM_DENSE
    row = lambda v: v.reshape(1, -1)
    consts = [w1.astype(BF16), w3.astype(BF16), w2.astype(BF16), row(ln_g), row(ln_b),
              gate_w.astype(BF16), ple_w.astype(BF16)]
    return pl.pallas_call(
        functools.partial(_dense_ffn_kernel, alpha),
        grid=(T // tm,),
        in_specs=[pl.BlockSpec((tm, D), lambda i: (i, 0)),
                  pl.BlockSpec((tm, D_PLE), lambda i: (i, 0))] + [_const_spec(c.shape) for c in consts],
        out_specs=pl.BlockSpec((tm, D), lambda i: (i, 0)),
        out_shape=jax.ShapeDtypeStruct((T, D), F32),
        scratch_shapes=[pltpu.VMEM((tm, D), F32)],
        compiler_params=pltpu.CompilerParams(
            dimension_semantics=("arbitrary",), vmem_limit_bytes=VMEM_LIMIT_BYTES),
        name="dense_ffn",
    )(x, p, *consts)


def _router_kernel(x_ref, wt_ref, idx_ref, gate_ref, cnt_ref, carry_ref):
    i = pl.program_id(0)
    tr = x_ref.shape[0]

    @pl.when(i == 0)
    def _():
        carry_ref[...] = jnp.zeros_like(carry_ref)

    x = x_ref[...]
    wt = wt_ref[...]
    xh = x.astype(BF16)
    xl = (x - xh.astype(F32)).astype(BF16)
    wh = wt.astype(BF16)
    wl = (wt - wh.astype(F32)).astype(BF16)
    nt_dims = (((1,), (1,)), ((), ()))
    dg = lambda a, b: lax.dot_general(a, b, nt_dims, preferred_element_type=F32)
    logits = dg(wh, xh) + (dg(wh, xl) + dg(wl, xh))

    eid = lax.broadcasted_iota(jnp.int32, logits.shape, 0)
    m1 = jnp.max(logits, axis=0, keepdims=True)
    i1 = jnp.min(jnp.where(logits == m1, eid, N_EXPERTS), axis=0, keepdims=True)
    rest = jnp.where(eid == i1, -jnp.inf, logits)
    m2 = jnp.max(rest, axis=0, keepdims=True)
    i2 = jnp.min(jnp.where(rest == m2, eid, N_EXPERTS), axis=0, keepdims=True)
    e2 = jnp.exp(m2 - m1)
    g1 = 1.0 / (1.0 + e2)
    g2 = e2 / (1.0 + e2)

    oh1 = (eid == i1).astype(F32)
    oh2 = (eid == i2).astype(F32)
    chosen = oh1 + oh2
    r_i = lax.broadcasted_iota(jnp.int32, (tr, tr), 0)
    c_i = lax.broadcasted_iota(jnp.int32, (tr, tr), 1)
    before = (r_i < c_i).astype(BF16)
    excl = _dot(chosen.astype(BF16), before) + carry_ref[:, 0:1]
    rank1 = jnp.sum(oh1 * excl, axis=0, keepdims=True)
    rank2 = jnp.sum(oh2 * excl, axis=0, keepdims=True)

    total = carry_ref[:, 0:1] + jnp.sum(chosen, axis=1, keepdims=True)
    carry_ref[...] = jnp.broadcast_to(total, carry_ref.shape)
    cnt_ref[...] = jnp.broadcast_to(total, cnt_ref.shape).astype(jnp.int32)

    zi = jnp.zeros((SUBLANES - 4, tr), jnp.int32)
    idx_ref[...] = jnp.concatenate(
        [i1, i2, rank1.astype(jnp.int32), rank2.astype(jnp.int32), zi], axis=0)
    gate_ref[...] = jnp.concatenate([g1, g2, jnp.zeros((SUBLANES - 2, tr), F32)], axis=0)


def _router(x, router_w):
    T, D = x.shape
    tr = TR_ROUTE
    return pl.pallas_call(
        _router_kernel,
        grid=(T // tr,),
        in_specs=[pl.BlockSpec((tr, D), lambda i: (i, 0)), _const_spec((N_EXPERTS, D))],
        out_specs=[pl.BlockSpec((SUBLANES, tr), lambda i: (0, i)),
                   pl.BlockSpec((SUBLANES, tr), lambda i: (0, i)),
                   pl.BlockSpec((N_EXPERTS, LANES), lambda i: (0, 0))],
        out_shape=[jax.ShapeDtypeStruct((SUBLANES, T), jnp.int32),
                   jax.ShapeDtypeStruct((SUBLANES, T), F32),
                   jax.ShapeDtypeStruct((N_EXPERTS, LANES), jnp.int32)],
        scratch_shapes=[pltpu.VMEM((N_EXPERTS, LANES), F32)],
        compiler_params=pltpu.CompilerParams(
            dimension_semantics=("arbitrary",), vmem_limit_bytes=VMEM_LIMIT_BYTES),
        name="router",
    )(x, router_w.T)


def _row_copy(src_hbm, src_row, dst, dst_row, sem):
    return pltpu.make_async_copy(src_hbm.at[pl.ds(src_row, 1), :], dst.at[pl.ds(dst_row, 1), :], sem)


def _dispatch_kernel(src_ref, nrows_ref, x_hbm, xs_hbm, sem):
    i = pl.program_id(0)
    base = i * TG_GATHER

    @pl.when(base < nrows_ref[0])
    def _():
        def issue(r, c):
            _row_copy(x_hbm, src_ref[base + r], xs_hbm, base + r, sem).start()
            return c

        lax.fori_loop(0, TG_GATHER, issue, 0)
        pltpu.make_async_copy(x_hbm.at[pl.ds(0, TG_GATHER), :], xs_hbm.at[pl.ds(base, TG_GATHER), :], sem).wait()

    @pl.when(base >= nrows_ref[0])
    def _():
        fill = pltpu.make_async_copy(x_hbm.at[pl.ds(0, TG_GATHER), :], xs_hbm.at[pl.ds(base, TG_GATHER), :], sem)
        fill.start()
        fill.wait()


def _dispatch(x, src, nrows, r_max):
    T, D = x.shape
    return pl.pallas_call(
        _dispatch_kernel,
        grid_spec=pltpu.PrefetchScalarGridSpec(
            num_scalar_prefetch=2, grid=(r_max // TG_GATHER,),
            in_specs=[pl.BlockSpec(memory_space=pl.ANY)],
            out_specs=pl.BlockSpec(memory_space=pl.ANY),
            scratch_shapes=[pltpu.SemaphoreType.DMA(())]),
        out_shape=jax.ShapeDtypeStruct((r_max, D), F32),
        compiler_params=pltpu.CompilerParams(dimension_semantics=("arbitrary",)),
        name="dispatch",
    )(src, nrows, x)


def _expert_ffn_kernel(te_ref, nt_ref, xs_ref, w1_ref, w3_ref, w2_ref, o_ref, xb_ref):
    i = pl.program_id(0)
    f = pl.program_id(1)

    @pl.when(i < nt_ref[0])
    def _():
        @pl.when(f == 0)
        def _():
            xb_ref[...] = xs_ref[...].astype(BF16)

        xb = xb_ref[...]
        a = _dot(xb, w1_ref[...])
        b = _dot(xb, w3_ref[...])
        h = (a * _sigmoid(a) * b).astype(BF16)
        part = _dot(h, w2_ref[...])

        @pl.when(f == 0)
        def _():
            o_ref[...] = part

        @pl.when(f > 0)
        def _():
            o_ref[...] += part

    @pl.when((i >= nt_ref[0]) & (f == 0))
    def _():
        o_ref[...] = jnp.zeros_like(o_ref)


def _expert_ffn(xs, te, nt, w1, w3, w2):
    r_max, D = xs.shape
    E, _, FF = w1.shape
    tm, cf = TM_GROUP, FF_CHUNK_MOE
    nf = FF // cf

    def tile(i, nt_ref):
        return jnp.minimum(i, nt_ref[0] - 1)

    def chunk(i, f, nt_ref):
        return jnp.where(i < nt_ref[0], f, nf - 1)

    return pl.pallas_call(
        _expert_ffn_kernel,
        grid_spec=pltpu.PrefetchScalarGridSpec(
            num_scalar_prefetch=2, grid=(r_max // tm, nf),
            in_specs=[pl.BlockSpec((tm, D), lambda i, f, te, nt: (tile(i, nt), 0)),
                      pl.BlockSpec((None, D, cf), lambda i, f, te, nt: (te[tile(i, nt)], 0, chunk(i, f, nt))),
                      pl.BlockSpec((None, D, cf), lambda i, f, te, nt: (te[tile(i, nt)], 0, chunk(i, f, nt))),
                      pl.BlockSpec((None, cf, D), lambda i, f, te, nt: (te[tile(i, nt)], chunk(i, f, nt), 0))],
            out_specs=pl.BlockSpec((tm, D), lambda i, f, te, nt: (i, 0)),
            scratch_shapes=[pltpu.VMEM((tm, D), BF16)]),
        out_shape=jax.ShapeDtypeStruct((r_max, D), F32),
        compiler_params=pltpu.CompilerParams(
            dimension_semantics=("arbitrary", "arbitrary"), vmem_limit_bytes=VMEM_LIMIT_BYTES),
        name="expert_ffn",
    )(te, nt, xs, w1, w3, w2)


def _combine_kernel(alpha, dest_ref, x_ref, p_ref, gate_ref, ys_hbm, ln_g_ref, ln_b_ref, gate_w_ref, ple_w_ref,
                    o_ref, y1_ref, y2_ref, sem):
    i = pl.program_id(0)
    tc = x_ref.shape[0]
    n_tok = pl.num_programs(0) * tc
    base = i * tc

    def issue(r, c):
        _row_copy(ys_hbm, dest_ref[base + r], y1_ref, r, sem).start()
        _row_copy(ys_hbm, dest_ref[n_tok + base + r], y2_ref, r, sem).start()
        return c

    lax.fori_loop(0, tc, issue, 0)
    pltpu.make_async_copy(ys_hbm.at[pl.ds(0, tc), :], y1_ref, sem).wait()
    pltpu.make_async_copy(ys_hbm.at[pl.ds(0, tc), :], y2_ref, sem).wait()

    g = gate_ref[...]
    f = g[:, 0:1] * y1_ref[...] + g[:, 1:2] * y2_ref[...]
    o_ref[...] = _norm_and_embed(alpha, x_ref[...], f, p_ref[...], ln_g_ref[...], ln_b_ref[...],
                                 gate_w_ref[...], ple_w_ref[...])


def _combine_layer(alpha, x, p, gates, dest, ys, ln_g, ln_b, gate_w, ple_w):
    T, D = x.shape
    tc = TC_COMBINE
    row = lambda v: v.reshape(1, -1)
    consts = [row(ln_g), row(ln_b), gate_w.astype(BF16), ple_w.astype(BF16)]
    return pl.pallas_call(
        functools.partial(_combine_kernel, alpha),
        grid_spec=pltpu.PrefetchScalarGridSpec(
            num_scalar_prefetch=1, grid=(T // tc,),
            in_specs=[pl.BlockSpec((tc, D), lambda i, d: (i, 0)),
                      pl.BlockSpec((tc, D_PLE), lambda i, d: (i, 0)),
                      pl.BlockSpec((tc, TOP_K), lambda i, d: (i, 0)),
                      pl.BlockSpec(memory_space=pl.ANY)]
                     + [pl.BlockSpec(c.shape, lambda i, d: (0, 0), pipeline_mode=pl.Buffered(1)) for c in consts],
            out_specs=pl.BlockSpec((tc, D), lambda i, d: (i, 0)),
            scratch_shapes=[pltpu.VMEM((tc, D), F32), pltpu.VMEM((tc, D), F32), pltpu.SemaphoreType.DMA(())]),
        out_shape=jax.ShapeDtypeStruct((T, D), F32),
        compiler_params=pltpu.CompilerParams(
            dimension_semantics=("arbitrary",), vmem_limit_bytes=VMEM_LIMIT_BYTES),
        name="combine",
    )(dest, x, p, gates, ys, *consts)


def _moe_layer(alpha, x, p, router_w, w1, w3, w2, ln_g, ln_b, gate_w, ple_w):
    T, D = x.shape
    tm = TM_GROUP
    n_assign = T * TOP_K
    max_tiles = n_assign // tm + N_EXPERTS
    r_max = max_tiles * tm

    route_i, route_g, counts = _router(x, router_w)
    idx = route_i[0:TOP_K]
    rank = route_i[TOP_K:2 * TOP_K]
    gates = route_g[0:TOP_K].T

    cnt = counts[:, 0]
    tiles_e = (cnt + tm - 1) // tm
    tile_end = jnp.cumsum(tiles_e)
    row_start = (tile_end - tiles_e) * tm
    nt = tile_end[-1]
    dest = (row_start[idx] + rank).astype(jnp.int32)
    tile_ids = jnp.minimum(jnp.arange(max_tiles, dtype=jnp.int32), nt - 1)
    te = jnp.sum((tile_end[None, :] <= tile_ids[:, None]).astype(jnp.int32), axis=1)
    te = jnp.minimum(te, N_EXPERTS - 1)
    tok = jnp.broadcast_to(jnp.arange(T, dtype=jnp.int32), (TOP_K, T))
    src = jnp.zeros((r_max,), jnp.int32).at[dest.reshape(-1)].set(tok.reshape(-1))
    nt1 = nt.reshape(1).astype(jnp.int32)

    xs = _dispatch(x, src, nt1 * tm, r_max)
    ys = _expert_ffn(xs, te, nt1, w1.astype(BF16), w3.astype(BF16), w2.astype(BF16))
    return _combine_layer(alpha, x, p, gates, dest.reshape(-1), ys, ln_g, ln_b, gate_w, ple_w)


def kernel(x, p, w_in, pool_w, pool_scale, conv_w, conv_b, conv_ln_g, conv_ln_b, conv_pw, w_out, ln1_g, ln1_b,
           dense_w1, dense_w3, dense_w2, router_w, exp_w1, exp_w3, exp_w2, ln2_g, ln2_b, ple_gate_w, ple_w):
    depth = w_in.shape[0]
    alpha = (2.0 * depth) ** 0.25
    B, S, D = x.shape
    for i in range(depth):
        x = _mixer(alpha, x, w_in[i], pool_w[i], pool_scale[i], conv_w[i], conv_b[i], conv_ln_g[i],
                   conv_ln_b[i], conv_pw[i], w_out[i], ln1_g[i], ln1_b[i])
        xt = x.reshape(B * S, D)
        pt = p[i].reshape(B * S, D_PLE)
        j = i // 2
        if i % 2 == 0:
            xt = _dense_layer(alpha, xt, pt, dense_w1[j], dense_w3[j], dense_w2[j], ln2_g[i], ln2_b[i],
                              ple_gate_w[i], ple_w[i])
        else:
            xt = _moe_layer(alpha, xt, pt, router_w[j], exp_w1[j], exp_w3[j], exp_w2[j], ln2_g[i], ln2_b[i],
                            ple_gate_w[i], ple_w[i])
        x = xt.reshape(B, S, D)
    return x
```

```python
import functools

import jax
import jax.numpy as jnp
from jax import lax
from jax.experimental import pallas as pl
from jax.experimental.pallas import tpu as pltpu

D_MODEL = 1024
D_PLE = 256
D_POOL = 512
D_CONV = D_MODEL - D_POOL
POOL_WINDOWS = (2, 4, 8, 16)
POOL_GROUP_DIM = D_POOL // len(POOL_WINDOWS)
CONV_WIDTH = 31
D_IN_PROJ = D_POOL + 2 * D_CONV
N_EXPERTS = 8
TOP_K = 2
LN_EPS = 1e-5

F32 = jnp.float32
BF16 = jnp.bfloat16

SUBLANES = 8
LANES = 128
VMEM_LIMIT_BYTES = 56 * 1024 * 1024

HALO = 32
TS_MIX = 512
RB_CONV = 64
TM_DENSE = 512
FF_CHUNK_DENSE = 1024
TR_ROUTE = 512
TM_GROUP = 1024
FF_CHUNK_MOE = 512
TG_GATHER = 512
TC_COMBINE = 256


def _sigmoid(z):
    return 1.0 / (1.0 + jnp.exp(-z))


def _layer_norm(h, g, b):
    mu = jnp.mean(h, axis=-1, keepdims=True)
    c = h - mu
    var = jnp.mean(c * c, axis=-1, keepdims=True)
    return c * lax.rsqrt(var + LN_EPS) * g + b


def _dot(a, b):
    return jnp.dot(a, b, preferred_element_type=F32)


def _mixer_kernel(alpha, x_ref, w_in_ref, pool_w_ref, pool_scale_ref, conv_w_ref, conv_b_ref,
                  cln_g_ref, cln_b_ref, conv_pw_ref, w_out_ref, ln_g_ref, ln_b_ref,
                  o_ref, p_scr, v_scr, y_scr):
    ts = x_ref.shape[0]
    s = pl.program_id(1)

    @pl.when(s == 0)
    def _():
        p_scr[0:HALO, :] = jnp.zeros((HALO, D_POOL), F32)
        v_scr[0:HALO, :] = jnp.zeros((HALO, D_CONV), F32)

    x = x_ref[...]
    u = _dot(x.astype(BF16), w_in_ref[...])
    p_scr[HALO:, :] = u[:, :D_POOL]
    v_scr[HALO:, :] = u[:, D_POOL:D_POOL + D_CONV] * _sigmoid(u[:, D_POOL + D_CONV:])

    pos = s * ts + lax.broadcasted_iota(jnp.int32, (ts, POOL_GROUP_DIM), 0)
    y_parts = []
    for g, w in enumerate(POOL_WINDOWS):
        lanes = slice(g * POOL_GROUP_DIM, (g + 1) * POOL_GROUP_DIM)
        pe = p_scr[:, lanes]
        acc = pe
        step = 1
        while step < w:
            acc = acc + pltpu.roll(acc, step, axis=0)
            step *= 2
        cnt = jnp.minimum(pos + 1, w).astype(F32)
        d = acc[HALO:, :] / cnt - pe[HALO:, :]
        yg = _dot(d.astype(BF16), pool_w_ref[g]) * pool_scale_ref[:, lanes]
        y_parts.append(yg.astype(BF16))

    n_lane_chunks = D_CONV // LANES

    def conv_block(rb, carry):
        r0 = pl.multiple_of(rb * RB_CONV, RB_CONV)
        cols = []
        for lc in range(n_lane_chunks):
            lanes = slice(lc * LANES, (lc + 1) * LANES)
            e = v_scr[pl.ds(r0, RB_CONV + HALO), lanes]
            acc = jnp.zeros((RB_CONV, LANES), F32)
            for b in range(SUBLANES):
                eb = e if b == 0 else pltpu.roll(e, b, axis=0)
                for a in range(HALO // SUBLANES):
                    dist = SUBLANES * a + b
                    if dist > CONV_WIDTH - 1:
                        continue
                    k = CONV_WIDTH - 1 - dist
                    lo = HALO - SUBLANES * a
                    acc = acc + eb[lo:lo + RB_CONV, :] * conv_w_ref[k:k + 1, lanes]
            cols.append(acc)
        y = jnp.concatenate(cols, axis=-1) + conv_b_ref[...]
        z = _layer_norm(y, cln_g_ref[...], cln_b_ref[...])
        y_scr[pl.ds(r0, RB_CONV), :] = (z * _sigmoid(z)).astype(BF16)
        return carry

    lax.fori_loop(0, ts // RB_CONV, conv_block, 0)
    y_conv = _dot(y_scr[...], conv_pw_ref[...])
    y_parts.append(y_conv.astype(BF16))

    mix = _dot(jnp.concatenate(y_parts, axis=-1), w_out_ref[...])
    o_ref[...] = _layer_norm(alpha * x + mix, ln_g_ref[...], ln_b_ref[...])

    p_scr[0:HALO, :] = p_scr[ts:ts + HALO, :]
    v_scr[0:HALO, :] = v_scr[ts:ts + HALO, :]


def _const_spec(shape):
    nd = len(shape)
    return pl.BlockSpec(shape, lambda *_: (0,) * nd, pipeline_mode=pl.Buffered(1))


def _mixer(alpha, x, w_in, pool_w, pool_scale, conv_w, conv_b, cln_g, cln_b, conv_pw, w_out, ln_g, ln_b):
    B, S, D = x.shape
    ts = TS_MIX
    row = lambda v: v.reshape(1, -1)
    consts = [w_in.astype(BF16), pool_w.astype(BF16), row(pool_scale), conv_w, row(conv_b),
              row(cln_g), row(cln_b), conv_pw.astype(BF16), w_out.astype(BF16), row(ln_g), row(ln_b)]
    return pl.pallas_call(
        functools.partial(_mixer_kernel, alpha),
        grid=(B, S // ts),
        in_specs=[pl.BlockSpec((None, ts, D), lambda b, s: (b, s, 0))] + [_const_spec(c.shape) for c in consts],
        out_specs=pl.BlockSpec((None, ts, D), lambda b, s: (b, s, 0)),
        out_shape=jax.ShapeDtypeStruct((B, S, D), F32),
        scratch_shapes=[pltpu.VMEM((HALO + ts, D_POOL), F32),
                        pltpu.VMEM((HALO + ts, D_CONV), F32),
                        pltpu.VMEM((ts, D_CONV), BF16)],
        compiler_params=pltpu.CompilerParams(
            dimension_semantics=("arbitrary", "arbitrary"), vmem_limit_bytes=VMEM_LIMIT_BYTES),
        name="mixer",
    )(x, *consts)


def _norm_and_embed(alpha, x, f, p, ln_g, ln_b, gate_w, ple_w):
    h = _layer_norm(alpha * x + f, ln_g, ln_b)
    gate = _sigmoid(_dot(h.astype(BF16), gate_w))
    return h + gate * _dot(p.astype(BF16), ple_w)


def _dense_ffn_kernel(alpha, x_ref, p_ref, w1_ref, w3_ref, w2_ref, ln_g_ref, ln_b_ref, gate_w_ref, ple_w_ref,
                      o_ref, acc_ref):
    x = x_ref[...]
    xb = x.astype(BF16)
    ff = w1_ref.shape[1]
    for c0 in range(0, ff, FF_CHUNK_DENSE):
        c1 = min(c0 + FF_CHUNK_DENSE, ff)
        a = _dot(xb, w1_ref[:, c0:c1])
        b = _dot(xb, w3_ref[:, c0:c1])
        h = (a * _sigmoid(a) * b).astype(BF16)
        part = _dot(h, w2_ref[c0:c1, :])
        if c0 == 0:
            acc_ref[...] = part
        else:
            acc_ref[...] += part
    o_ref[...] = _norm_and_embed(alpha, x, acc_ref[...], p_ref[...], ln_g_ref[...], ln_b_ref[...],
                                 gate_w_ref[...], ple_w_ref[...])


def _dense_layer(alpha, x, p, w1, w3, w2, ln_g, ln_b, gate_w, ple_w):
    T, D = x.shape
    tm = TM_DENSE
    row = lambda v: v.reshape(1, -1)
    consts = [w1.astype(BF16), w3.astype(BF16), w2.astype(BF16), row(ln_g), row(ln_b),
              gate_w.astype(BF16), ple_w.astype(BF16)]
    return pl.pallas_call(
        functools.partial(_dense_ffn_kernel, alpha),
        grid=(T // tm,),
        in_specs=[pl.BlockSpec((tm, D), lambda i: (i, 0)),
                  pl.BlockSpec((tm, D_PLE), lambda i: (i, 0))] + [_const_spec(c.shape) for c in consts],
        out_specs=pl.BlockSpec((tm, D), lambda i: (i, 0)),
        out_shape=jax.ShapeDtypeStruct((T, D), F32),
        scratch_shapes=[pltpu.VMEM((tm, D), F32)],
        compiler_params=pltpu.CompilerParams(
            dimension_semantics=("arbitrary",), vmem_limit_bytes=VMEM_LIMIT_BYTES),
        name="dense_ffn",
    )(x, p, *consts)


def _router_kernel(x_ref, wt_ref, idx_ref, gate_ref, cnt_ref, carry_ref):
    i = pl.program_id(0)
    tr = x_ref.shape[0]

    @pl.when(i == 0)
    def _():
        carry_ref[...] = jnp.zeros_like(carry_ref)

    x = x_ref[...]
    wt = wt_ref[...]
    xh = x.astype(BF16)
    xl = (x - xh.astype(F32)).astype(BF16)
    wh = wt.astype(BF16)
    wl = (wt - wh.astype(F32)).astype(BF16)
    nt_dims = (((1,), (1,)), ((), ()))
    dg = lambda a, b: lax.dot_general(a, b, nt_dims, preferred_element_type=F32)
    logits = dg(wh, xh) + (dg(wh, xl) + dg(wl, xh))

    eid = lax.broadcasted_iota(jnp.int32, logits.shape, 0)
    m1 = jnp.max(logits, axis=0, keepdims=True)
    i1 = jnp.min(jnp.where(logits == m1, eid, N_EXPERTS), axis=0, keepdims=True)
    rest = jnp.where(eid == i1, -jnp.inf, logits)
    m2 = jnp.max(rest, axis=0, keepdims=True)
    i2 = jnp.min(jnp.where(rest == m2, eid, N_EXPERTS), axis=0, keepdims=True)
    e2 = jnp.exp(m2 - m1)
    g1 = 1.0 / (1.0 + e2)
    g2 = e2 / (1.0 + e2)

    oh1 = (eid == i1).astype(F32)
    oh2 = (eid == i2).astype(F32)
    chosen = oh1 + oh2
    r_i = lax.broadcasted_iota(jnp.int32, (tr, tr), 0)
    c_i = lax.broadcasted_iota(jnp.int32, (tr, tr), 1)
    before = (r_i < c_i).astype(BF16)
    excl = _dot(chosen.astype(BF16), before) + carry_ref[:, 0:1]
    rank1 = jnp.sum(oh1 * excl, axis=0, keepdims=True)
    rank2 = jnp.sum(oh2 * excl, axis=0, keepdims=True)

    total = carry_ref[:, 0:1] + jnp.sum(chosen, axis=1, keepdims=True)
    carry_ref[...] = jnp.broadcast_to(total, carry_ref.shape)
    cnt_ref[...] = jnp.broadcast_to(total, cnt_ref.shape).astype(jnp.int32)

    zi = jnp.zeros((SUBLANES - 4, tr), jnp.int32)
    idx_ref[...] = jnp.concatenate(
        [i1, i2, rank1.astype(jnp.int32), rank2.astype(jnp.int32), zi], axis=0)
    gate_ref[...] = jnp.concatenate([g1, g2, jnp.zeros((SUBLANES - 2, tr), F32)], axis=0)


def _router(x, router_w):
    T, D = x.shape
    tr = TR_ROUTE
    return pl.pallas_call(
        _router_kernel,
        grid=(T // tr,),
        in_specs=[pl.BlockSpec((tr, D), lambda i: (i, 0)), _const_spec((N_EXPERTS, D))],
        out_specs=[pl.BlockSpec((SUBLANES, tr), lambda i: (0, i)),
                   pl.BlockSpec((SUBLANES, tr), lambda i: (0, i)),
                   pl.BlockSpec((N_EXPERTS, LANES), lambda i: (0, 0))],
        out_shape=[jax.ShapeDtypeStruct((SUBLANES, T), jnp.int32),
                   jax.ShapeDtypeStruct((SUBLANES, T), F32),
                   jax.ShapeDtypeStruct((N_EXPERTS, LANES), jnp.int32)],
        scratch_shapes=[pltpu.VMEM((N_EXPERTS, LANES), F32)],
        compiler_params=pltpu.CompilerParams(
            dimension_semantics=("arbitrary",), vmem_limit_bytes=VMEM_LIMIT_BYTES),
        name="router",
    )(x, router_w.T)


def _row_copy(src_hbm, src_row, dst, dst_row, sem):
    return pltpu.make_async_copy(src_hbm.at[pl.ds(src_row, 1), :], dst.at[pl.ds(dst_row, 1), :], sem)


def _dispatch_kernel(src_ref, nrows_ref, x_hbm, xs_ref, sem):
    i = pl.program_id(0)
    base = i * TG_GATHER

    @pl.when(base < nrows_ref[0])
    def _():
        def issue(r, c):
            _row_copy(x_hbm, src_ref[base + r], xs_ref, r, sem).start()
            return c

        lax.fori_loop(0, TG_GATHER, issue, 0)
        pltpu.make_async_copy(x_hbm.at[pl.ds(0, TG_GATHER), :], xs_ref, sem).wait()

    @pl.when(base >= nrows_ref[0])
    def _():
        xs_ref[...] = jnp.zeros_like(xs_ref)


def _dispatch(x, src, nrows, r_max):
    T, D = x.shape
    return pl.pallas_call(
        _dispatch_kernel,
        grid_spec=pltpu.PrefetchScalarGridSpec(
            num_scalar_prefetch=2, grid=(r_max // TG_GATHER,),
            in_specs=[pl.BlockSpec(memory_space=pl.ANY)],
            out_specs=pl.BlockSpec((TG_GATHER, D), lambda i, src, n: (i, 0)),
            scratch_shapes=[pltpu.SemaphoreType.DMA(())]),
        out_shape=jax.ShapeDtypeStruct((r_max, D), F32),
        compiler_params=pltpu.CompilerParams(dimension_semantics=("arbitrary",)),
        name="dispatch",
    )(src, nrows, x)


def _expert_ffn_kernel(te_ref, nt_ref, xs_ref, w1_ref, w3_ref, w2_ref, o_ref, xb_ref):
    i = pl.program_id(0)
    f = pl.program_id(1)

    @pl.when(i < nt_ref[0])
    def _():
        @pl.when(f == 0)
        def _():
            xb_ref[...] = xs_ref[...].astype(BF16)

        xb = xb_ref[...]
        a = _dot(xb, w1_ref[...])
        b = _dot(xb, w3_ref[...])
        h = (a * _sigmoid(a) * b).astype(BF16)
        part = _dot(h, w2_ref[...])

        @pl.when(f == 0)
        def _():
            o_ref[...] = part

        @pl.when(f > 0)
        def _():
            o_ref[...] += part

    @pl.when((i >= nt_ref[0]) & (f == 0))
    def _():
        o_ref[...] = jnp.zeros_like(o_ref)


def _expert_ffn(xs, te, nt, w1, w3, w2):
    r_max, D = xs.shape
    E, _, FF = w1.shape
    tm, cf = TM_GROUP, FF_CHUNK_MOE
    nf = FF // cf

    def tile(i, nt_ref):
        return jnp.minimum(i, nt_ref[0] - 1)

    def chunk(i, f, nt_ref):
        return jnp.where(i < nt_ref[0], f, nf - 1)

    return pl.pallas_call(
        _expert_ffn_kernel,
        grid_spec=pltpu.PrefetchScalarGridSpec(
            num_scalar_prefetch=2, grid=(r_max // tm, nf),
            in_specs=[pl.BlockSpec((tm, D), lambda i, f, te, nt: (tile(i, nt), 0)),
                      pl.BlockSpec((None, D, cf), lambda i, f, te, nt: (te[tile(i, nt)], 0, chunk(i, f, nt))),
                      pl.BlockSpec((None, D, cf), lambda i, f, te, nt: (te[tile(i, nt)], 0, chunk(i, f, nt))),
                      pl.BlockSpec((None, cf, D), lambda i, f, te, nt: (te[tile(i, nt)], chunk(i, f, nt), 0))],
            out_specs=pl.BlockSpec((tm, D), lambda i, f, te, nt: (i, 0)),
            scratch_shapes=[pltpu.VMEM((tm, D), BF16)]),
        out_shape=jax.ShapeDtypeStruct((r_max, D), F32),
        compiler_params=pltpu.CompilerParams(
            dimension_semantics=("arbitrary", "arbitrary"), vmem_limit_bytes=VMEM_LIMIT_BYTES),
        name="expert_ffn",
    )(te, nt, xs, w1, w3, w2)


def _combine_kernel(alpha, dest_ref, x_ref, p_ref, gate_ref, ys_hbm, ln_g_ref, ln_b_ref, gate_w_ref, ple_w_ref,
                    o_ref, y1_ref, y2_ref, sem):
    i = pl.program_id(0)
    tc = x_ref.shape[0]
    n_tok = pl.num_programs(0) * tc
    base = i * tc

    def issue(r, c):
        _row_copy(ys_hbm, dest_ref[base + r], y1_ref, r, sem).start()
        _row_copy(ys_hbm, dest_ref[n_tok + base + r], y2_ref, r, sem).start()
        return c

    lax.fori_loop(0, tc, issue, 0)
    pltpu.make_async_copy(ys_hbm.at[pl.ds(0, tc), :], y1_ref, sem).wait()
    pltpu.make_async_copy(ys_hbm.at[pl.ds(0, tc), :], y2_ref, sem).wait()

    g = gate_ref[...]
    f = g[:, 0:1] * y1_ref[...] + g[:, 1:2] * y2_ref[...]
    o_ref[...] = _norm_and_embed(alpha, x_ref[...], f, p_ref[...], ln_g_ref[...], ln_b_ref[...],
                                 gate_w_ref[...], ple_w_ref[...])


def _combine_layer(alpha, x, p, gates, dest, ys, ln_g, ln_b, gate_w, ple_w):
    T, D = x.shape
    tc = TC_COMBINE
    row = lambda v: v.reshape(1, -1)
    consts = [row(ln_g), row(ln_b), gate_w.astype(BF16), ple_w.astype(BF16)]
    return pl.pallas_call(
        functools.partial(_combine_kernel, alpha),
        grid_spec=pltpu.PrefetchScalarGridSpec(
            num_scalar_prefetch=1, grid=(T // tc,),
            in_specs=[pl.BlockSpec((tc, D), lambda i, d: (i, 0)),
                      pl.BlockSpec((tc, D_PLE), lambda i, d: (i, 0)),
                      pl.BlockSpec((tc, TOP_K), lambda i, d: (i, 0)),
                      pl.BlockSpec(memory_space=pl.ANY)]
                     + [pl.BlockSpec(c.shape, lambda i, d: (0, 0), pipeline_mode=pl.Buffered(1)) for c in consts],
            out_specs=pl.BlockSpec((tc, D), lambda i, d: (i, 0)),
            scratch_shapes=[pltpu.VMEM((tc, D), F32), pltpu.VMEM((tc, D), F32), pltpu.SemaphoreType.DMA(())]),
        out_shape=jax.ShapeDtypeStruct((T, D), F32),
        compiler_params=pltpu.CompilerParams(
            dimension_semantics=("arbitrary",), vmem_limit_bytes=VMEM_LIMIT_BYTES),
        name="combine",
    )(dest, x, p, gates, ys, *consts)


def _moe_layer(alpha, x, p, router_w, w1, w3, w2, ln_g, ln_b, gate_w, ple_w):
    T, D = x.shape
    tm = TM_GROUP
    n_assign = T * TOP_K
    max_tiles = n_assign // tm + N_EXPERTS
    r_max = max_tiles * tm

    route_i, route_g, counts = _router(x, router_w)
    idx = route_i[0:TOP_K]
    rank = route_i[TOP_K:2 * TOP_K]
    gates = route_g[0:TOP_K].T

    cnt = counts[:, 0]
    tiles_e = (cnt + tm - 1) // tm
    tile_end = jnp.cumsum(tiles_e)
    row_start = (tile_end - tiles_e) * tm
    nt = tile_end[-1]
    start_of = jnp.sum(jnp.where(idx[:, :, None] == jnp.arange(N_EXPERTS), row_start, 0), axis=-1)
    dest = (start_of + rank).astype(jnp.int32)
    tile_ids = jnp.minimum(jnp.arange(max_tiles, dtype=jnp.int32), nt - 1)
    te = jnp.sum((tile_end[None, :] <= tile_ids[:, None]).astype(jnp.int32), axis=1)
    te = jnp.minimum(te, N_EXPERTS - 1)
    tok = jnp.broadcast_to(jnp.arange(T, dtype=jnp.int32), (TOP_K, T))
    src = jnp.zeros((r_max,), jnp.int32).at[dest.reshape(-1)].set(tok.reshape(-1))
    nt1 = nt.reshape(1).astype(jnp.int32)

    xs = _dispatch(x, src, nt1 * tm, r_max)
    ys = _expert_ffn(xs, te, nt1, w1.astype(BF16), w3.astype(BF16), w2.astype(BF16))
    return _combine_layer(alpha, x, p, gates, dest.reshape(-1), ys, ln_g, ln_b, gate_w, ple_w)


def kernel(x, p, w_in, pool_w, pool_scale, conv_w, conv_b, conv_ln_g, conv_ln_b, conv_pw, w_out, ln1_g, ln1_b,
           dense_w1, dense_w3, dense_w2, router_w, exp_w1, exp_w3, exp_w2, ln2_g, ln2_b, ple_gate_w, ple_w):
    depth = w_in.shape[0]
    alpha = (2.0 * depth) ** 0.25
    B, S, D = x.shape
    for i in range(depth):
        x = _mixer(alpha, x, w_in[i], pool_w[i], pool_scale[i], conv_w[i], conv_b[i], conv_ln_g[i],
                   conv_ln_b[i], conv_pw[i], w_out[i], ln1_g[i], ln1_b[i])
        xt = x.reshape(B * S, D)
        pt = p[i].reshape(B * S, D_PLE)
        j = i // 2
        if i % 2 == 0:
            xt = _dense_layer(alpha, xt, pt, dense_w1[j], dense_w3[j], dense_w2[j], ln2_g[i], ln2_b[i],
                              ple_gate_w[i], ple_w[i])
        else:
            xt = _moe_layer(alpha, xt, pt, router_w[j], exp_w1[j], exp_w3[j], exp_w2[j], ln2_g[i], ln2_b[i],
                            ple_gate_w[i], ple_w[i])
        x = xt.reshape(B, S, D)
    return x
```

```python
import functools

import jax
import jax.numpy as jnp
from jax import lax
from jax.experimental import pallas as pl
from jax.experimental.pallas import tpu as pltpu

D_MODEL = 1024
D_PLE = 256
D_POOL = 512
D_CONV = D_MODEL - D_POOL
POOL_WINDOWS = (2, 4, 8, 16)
POOL_GROUP_DIM = D_POOL // len(POOL_WINDOWS)
CONV_WIDTH = 31
D_IN_PROJ = D_POOL + 2 * D_CONV
N_EXPERTS = 8
TOP_K = 2
LN_EPS = 1e-5

F32 = jnp.float32
BF16 = jnp.bfloat16

SUBLANES = 8
LANES = 128
VMEM_LIMIT_BYTES = 56 * 1024 * 1024

HALO = 32
TS_MIX = 512
RB_CONV = 64
TM_DENSE = 512
FF_CHUNK_DENSE = 1024
TR_ROUTE = 512
TM_GROUP = 1024
FF_CHUNK_MOE = 512
TC_DISPATCH = 256
TC_COMBINE = 256
ROW_CHUNKS = D_MODEL // LANES
ISSUE_UNROLL = 8


def _sigmoid(z):
    return 1.0 / (1.0 + jnp.exp(-z))


def _layer_norm(h, g, b):
    mu = jnp.mean(h, axis=-1, keepdims=True)
    c = h - mu
    var = jnp.mean(c * c, axis=-1, keepdims=True)
    return c * lax.rsqrt(var + LN_EPS) * g + b


def _dot(a, b):
    return jnp.dot(a, b, preferred_element_type=F32)


def _mixer_kernel(alpha, x_ref, w_in_ref, pool_w_ref, pool_scale_ref, conv_w_ref, conv_b_ref,
                  cln_g_ref, cln_b_ref, conv_pw_ref, w_out_ref, ln_g_ref, ln_b_ref,
                  o_ref, p_scr, v_scr, y_scr):
    ts = x_ref.shape[0]
    s = pl.program_id(1)

    @pl.when(s == 0)
    def _():
        p_scr[0:HALO, :] = jnp.zeros((HALO, D_POOL), F32)
        v_scr[0:HALO, :] = jnp.zeros((HALO, D_CONV), F32)

    x = x_ref[...]
    u = _dot(x.astype(BF16), w_in_ref[...])
    p_scr[HALO:, :] = u[:, :D_POOL]
    v_scr[HALO:, :] = u[:, D_POOL:D_POOL + D_CONV] * _sigmoid(u[:, D_POOL + D_CONV:])

    pos = s * ts + lax.broadcasted_iota(jnp.int32, (ts, POOL_GROUP_DIM), 0)
    y_parts = []
    for g, w in enumerate(POOL_WINDOWS):
        lanes = slice(g * POOL_GROUP_DIM, (g + 1) * POOL_GROUP_DIM)
        pe = p_scr[:, lanes]
        acc = pe
        step = 1
        while step < w:
            acc = acc + pltpu.roll(acc, step, axis=0)
            step *= 2
        cnt = jnp.minimum(pos + 1, w).astype(F32)
        d = acc[HALO:, :] / cnt - pe[HALO:, :]
        yg = _dot(d.astype(BF16), pool_w_ref[g]) * pool_scale_ref[:, lanes]
        y_parts.append(yg.astype(BF16))

    n_lane_chunks = D_CONV // LANES

    def conv_block(rb, carry):
        r0 = pl.multiple_of(rb * RB_CONV, RB_CONV)
        cols = []
        for lc in range(n_lane_chunks):
            lanes = slice(lc * LANES, (lc + 1) * LANES)
            e = v_scr[pl.ds(r0, RB_CONV + HALO), lanes]
            acc = jnp.zeros((RB_CONV, LANES), F32)
            for b in range(SUBLANES):
                eb = e if b == 0 else pltpu.roll(e, b, axis=0)
                for a in range(HALO // SUBLANES):
                    dist = SUBLANES * a + b
                    if dist > CONV_WIDTH - 1:
                        continue
                    k = CONV_WIDTH - 1 - dist
                    lo = HALO - SUBLANES * a
                    acc = acc + eb[lo:lo + RB_CONV, :] * conv_w_ref[k:k + 1, lanes]
            cols.append(acc)
        y = jnp.concatenate(cols, axis=-1) + conv_b_ref[...]
        z = _layer_norm(y, cln_g_ref[...], cln_b_ref[...])
        y_scr[pl.ds(r0, RB_CONV), :] = (z * _sigmoid(z)).astype(BF16)
        return carry

    lax.fori_loop(0, ts // RB_CONV, conv_block, 0)
    y_conv = _dot(y_scr[...], conv_pw_ref[...])
    y_parts.append(y_conv.astype(BF16))

    mix = _dot(jnp.concatenate(y_parts, axis=-1), w_out_ref[...])
    o_ref[...] = _layer_norm(alpha * x + mix, ln_g_ref[...], ln_b_ref[...])

    p_scr[0:HALO, :] = p_scr[ts:ts + HALO, :]
    v_scr[0:HALO, :] = v_scr[ts:ts + HALO, :]


def _const_spec(shape):
    nd = len(shape)
    return pl.BlockSpec(shape, lambda *_: (0,) * nd, pipeline_mode=pl.Buffered(1))


def _mixer(alpha, x, w_in, pool_w, pool_scale, conv_w, conv_b, cln_g, cln_b, conv_pw, w_out, ln_g, ln_b):
    B, S, D = x.shape
    ts = TS_MIX
    row = lambda v: v.reshape(1, -1)
    consts = [w_in.astype(BF16), pool_w.astype(BF16), row(pool_scale), conv_w, row(conv_b),
              row(cln_g), row(cln_b), conv_pw.astype(BF16), w_out.astype(BF16), row(ln_g), row(ln_b)]
    return pl.pallas_call(
        functools.partial(_mixer_kernel, alpha),
        grid=(B, S // ts),
        in_specs=[pl.BlockSpec((None, ts, D), lambda b, s: (b, s, 0))] + [_const_spec(c.shape) for c in consts],
        out_specs=pl.BlockSpec((None, ts, D), lambda b, s: (b, s, 0)),
        out_shape=jax.ShapeDtypeStruct((B, S, D), F32),
        scratch_shapes=[pltpu.VMEM((HALO + ts, D_POOL), F32),
                        pltpu.VMEM((HALO + ts, D_CONV), F32),
                        pltpu.VMEM((ts, D_CONV), BF16)],
        compiler_params=pltpu.CompilerParams(
            dimension_semantics=("arbitrary", "arbitrary"), vmem_limit_bytes=VMEM_LIMIT_BYTES),
        name="mixer",
    )(x, *consts)


def _norm_and_embed(alpha, x, f, p, ln_g, ln_b, gate_w, ple_w):
    h = _layer_norm(alpha * x + f, ln_g, ln_b)
    gate = _sigmoid(_dot(h.astype(BF16), gate_w))
    return h + gate * _dot(p.astype(BF16), ple_w)


def _dense_ffn_kernel(alpha, x_ref, p_ref, w1_ref, w3_ref, w2_ref, ln_g_ref, ln_b_ref, gate_w_ref, ple_w_ref,
                      o_ref, acc_ref):
    x = x_ref[...]
    xb = x.astype(BF16)
    ff = w1_ref.shape[1]
    for c0 in range(0, ff, FF_CHUNK_DENSE):
        c1 = min(c0 + FF_CHUNK_DENSE, ff)
        a = _dot(xb, w1_ref[:, c0:c1])
        b = _dot(xb, w3_ref[:, c0:c1])
        h = (a * _sigmoid(a) * b).astype(BF16)
        part = _dot(h, w2_ref[c0:c1, :])
        if c0 == 0:
            acc_ref[...] = part
        else:
            acc_ref[...] += part
    o_ref[...] = _norm_and_embed(alpha, x, acc_ref[...], p_ref[...], ln_g_ref[...], ln_b_ref[...],
                                 gate_w_ref[...], ple_w_ref[...])


def _dense_layer(alpha, x, p, w1, w3, w2, ln_g, ln_b, gate_w, ple_w):
    T, D = x.shape
    tm = TM_DENSE
    row = lambda v: v.reshape(1, -1)
    consts = [w1.astype(BF16), w3.astype(BF16), w2.astype(BF16), row(ln_g), row(ln_b),
              gate_w.astype(BF16), ple_w.astype(BF16)]
    return pl.pallas_call(
        functools.partial(_dense_ffn_kernel, alpha),
        grid=(T // tm,),
        in_specs=[pl.BlockSpec((tm, D), lambda i: (i, 0)),
                  pl.BlockSpec((tm, D_PLE), lambda i: (i, 0))] + [_const_spec(c.shape) for c in consts],
        out_specs=pl.BlockSpec((tm, D), lambda i: (i, 0)),
        out_shape=jax.ShapeDtypeStruct((T, D), F32),
        scratch_shapes=[pltpu.VMEM((tm, D), F32)],
        compiler_params=pltpu.CompilerParams(
            dimension_semantics=("arbitrary",), vmem_limit_bytes=VMEM_LIMIT_BYTES),
        name="dense_ffn",
    )(x, p, *consts)


def _router_kernel(x_ref, wt_ref, idx_ref, gate_ref, cnt_ref, carry_ref):
    i = pl.program_id(0)
    tr = x_ref.shape[0]

    @pl.when(i == 0)
    def _():
        carry_ref[...] = jnp.zeros_like(carry_ref)

    x = x_ref[...]
    wt = wt_ref[...]
    xh = x.astype(BF16)
    xl = (x - xh.astype(F32)).astype(BF16)
    wh = wt.astype(BF16)
    wl = (wt - wh.astype(F32)).astype(BF16)
    nt_dims = (((1,), (1,)), ((), ()))
    dg = lambda a, b: lax.dot_general(a, b, nt_dims, preferred_element_type=F32)
    logits = dg(wh, xh) + (dg(wh, xl) + dg(wl, xh))

    eid = lax.broadcasted_iota(jnp.int32, logits.shape, 0)
    m1 = jnp.max(logits, axis=0, keepdims=True)
    i1 = jnp.min(jnp.where(logits == m1, eid, N_EXPERTS), axis=0, keepdims=True)
    rest = jnp.where(eid == i1, -jnp.inf, logits)
    m2 = jnp.max(rest, axis=0, keepdims=True)
    i2 = jnp.min(jnp.where(rest == m2, eid, N_EXPERTS), axis=0, keepdims=True)
    e2 = jnp.exp(m2 - m1)
    g1 = 1.0 / (1.0 + e2)
    g2 = e2 / (1.0 + e2)

    oh1 = (eid == i1).astype(F32)
    oh2 = (eid == i2).astype(F32)
    chosen = oh1 + oh2
    r_i = lax.broadcasted_iota(jnp.int32, (tr, tr), 0)
    c_i = lax.broadcasted_iota(jnp.int32, (tr, tr), 1)
    before = (r_i < c_i).astype(BF16)
    excl = _dot(chosen.astype(BF16), before) + carry_ref[:, 0:1]
    rank1 = jnp.sum(oh1 * excl, axis=0, keepdims=True)
    rank2 = jnp.sum(oh2 * excl, axis=0, keepdims=True)

    total = carry_ref[:, 0:1] + jnp.sum(chosen, axis=1, keepdims=True)
    carry_ref[...] = jnp.broadcast_to(total, carry_ref.shape)
    cnt_ref[...] = jnp.broadcast_to(total, cnt_ref.shape).astype(jnp.int32)

    zi = jnp.zeros((SUBLANES - 4, tr), jnp.int32)
    idx_ref[...] = jnp.concatenate(
        [i1, i2, rank1.astype(jnp.int32), rank2.astype(jnp.int32), zi], axis=0)
    gate_ref[...] = jnp.concatenate([g1, g2, jnp.zeros((SUBLANES - 2, tr), F32)], axis=0)


def _router(x, router_w):
    T, D = x.shape
    tr = TR_ROUTE
    return pl.pallas_call(
        _router_kernel,
        grid=(T // tr,),
        in_specs=[pl.BlockSpec((tr, D), lambda i: (i, 0)), _const_spec((N_EXPERTS, D))],
        out_specs=[pl.BlockSpec((SUBLANES, tr), lambda i: (0, i)),
                   pl.BlockSpec((SUBLANES, tr), lambda i: (0, i)),
                   pl.BlockSpec((N_EXPERTS, LANES), lambda i: (0, 0))],
        out_shape=[jax.ShapeDtypeStruct((SUBLANES, T), jnp.int32),
                   jax.ShapeDtypeStruct((SUBLANES, T), F32),
                   jax.ShapeDtypeStruct((N_EXPERTS, LANES), jnp.int32)],
        scratch_shapes=[pltpu.VMEM((N_EXPERTS, LANES), F32)],
        compiler_params=pltpu.CompilerParams(
            dimension_semantics=("arbitrary",), vmem_limit_bytes=VMEM_LIMIT_BYTES),
        name="router",
    )(x, router_w.T)


def _store_token_tiles(dst_ref, value):
    rows = value.shape[0]
    for c in range(ROW_CHUNKS):
        dst_ref[pl.ds(c, rows, stride=ROW_CHUNKS), :] = value[:, c * LANES:(c + 1) * LANES]


def _load_token_tile_chunk(src_ref, rows, c):
    return src_ref[pl.ds(c, rows, stride=ROW_CHUNKS), :]


def _load_token_tiles(src_ref, rows):
    return jnp.concatenate([_load_token_tile_chunk(src_ref, rows, c) for c in range(ROW_CHUNKS)], axis=-1)


def _tile_rows(row):
    return pl.ds(pl.multiple_of(row * ROW_CHUNKS, ROW_CHUNKS), ROW_CHUNKS)


def _dispatch_kernel(dest_ref, x_ref, xs_hbm, xt_ref, sems):
    i = pl.program_id(0)
    n = pl.num_programs(0)
    tc = x_ref.shape[0]
    n_tok = n * tc
    base = i * tc
    slot = i % 2
    stage = xt_ref.at[slot]

    _store_token_tiles(stage, x_ref[...])

    def copy(k, r):
        d = dest_ref[k * n_tok + base + r]
        return pltpu.make_async_copy(stage.at[_tile_rows(r), :], xs_hbm.at[_tile_rows(d), :], sems.at[slot])

    def issue(rb, c):
        for j in range(ISSUE_UNROLL):
            r = rb * ISSUE_UNROLL + j
            for k in range(TOP_K):
                copy(k, r).start()
        return c

    lax.fori_loop(0, tc // ISSUE_UNROLL, issue, 0)

    def wait_step(s):
        for _ in range(TOP_K):
            pltpu.make_async_copy(xt_ref.at[s], xs_hbm.at[pl.ds(0, tc * ROW_CHUNKS), :], sems.at[s]).wait()

    @pl.when(i > 0)
    def _():
        wait_step(1 - slot)

    @pl.when(i == n - 1)
    def _():
        wait_step(slot)


def _dispatch(x, dest):
    T, D = x.shape
    tc = TC_DISPATCH
    return pl.pallas_call(
        _dispatch_kernel,
        grid_spec=pltpu.PrefetchScalarGridSpec(
            num_scalar_prefetch=1, grid=(T // tc,),
            in_specs=[pl.BlockSpec((tc, D), lambda i, d: (i, 0))],
            out_specs=pl.BlockSpec(memory_space=pl.ANY),
            scratch_shapes=[pltpu.VMEM((2, tc * ROW_CHUNKS, LANES), F32), pltpu.SemaphoreType.DMA((2,))]),
        out_shape=jax.ShapeDtypeStruct((TOP_K * T * ROW_CHUNKS, LANES), F32),
        compiler_params=pltpu.CompilerParams(dimension_semantics=("arbitrary",)),
        name="dispatch",
    )(dest, x)


def _expert_ffn_kernel(tu_ref, eu_ref, lo_ref, hi_ref, first_ref, last_ref, nu_ref,
                       xs_ref, w1_ref, w3_ref, w2_ref, o_ref, xb_ref, acc_ref):
    u = pl.program_id(0)
    f = pl.program_id(1)
    nf = pl.num_programs(1)
    tm = acc_ref.shape[0]

    @pl.when(u < nu_ref[0])
    def _():
        @pl.when(f == 0)
        def _():
            row = lax.broadcasted_iota(jnp.int32, (tm, LANES), 0)
            lo = lo_ref[u]
            hi = hi_ref[u]
            for c in range(ROW_CHUNKS):
                xc = _load_token_tile_chunk(xs_ref, tm, c)
                xc = jnp.where(row >= lo, jnp.where(row < hi, xc, 0.0), 0.0)
                xb_ref[:, c * LANES:(c + 1) * LANES] = xc.astype(BF16)

        xb = xb_ref[...]
        a = _dot(xb, w1_ref[...])
        b = _dot(xb, w3_ref[...])
        h = (a * _sigmoid(a) * b).astype(BF16)
        part = _dot(h, w2_ref[...])
        opens_tile = (first_ref[u] == 1) & (f == 0)

        @pl.when(opens_tile)
        def _():
            acc_ref[...] = part

        @pl.when(jnp.logical_not(opens_tile))
        def _():
            acc_ref[...] += part

        @pl.when((last_ref[u] == 1) & (f == nf - 1))
        def _():
            _store_token_tiles(o_ref, acc_ref[...])


def _expert_ffn(xs, units, w1, w3, w2):
    tu, eu, lo, hi, first, last, nu = units
    E, D, FF = w1.shape
    tm, cf = TM_GROUP, FF_CHUNK_MOE
    nf = FF // cf
    n_units = tu.shape[0]

    def chunk(u, f, nu_ref):
        return jnp.where(u < nu_ref[0], f, nf - 1)

    def xmap(u, f, tu, eu, lo, hi, first, last, nu):
        return (tu[u], 0)

    def w13map(u, f, tu, eu, lo, hi, first, last, nu):
        return (eu[u], 0, chunk(u, f, nu))

    def w2map(u, f, tu, eu, lo, hi, first, last, nu):
        return (eu[u], chunk(u, f, nu), 0)

    return pl.pallas_call(
        _expert_ffn_kernel,
        grid_spec=pltpu.PrefetchScalarGridSpec(
            num_scalar_prefetch=7, grid=(n_units, nf),
            in_specs=[pl.BlockSpec((tm * ROW_CHUNKS, LANES), xmap),
                      pl.BlockSpec((None, D, cf), w13map),
                      pl.BlockSpec((None, D, cf), w13map),
                      pl.BlockSpec((None, cf, D), w2map)],
            out_specs=pl.BlockSpec((tm * ROW_CHUNKS, LANES), xmap),
            scratch_shapes=[pltpu.VMEM((tm, D), BF16), pltpu.VMEM((tm, D), F32)]),
        out_shape=jax.ShapeDtypeStruct(xs.shape, F32),
        compiler_params=pltpu.CompilerParams(
            dimension_semantics=("arbitrary", "arbitrary"), vmem_limit_bytes=VMEM_LIMIT_BYTES),
        name="expert_ffn",
    )(tu, eu, lo, hi, first, last, nu, xs, w1, w3, w2)


def _expert_units(cnt, n_rows):
    tm = TM_GROUP
    n_tiles = n_rows // tm
    n_units = n_tiles + N_EXPERTS - 1
    end = jnp.cumsum(cnt)
    start = end - cnt
    tile_lo = jnp.arange(n_tiles, dtype=jnp.int32)[:, None] * tm
    lo = jnp.maximum(start[None, :], tile_lo)
    hi = jnp.minimum(end[None, :], tile_lo + tm)
    valid = (hi > lo).reshape(-1)
    pos = jnp.cumsum(valid.astype(jnp.int32)) - 1
    nu = pos[-1] + 1
    want = jnp.minimum(jnp.arange(n_units, dtype=jnp.int32), nu - 1)
    sel = (pos[None, :] == want[:, None]) & valid[None, :]

    def pick(v):
        return jnp.sum(jnp.where(sel, v.reshape(-1)[None, :], 0), axis=1).astype(jnp.int32)

    shape = (n_tiles, N_EXPERTS)
    tu = pick(jnp.broadcast_to(jnp.arange(n_tiles, dtype=jnp.int32)[:, None], shape))
    eu = pick(jnp.broadcast_to(jnp.arange(N_EXPERTS, dtype=jnp.int32)[None, :], shape))
    lo_u = pick(lo - tile_lo)
    hi_u = pick(hi - tile_lo)
    prev_t = jnp.concatenate([jnp.full((1,), -1, jnp.int32), tu[:-1]])
    next_t = jnp.concatenate([tu[1:], jnp.full((1,), -1, jnp.int32)])
    uid = jnp.arange(n_units, dtype=jnp.int32)
    first = (tu != prev_t).astype(jnp.int32)
    last = ((tu != next_t) | (uid == nu - 1)).astype(jnp.int32)
    return tu, eu, lo_u, hi_u, first, last, nu.reshape(1).astype(jnp.int32), start


def _combine_kernel(alpha, dest_ref, x_ref, p_ref, gate_ref, ys_hbm, ln_g_ref, ln_b_ref, gate_w_ref, ple_w_ref,
                    o_ref, yb_ref, sems):
    i = pl.program_id(0)
    n = pl.num_programs(0)
    tc = x_ref.shape[0]
    n_tok = n * tc
    slot = i % 2

    def issue_step(step, s):
        base = step * tc

        def issue(rb, c):
            for j in range(ISSUE_UNROLL):
                r = rb * ISSUE_UNROLL + j
                for k in range(TOP_K):
                    d = dest_ref[k * n_tok + base + r]
                    pltpu.make_async_copy(ys_hbm.at[_tile_rows(d), :], yb_ref.at[s, k, _tile_rows(r), :],
                                          sems.at[s]).start()
            return c

        lax.fori_loop(0, tc // ISSUE_UNROLL, issue, 0)

    @pl.when(i == 0)
    def _():
        issue_step(0, 0)

    @pl.when(i + 1 < n)
    def _():
        issue_step(i + 1, 1 - slot)

    for k in range(TOP_K):
        pltpu.make_async_copy(ys_hbm.at[pl.ds(0, tc * ROW_CHUNKS), :], yb_ref.at[slot, k], sems.at[slot]).wait()

    g = gate_ref[...]
    f = (g[:, 0:1] * _load_token_tiles(yb_ref.at[slot, 0], tc)
         + g[:, 1:2] * _load_token_tiles(yb_ref.at[slot, 1], tc))
    o_ref[...] = _norm_and_embed(alpha, x_ref[...], f, p_ref[...], ln_g_ref[...], ln_b_ref[...],
                                 gate_w_ref[...], ple_w_ref[...])


def _combine_layer(alpha, x, p, gates, dest, ys, ln_g, ln_b, gate_w, ple_w):
    T, D = x.shape
    tc = TC_COMBINE
    row = lambda v: v.reshape(1, -1)
    consts = [row(ln_g), row(ln_b), gate_w.astype(BF16), ple_w.astype(BF16)]
    return pl.pallas_call(
        functools.partial(_combine_kernel, alpha),
        grid_spec=pltpu.PrefetchScalarGridSpec(
            num_scalar_prefetch=1, grid=(T // tc,),
            in_specs=[pl.BlockSpec((tc, D), lambda i, d: (i, 0)),
                      pl.BlockSpec((tc, D_PLE), lambda i, d: (i, 0)),
                      pl.BlockSpec((tc, TOP_K), lambda i, d: (i, 0)),
                      pl.BlockSpec(memory_space=pl.ANY)]
                     + [pl.BlockSpec(c.shape, lambda i, d: (0, 0), pipeline_mode=pl.Buffered(1)) for c in consts],
            out_specs=pl.BlockSpec((tc, D), lambda i, d: (i, 0)),
            scratch_shapes=[pltpu.VMEM((2, TOP_K, tc * ROW_CHUNKS, LANES), F32),
                            pltpu.SemaphoreType.DMA((2,))]),
        out_shape=jax.ShapeDtypeStruct((T, D), F32),
        compiler_params=pltpu.CompilerParams(
            dimension_semantics=("arbitrary",), vmem_limit_bytes=VMEM_LIMIT_BYTES),
        name="combine",
    )(dest, x, p, gates, ys, *consts)


def _moe_layer(alpha, x, p, router_w, w1, w3, w2, ln_g, ln_b, gate_w, ple_w):
    T, D = x.shape
    route_i, route_g, counts = _router(x, router_w)
    idx = route_i[0:TOP_K]
    rank = route_i[TOP_K:2 * TOP_K]
    gates = route_g[0:TOP_K].T

    *units, start = _expert_units(counts[:, 0], TOP_K * T)
    start_of = jnp.sum(jnp.where(idx[:, :, None] == jnp.arange(N_EXPERTS), start, 0), axis=-1)
    dest = (start_of + rank).astype(jnp.int32).reshape(-1)

    xs = _dispatch(x, dest)
    ys = _expert_ffn(xs, units, w1.astype(BF16), w3.astype(BF16), w2.astype(BF16))
    return _combine_layer(alpha, x, p, gates, dest, ys, ln_g, ln_b, gate_w, ple_w)


def kernel(x, p, w_in, pool_w, pool_scale, conv_w, conv_b, conv_ln_g, conv_ln_b, conv_pw, w_out, ln1_g, ln1_b,
           dense_w1, dense_w3, dense_w2, router_w, exp_w1, exp_w3, exp_w2, ln2_g, ln2_b, ple_gate_w, ple_w):
    depth = w_in.shape[0]
    alpha = (2.0 * depth) ** 0.25
    B, S, D = x.shape
    for i in range(depth):
        x = _mixer(alpha, x, w_in[i], pool_w[i], pool_scale[i], conv_w[i], conv_b[i], conv_ln_g[i],
                   conv_ln_b[i], conv_pw[i], w_out[i], ln1_g[i], ln1_b[i])
        xt = x.reshape(B * S, D)
        pt = p[i].reshape(B * S, D_PLE)
        j = i // 2
        if i % 2 == 0:
            xt = _dense_layer(alpha, xt, pt, dense_w1[j], dense_w3[j], dense_w2[j], ln2_g[i], ln2_b[i],
                              ple_gate_w[i], ple_w[i])
        else:
            xt = _moe_layer(alpha, xt, pt, router_w[j], exp_w1[j], exp_w3[j], exp_w2[j], ln2_g[i], ln2_b[i],
                            ple_gate_w[i], ple_w[i])
        x = xt.reshape(B, S, D)
    return x
```

```python
import functools

import jax
import jax.numpy as jnp
from jax import lax
from jax.experimental import pallas as pl
from jax.experimental.pallas import tpu as pltpu

D_MODEL = 1024
D_PLE = 256
D_POOL = 512
D_CONV = D_MODEL - D_POOL
POOL_WINDOWS = (2, 4, 8, 16)
POOL_GROUP_DIM = D_POOL // len(POOL_WINDOWS)
CONV_WIDTH = 31
D_IN_PROJ = D_POOL + 2 * D_CONV
N_EXPERTS = 8
TOP_K = 2
LN_EPS = 1e-5

F32 = jnp.float32
BF16 = jnp.bfloat16

SUBLANES = 8
LANES = 128
VMEM_LIMIT_BYTES = 56 * 1024 * 1024

HALO = 32
TS_MIX = 512
RB_CONV = 64
TM_DENSE = 512
FF_CHUNK_DENSE = 1024
TR_ROUTE = 512
TM_GROUP = 512
FF_CHUNK_MOE = 1792
TC_DISPATCH = 256
TC_COMBINE = 256
ROW_CHUNKS = D_MODEL // LANES
ISSUE_UNROLL = 8


def _sigmoid(z):
    return 1.0 / (1.0 + jnp.exp(-z))


def _layer_norm(h, g, b):
    mu = jnp.mean(h, axis=-1, keepdims=True)
    c = h - mu
    var = jnp.mean(c * c, axis=-1, keepdims=True)
    return c * lax.rsqrt(var + LN_EPS) * g + b


def _dot(a, b):
    return jnp.dot(a, b, preferred_element_type=F32)


def _mixer_kernel(alpha, x_ref, w_in_ref, pool_w_ref, pool_scale_ref, conv_w_ref, conv_b_ref,
                  cln_g_ref, cln_b_ref, conv_pw_ref, w_out_ref, ln_g_ref, ln_b_ref,
                  o_ref, p_scr, v_scr, y_scr):
    ts = x_ref.shape[0]
    s = pl.program_id(1)

    @pl.when(s == 0)
    def _():
        p_scr[0:HALO, :] = jnp.zeros((HALO, D_POOL), F32)
        v_scr[0:HALO, :] = jnp.zeros((HALO, D_CONV), F32)

    x = x_ref[...]
    u = _dot(x.astype(BF16), w_in_ref[...])
    p_scr[HALO:, :] = u[:, :D_POOL]
    v_scr[HALO:, :] = u[:, D_POOL:D_POOL + D_CONV] * _sigmoid(u[:, D_POOL + D_CONV:])

    pos = s * ts + lax.broadcasted_iota(jnp.int32, (ts, POOL_GROUP_DIM), 0)
    y_parts = []
    for g, w in enumerate(POOL_WINDOWS):
        lanes = slice(g * POOL_GROUP_DIM, (g + 1) * POOL_GROUP_DIM)
        pe = p_scr[:, lanes]
        acc = pe
        step = 1
        while step < w:
            acc = acc + pltpu.roll(acc, step, axis=0)
            step *= 2
        cnt = jnp.minimum(pos + 1, w).astype(F32)
        d = acc[HALO:, :] / cnt - pe[HALO:, :]
        yg = _dot(d.astype(BF16), pool_w_ref[g]) * pool_scale_ref[:, lanes]
        y_parts.append(yg.astype(BF16))

    n_lane_chunks = D_CONV // LANES

    def conv_block(rb, carry):
        r0 = pl.multiple_of(rb * RB_CONV, RB_CONV)
        cols = []
        for lc in range(n_lane_chunks):
            lanes = slice(lc * LANES, (lc + 1) * LANES)
            e = v_scr[pl.ds(r0, RB_CONV + HALO), lanes]
            acc = jnp.zeros((RB_CONV, LANES), F32)
            for b in range(SUBLANES):
                eb = e if b == 0 else pltpu.roll(e, b, axis=0)
                for a in range(HALO // SUBLANES):
                    dist = SUBLANES * a + b
                    if dist > CONV_WIDTH - 1:
                        continue
                    k = CONV_WIDTH - 1 - dist
                    lo = HALO - SUBLANES * a
                    acc = acc + eb[lo:lo + RB_CONV, :] * conv_w_ref[k:k + 1, lanes]
            cols.append(acc)
        y = jnp.concatenate(cols, axis=-1) + conv_b_ref[...]
        z = _layer_norm(y, cln_g_ref[...], cln_b_ref[...])
        y_scr[pl.ds(r0, RB_CONV), :] = (z * _sigmoid(z)).astype(BF16)
        return carry

    lax.fori_loop(0, ts // RB_CONV, conv_block, 0)
    y_conv = _dot(y_scr[...], conv_pw_ref[...])
    y_parts.append(y_conv.astype(BF16))

    mix = _dot(jnp.concatenate(y_parts, axis=-1), w_out_ref[...])
    o_ref[...] = _layer_norm(alpha * x + mix, ln_g_ref[...], ln_b_ref[...])

    p_scr[0:HALO, :] = p_scr[ts:ts + HALO, :]
    v_scr[0:HALO, :] = v_scr[ts:ts + HALO, :]


def _const_spec(shape):
    nd = len(shape)
    return pl.BlockSpec(shape, lambda *_: (0,) * nd, pipeline_mode=pl.Buffered(1))


def _mixer(alpha, x, w_in, pool_w, pool_scale, conv_w, conv_b, cln_g, cln_b, conv_pw, w_out, ln_g, ln_b):
    B, S, D = x.shape
    ts = TS_MIX
    row = lambda v: v.reshape(1, -1)
    consts = [w_in.astype(BF16), pool_w.astype(BF16), row(pool_scale), conv_w, row(conv_b),
              row(cln_g), row(cln_b), conv_pw.astype(BF16), w_out.astype(BF16), row(ln_g), row(ln_b)]
    return pl.pallas_call(
        functools.partial(_mixer_kernel, alpha),
        grid=(B, S // ts),
        in_specs=[pl.BlockSpec((None, ts, D), lambda b, s: (b, s, 0))] + [_const_spec(c.shape) for c in consts],
        out_specs=pl.BlockSpec((None, ts, D), lambda b, s: (b, s, 0)),
        out_shape=jax.ShapeDtypeStruct((B, S, D), F32),
        scratch_shapes=[pltpu.VMEM((HALO + ts, D_POOL), F32),
                        pltpu.VMEM((HALO + ts, D_CONV), F32),
                        pltpu.VMEM((ts, D_CONV), BF16)],
        compiler_params=pltpu.CompilerParams(
            dimension_semantics=("arbitrary", "arbitrary"), vmem_limit_bytes=VMEM_LIMIT_BYTES),
        name="mixer",
    )(x, *consts)


def _norm_and_embed(alpha, x, f, p, ln_g, ln_b, gate_w, ple_w):
    h = _layer_norm(alpha * x + f, ln_g, ln_b)
    gate = _sigmoid(_dot(h.astype(BF16), gate_w))
    return h + gate * _dot(p.astype(BF16), ple_w)


def _dense_ffn_kernel(alpha, x_ref, p_ref, w1_ref, w3_ref, w2_ref, ln_g_ref, ln_b_ref, gate_w_ref, ple_w_ref,
                      o_ref, acc_ref):
    x = x_ref[...]
    xb = x.astype(BF16)
    ff = w1_ref.shape[1]
    for c0 in range(0, ff, FF_CHUNK_DENSE):
        c1 = min(c0 + FF_CHUNK_DENSE, ff)
        a = _dot(xb, w1_ref[:, c0:c1])
        b = _dot(xb, w3_ref[:, c0:c1])
        h = (a * _sigmoid(a) * b).astype(BF16)
        part = _dot(h, w2_ref[c0:c1, :])
        if c0 == 0:
            acc_ref[...] = part
        else:
            acc_ref[...] += part
    o_ref[...] = _norm_and_embed(alpha, x, acc_ref[...], p_ref[...], ln_g_ref[...], ln_b_ref[...],
                                 gate_w_ref[...], ple_w_ref[...])


def _dense_layer(alpha, x, p, w1, w3, w2, ln_g, ln_b, gate_w, ple_w):
    T, D = x.shape
    tm = TM_DENSE
    row = lambda v: v.reshape(1, -1)
    consts = [w1.astype(BF16), w3.astype(BF16), w2.astype(BF16), row(ln_g), row(ln_b),
              gate_w.astype(BF16), ple_w.astype(BF16)]
    return pl.pallas_call(
        functools.partial(_dense_ffn_kernel, alpha),
        grid=(T // tm,),
        in_specs=[pl.BlockSpec((tm, D), lambda i: (i, 0)),
                  pl.BlockSpec((tm, D_PLE), lambda i: (i, 0))] + [_const_spec(c.shape) for c in consts],
        out_specs=pl.BlockSpec((tm, D), lambda i: (i, 0)),
        out_shape=jax.ShapeDtypeStruct((T, D), F32),
        scratch_shapes=[pltpu.VMEM((tm, D), F32)],
        compiler_params=pltpu.CompilerParams(
            dimension_semantics=("arbitrary",), vmem_limit_bytes=VMEM_LIMIT_BYTES),
        name="dense_ffn",
    )(x, p, *consts)


def _router_kernel(x_ref, wt_ref, idx_ref, gate_ref, cnt_ref, carry_ref):
    i = pl.program_id(0)
    tr = x_ref.shape[0]

    @pl.when(i == 0)
    def _():
        carry_ref[...] = jnp.zeros_like(carry_ref)

    x = x_ref[...]
    wt = wt_ref[...]
    xh = x.astype(BF16)
    xl = (x - xh.astype(F32)).astype(BF16)
    wh = wt.astype(BF16)
    wl = (wt - wh.astype(F32)).astype(BF16)
    nt_dims = (((1,), (1,)), ((), ()))
    dg = lambda a, b: lax.dot_general(a, b, nt_dims, preferred_element_type=F32)
    logits = dg(wh, xh) + (dg(wh, xl) + dg(wl, xh))

    eid = lax.broadcasted_iota(jnp.int32, logits.shape, 0)
    m1 = jnp.max(logits, axis=0, keepdims=True)
    i1 = jnp.min(jnp.where(logits == m1, eid, N_EXPERTS), axis=0, keepdims=True)
    rest = jnp.where(eid == i1, -jnp.inf, logits)
    m2 = jnp.max(rest, axis=0, keepdims=True)
    i2 = jnp.min(jnp.where(rest == m2, eid, N_EXPERTS), axis=0, keepdims=True)
    e2 = jnp.exp(m2 - m1)
    g1 = 1.0 / (1.0 + e2)
    g2 = e2 / (1.0 + e2)

    oh1 = (eid == i1).astype(F32)
    oh2 = (eid == i2).astype(F32)
    chosen = oh1 + oh2
    r_i = lax.broadcasted_iota(jnp.int32, (tr, tr), 0)
    c_i = lax.broadcasted_iota(jnp.int32, (tr, tr), 1)
    before = (r_i < c_i).astype(BF16)
    excl = _dot(chosen.astype(BF16), before) + carry_ref[:, 0:1]
    rank1 = jnp.sum(oh1 * excl, axis=0, keepdims=True)
    rank2 = jnp.sum(oh2 * excl, axis=0, keepdims=True)

    total = carry_ref[:, 0:1] + jnp.sum(chosen, axis=1, keepdims=True)
    carry_ref[...] = jnp.broadcast_to(total, carry_ref.shape)
    cnt_ref[...] = jnp.broadcast_to(total, cnt_ref.shape).astype(jnp.int32)

    zi = jnp.zeros((SUBLANES - 4, tr), jnp.int32)
    idx_ref[...] = jnp.concatenate(
        [i1, i2, rank1.astype(jnp.int32), rank2.astype(jnp.int32), zi], axis=0)
    gate_ref[...] = jnp.concatenate([g1, g2, jnp.zeros((SUBLANES - 2, tr), F32)], axis=0)


def _router(x, router_w):
    T, D = x.shape
    tr = TR_ROUTE
    return pl.pallas_call(
        _router_kernel,
        grid=(T // tr,),
        in_specs=[pl.BlockSpec((tr, D), lambda i: (i, 0)), _const_spec((N_EXPERTS, D))],
        out_specs=[pl.BlockSpec((SUBLANES, tr), lambda i: (0, i)),
                   pl.BlockSpec((SUBLANES, tr), lambda i: (0, i)),
                   pl.BlockSpec((N_EXPERTS, LANES), lambda i: (0, 0))],
        out_shape=[jax.ShapeDtypeStruct((SUBLANES, T), jnp.int32),
                   jax.ShapeDtypeStruct((SUBLANES, T), F32),
                   jax.ShapeDtypeStruct((N_EXPERTS, LANES), jnp.int32)],
        scratch_shapes=[pltpu.VMEM((N_EXPERTS, LANES), F32)],
        compiler_params=pltpu.CompilerParams(
            dimension_semantics=("arbitrary",), vmem_limit_bytes=VMEM_LIMIT_BYTES),
        name="router",
    )(x, router_w.T)


def _store_token_tiles(dst_ref, value):
    rows = value.shape[0]
    for c in range(ROW_CHUNKS):
        dst_ref[pl.ds(c, rows, stride=ROW_CHUNKS), :] = value[:, c * LANES:(c + 1) * LANES]


def _load_token_tile_chunk(src_ref, rows, c):
    return src_ref[pl.ds(c, rows, stride=ROW_CHUNKS), :]


def _load_token_tiles(src_ref, rows):
    return jnp.concatenate([_load_token_tile_chunk(src_ref, rows, c) for c in range(ROW_CHUNKS)], axis=-1)


def _tile_rows(row):
    return pl.ds(pl.multiple_of(row * ROW_CHUNKS, ROW_CHUNKS), ROW_CHUNKS)


def _dispatch_kernel(dest_ref, fill_ref, x_ref, xs_hbm, xt_ref, zero_ref, sems, fill_sem):
    i = pl.program_id(0)
    n = pl.num_programs(0)
    tc = x_ref.shape[0]
    n_tok = n * tc
    base = i * tc
    slot = i % 2
    stage = xt_ref.at[slot]

    @pl.when(i == 0)
    def _():
        zero_ref[...] = jnp.zeros_like(zero_ref)
        tile_rows = zero_ref.shape[0]

        def fill(j):
            start = pl.multiple_of(fill_ref[j] * tile_rows, tile_rows)
            return pltpu.make_async_copy(zero_ref, xs_hbm.at[pl.ds(start, tile_rows), :], fill_sem)

        for j in range(fill_ref.shape[0]):
            @pl.when(fill_ref[j] >= 0)
            def _():
                fill(j).start()

        for j in range(fill_ref.shape[0]):
            @pl.when(fill_ref[j] >= 0)
            def _():
                fill(j).wait()

    _store_token_tiles(stage, x_ref[...])

    def copy(k, r):
        d = dest_ref[k * n_tok + base + r]
        return pltpu.make_async_copy(stage.at[_tile_rows(r), :], xs_hbm.at[_tile_rows(d), :], sems.at[slot])

    def issue(rb, c):
        for j in range(ISSUE_UNROLL):
            r = rb * ISSUE_UNROLL + j
            for k in range(TOP_K):
                copy(k, r).start(priority=k)
        return c

    lax.fori_loop(0, tc // ISSUE_UNROLL, issue, 0)

    def wait_step(s):
        for _ in range(TOP_K):
            pltpu.make_async_copy(xt_ref.at[s], xs_hbm.at[pl.ds(0, tc * ROW_CHUNKS), :], sems.at[s]).wait()

    @pl.when(i > 0)
    def _():
        wait_step(1 - slot)

    @pl.when(i == n - 1)
    def _():
        wait_step(slot)


def _dispatch(x, dest, fill_tiles, n_rows):
    T, D = x.shape
    tc = TC_DISPATCH
    return pl.pallas_call(
        _dispatch_kernel,
        grid_spec=pltpu.PrefetchScalarGridSpec(
            num_scalar_prefetch=2, grid=(T // tc,),
            in_specs=[pl.BlockSpec((tc, D), lambda i, d, ft: (i, 0))],
            out_specs=pl.BlockSpec(memory_space=pl.ANY),
            scratch_shapes=[pltpu.VMEM((2, tc * ROW_CHUNKS, LANES), F32),
                            pltpu.VMEM((TM_GROUP * ROW_CHUNKS, LANES), F32),
                            pltpu.SemaphoreType.DMA((2,)), pltpu.SemaphoreType.DMA(())]),
        out_shape=jax.ShapeDtypeStruct((n_rows * ROW_CHUNKS, LANES), F32),
        compiler_params=pltpu.CompilerParams(dimension_semantics=("arbitrary",)),
        name="dispatch",
    )(dest, fill_tiles, x)


def _expert_ffn_kernel(te_ref, nt_ref, xs_ref, w1_ref, w3_ref, w2_ref, o_ref, xb_ref, acc_ref):
    i = pl.program_id(0)
    f = pl.program_id(1)
    nf = pl.num_programs(1)
    tm = acc_ref.shape[0]

    @pl.when(i < nt_ref[0])
    def _():
        @pl.when(f == 0)
        def _():
            for c in range(ROW_CHUNKS):
                xb_ref[:, c * LANES:(c + 1) * LANES] = _load_token_tile_chunk(xs_ref, tm, c).astype(BF16)

        xb = xb_ref[...]
        a = _dot(xb, w1_ref[...])
        b = _dot(xb, w3_ref[...])
        h = (a * _sigmoid(a) * b).astype(BF16)
        part = _dot(h, w2_ref[...])

        @pl.when(f == 0)
        def _():
            acc_ref[...] = part

        @pl.when((f > 0) & (f < nf - 1))
        def _():
            acc_ref[...] += part

        @pl.when(f == nf - 1)
        def _():
            _store_token_tiles(o_ref, acc_ref[...] + part)

    @pl.when((i >= nt_ref[0]) & (f == 0))
    def _():
        o_ref[...] = jnp.zeros_like(o_ref)


def _expert_ffn(xs, te, nt, w1, w3, w2):
    E, D, FF = w1.shape
    tm, cf = TM_GROUP, FF_CHUNK_MOE
    nf = FF // cf
    max_tiles = xs.shape[0] // (tm * ROW_CHUNKS)

    def tile(i, nt_ref):
        return jnp.minimum(i, nt_ref[0] - 1)

    def chunk(i, f, nt_ref):
        return jnp.where(i < nt_ref[0], f, nf - 1)

    return pl.pallas_call(
        _expert_ffn_kernel,
        grid_spec=pltpu.PrefetchScalarGridSpec(
            num_scalar_prefetch=2, grid=(max_tiles, nf),
            in_specs=[pl.BlockSpec((tm * ROW_CHUNKS, LANES), lambda i, f, te, nt: (tile(i, nt), 0)),
                      pl.BlockSpec((None, D, cf), lambda i, f, te, nt: (te[tile(i, nt)], 0, chunk(i, f, nt))),
                      pl.BlockSpec((None, D, cf), lambda i, f, te, nt: (te[tile(i, nt)], 0, chunk(i, f, nt))),
                      pl.BlockSpec((None, cf, D), lambda i, f, te, nt: (te[tile(i, nt)], chunk(i, f, nt), 0))],
            out_specs=pl.BlockSpec((tm * ROW_CHUNKS, LANES), lambda i, f, te, nt: (i, 0)),
            scratch_shapes=[pltpu.VMEM((tm, D), BF16), pltpu.VMEM((tm, D), F32)]),
        out_shape=jax.ShapeDtypeStruct(xs.shape, F32),
        compiler_params=pltpu.CompilerParams(
            dimension_semantics=("arbitrary", "arbitrary"), vmem_limit_bytes=VMEM_LIMIT_BYTES),
        name="expert_ffn",
    )(te, nt, xs, w1, w3, w2)


def _sorted_layout(cnt, n_assign):
    tm = TM_GROUP
    max_tiles = n_assign // tm + N_EXPERTS
    tiles_e = (cnt + tm - 1) // tm
    tile_end = jnp.cumsum(tiles_e)
    row_start = (tile_end - tiles_e) * tm
    nt = tile_end[-1]
    tile_ids = jnp.minimum(jnp.arange(max_tiles, dtype=jnp.int32), nt - 1)
    te = jnp.sum((tile_end[None, :] <= tile_ids[:, None]).astype(jnp.int32), axis=1)
    te = jnp.minimum(te, N_EXPERTS - 1)
    last_tile = jnp.where(tiles_e > 0, tile_end - 1, -1)
    tail = nt + jnp.arange(N_EXPERTS, dtype=jnp.int32)
    tail = jnp.where(tail < max_tiles, tail, -1)
    fill_tiles = jnp.concatenate([last_tile, tail]).astype(jnp.int32)
    return row_start, te, nt.reshape(1).astype(jnp.int32), fill_tiles, max_tiles * tm


def _combine_kernel(alpha, dest_ref, x_ref, p_ref, gate_ref, ys_hbm, ln_g_ref, ln_b_ref, gate_w_ref, ple_w_ref,
                    o_ref, yb_ref, sems):
    i = pl.program_id(0)
    n = pl.num_programs(0)
    tc = x_ref.shape[0]
    n_tok = n * tc
    slot = i % 2

    def issue_step(step, s):
        base = step * tc

        def issue(rb, c):
            for j in range(ISSUE_UNROLL):
                r = rb * ISSUE_UNROLL + j
                for k in range(TOP_K):
                    d = dest_ref[k * n_tok + base + r]
                    pltpu.make_async_copy(ys_hbm.at[_tile_rows(d), :], yb_ref.at[s, k, _tile_rows(r), :],
                                          sems.at[s]).start()
            return c

        lax.fori_loop(0, tc // ISSUE_UNROLL, issue, 0)

    @pl.when(i == 0)
    def _():
        issue_step(0, 0)

    @pl.when(i + 1 < n)
    def _():
        issue_step(i + 1, 1 - slot)

    for k in range(TOP_K):
        pltpu.make_async_copy(ys_hbm.at[pl.ds(0, tc * ROW_CHUNKS), :], yb_ref.at[slot, k], sems.at[slot]).wait()

    g = gate_ref[...]
    f = (g[:, 0:1] * _load_token_tiles(yb_ref.at[slot, 0], tc)
         + g[:, 1:2] * _load_token_tiles(yb_ref.at[slot, 1], tc))
    o_ref[...] = _norm_and_embed(alpha, x_ref[...], f, p_ref[...], ln_g_ref[...], ln_b_ref[...],
                                 gate_w_ref[...], ple_w_ref[...])


def _combine_layer(alpha, x, p, gates, dest, ys, ln_g, ln_b, gate_w, ple_w):
    T, D = x.shape
    tc = TC_COMBINE
    row = lambda v: v.reshape(1, -1)
    consts = [row(ln_g), row(ln_b), gate_w.astype(BF16), ple_w.astype(BF16)]
    return pl.pallas_call(
        functools.partial(_combine_kernel, alpha),
        grid_spec=pltpu.PrefetchScalarGridSpec(
            num_scalar_prefetch=1, grid=(T // tc,),
            in_specs=[pl.BlockSpec((tc, D), lambda i, d: (i, 0)),
                      pl.BlockSpec((tc, D_PLE), lambda i, d: (i, 0)),
                      pl.BlockSpec((tc, TOP_K), lambda i, d: (i, 0)),
                      pl.BlockSpec(memory_space=pl.ANY)]
                     + [pl.BlockSpec(c.shape, lambda i, d: (0, 0), pipeline_mode=pl.Buffered(1)) for c in consts],
            out_specs=pl.BlockSpec((tc, D), lambda i, d: (i, 0)),
            scratch_shapes=[pltpu.VMEM((2, TOP_K, tc * ROW_CHUNKS, LANES), F32),
                            pltpu.SemaphoreType.DMA((2,))]),
        out_shape=jax.ShapeDtypeStruct((T, D), F32),
        compiler_params=pltpu.CompilerParams(
            dimension_semantics=("arbitrary",), vmem_limit_bytes=VMEM_LIMIT_BYTES),
        name="combine",
    )(dest, x, p, gates, ys, *consts)


def _moe_layer(alpha, x, p, router_w, w1, w3, w2, ln_g, ln_b, gate_w, ple_w):
    T, D = x.shape
    route_i, route_g, counts = _router(x, router_w)
    idx = route_i[0:TOP_K]
    rank = route_i[TOP_K:2 * TOP_K]
    gates = route_g[0:TOP_K].T

    start, te, nt, fill_tiles, n_rows = _sorted_layout(counts[:, 0], TOP_K * T)
    start_of = jnp.sum(jnp.where(idx[:, :, None] == jnp.arange(N_EXPERTS), start, 0), axis=-1)
    dest = (start_of + rank).astype(jnp.int32).reshape(-1)

    xs = _dispatch(x, dest, fill_tiles, n_rows)
    ys = _expert_ffn(xs, te, nt, w1.astype(BF16), w3.astype(BF16), w2.astype(BF16))
    return _combine_layer(alpha, x, p, gates, dest, ys, ln_g, ln_b, gate_w, ple_w)


def kernel(x, p, w_in, pool_w, pool_scale, conv_w, conv_b, conv_ln_g, conv_ln_b, conv_pw, w_out, ln1_g, ln1_b,
           dense_w1, dense_w3, dense_w2, router_w, exp_w1, exp_w3, exp_w2, ln2_g, ln2_b, ple_gate_w, ple_w):
    depth = w_in.shape[0]
    alpha = (2.0 * depth) ** 0.25
    B, S, D = x.shape
    for i in range(depth):
        x = _mixer(alpha, x, w_in[i], pool_w[i], pool_scale[i], conv_w[i], conv_b[i], conv_ln_g[i],
                   conv_ln_b[i], conv_pw[i], w_out[i], ln1_g[i], ln1_b[i])
        xt = x.reshape(B * S, D)
        pt = p[i].reshape(B * S, D_PLE)
        j = i // 2
        if i % 2 == 0:
            xt = _dense_layer(alpha, xt, pt, dense_w1[j], dense_w3[j], dense_w2[j], ln2_g[i], ln2_b[i],
                              ple_gate_w[i], ple_w[i])
        else:
            xt = _moe_layer(alpha, xt, pt, router_w[j], exp_w1[j], exp_w3[j], exp_w2[j], ln2_g[i], ln2_b[i],
                            ple_gate_w[i], ple_w[i])
        x = xt.reshape(B, S, D)
    return x
```

```python
import functools

import jax
import jax.numpy as jnp
from jax import lax
from jax.experimental import pallas as pl
from jax.experimental.pallas import tpu as pltpu

D_MODEL = 1024
D_PLE = 256
D_POOL = 512
D_CONV = D_MODEL - D_POOL
POOL_WINDOWS = (2, 4, 8, 16)
POOL_GROUP_DIM = D_POOL // len(POOL_WINDOWS)
CONV_WIDTH = 31
D_IN_PROJ = D_POOL + 2 * D_CONV
N_EXPERTS = 8
TOP_K = 2
LN_EPS = 1e-5

F32 = jnp.float32
BF16 = jnp.bfloat16

SUBLANES = 8
LANES = 128
VMEM_LIMIT_BYTES = 56 * 1024 * 1024

HALO = 32
TS_MIX = 512
RB_CONV = 64
TM_DENSE = 512
FF_CHUNK_DENSE = 1024
TR_ROUTE = 512
TM_GROUP = 512
FF_CHUNK_MOE = 1792
TC_DISPATCH = 256
TC_COMBINE = 256
ROW_CHUNKS = D_MODEL // LANES
ISSUE_UNROLL = 8


def _sigmoid(z):
    return 1.0 / (1.0 + jnp.exp(-z))


def _layer_norm(h, g, b):
    mu = jnp.mean(h, axis=-1, keepdims=True)
    c = h - mu
    var = jnp.mean(c * c, axis=-1, keepdims=True)
    return c * lax.rsqrt(var + LN_EPS) * g + b


def _dot(a, b):
    return jnp.dot(a, b, preferred_element_type=F32)


def _mixer_kernel(alpha, tiles_per_seq, x_ref, w_in_f32, pool_w_f32, pool_scale_ref, conv_w_ref, conv_b_ref,
                  cln_g_ref, cln_b_ref, conv_pw_f32, w_out_f32, ln_g_ref, ln_b_ref,
                  o_ref, p_scr, v_scr, u_scr, y_scr, ypool_scr, xprev_scr, w_in_ref, pool_w_ref, conv_pw_ref,
                  w_out_ref):
    ts = x_ref.shape[0]
    t = pl.program_id(0)

    @pl.when(t == 0)
    def _():
        w_in_ref[...] = w_in_f32[...].astype(BF16)
        pool_w_ref[...] = pool_w_f32[...].astype(BF16)
        conv_pw_ref[...] = conv_pw_f32[...].astype(BF16)
        w_out_ref[...] = w_out_f32[...].astype(BF16)
        p_scr[...] = jnp.zeros_like(p_scr)
        v_scr[...] = jnp.zeros_like(v_scr)
        ypool_scr[...] = jnp.zeros_like(ypool_scr)
        xprev_scr[...] = jnp.zeros_like(xprev_scr)

    for rb in range(ts // RB_CONV):
        r0 = rb * RB_CONV
        cols = []
        for lc in range(D_CONV // LANES):
            lanes = slice(lc * LANES, (lc + 1) * LANES)
            e = v_scr[r0:r0 + RB_CONV + HALO, lanes]
            acc = jnp.zeros((RB_CONV, LANES), F32)
            for b in range(SUBLANES):
                eb = e if b == 0 else pltpu.roll(e, b, axis=0)
                for a in range(HALO // SUBLANES):
                    dist = SUBLANES * a + b
                    if dist > CONV_WIDTH - 1:
                        continue
                    k = CONV_WIDTH - 1 - dist
                    lo = HALO - SUBLANES * a
                    acc = acc + eb[lo:lo + RB_CONV, :] * conv_w_ref[k:k + 1, lanes]
            cols.append(acc)
        y = jnp.concatenate(cols, axis=-1) + conv_b_ref[...]
        z = _layer_norm(y, cln_g_ref[...], cln_b_ref[...])
        y_scr[r0:r0 + RB_CONV, :] = (z * _sigmoid(z)).astype(BF16)

    y_conv = _dot(y_scr[...], conv_pw_ref[...])
    heads = jnp.concatenate([ypool_scr[...], y_conv.astype(BF16)], axis=-1)
    mix = _dot(heads, w_out_ref[...])
    o_ref[...] = _layer_norm(alpha * xprev_scr[...] + mix, ln_g_ref[...], ln_b_ref[...])

    tile_in_seq = t % tiles_per_seq
    starts_seq = tile_in_seq == 0
    p_scr[0:HALO, :] = jnp.where(starts_seq, 0.0, p_scr[ts:ts + HALO, :])
    v_scr[0:HALO, :] = jnp.where(starts_seq, 0.0, v_scr[ts:ts + HALO, :])

    x = x_ref[...]
    xprev_scr[...] = x
    u_scr[...] = _dot(x.astype(BF16), w_in_ref[...])
    p_scr[HALO:, :] = u_scr[:, :D_POOL]
    v_scr[HALO:, :] = u_scr[:, D_POOL:D_POOL + D_CONV] * _sigmoid(u_scr[:, D_POOL + D_CONV:])

    pos = tile_in_seq * ts + lax.broadcasted_iota(jnp.int32, (ts, POOL_GROUP_DIM), 0)
    for g, w in enumerate(POOL_WINDOWS):
        lanes = slice(g * POOL_GROUP_DIM, (g + 1) * POOL_GROUP_DIM)
        pe = p_scr[:, lanes]
        acc = pe
        step = 1
        while step < w:
            acc = acc + pltpu.roll(acc, step, axis=0)
            step *= 2
        cnt = jnp.minimum(pos + 1, w).astype(F32)
        d = acc[HALO:, :] / cnt - pe[HALO:, :]
        yg = _dot(d.astype(BF16), pool_w_ref[g]) * pool_scale_ref[:, lanes]
        ypool_scr[:, lanes] = yg.astype(BF16)


def _const_spec(shape):
    nd = len(shape)
    return pl.BlockSpec(shape, lambda *_: (0,) * nd, pipeline_mode=pl.Buffered(1))


def _mixer(alpha, x, w_in, pool_w, pool_scale, conv_w, conv_b, cln_g, cln_b, conv_pw, w_out, ln_g, ln_b):
    B, S, D = x.shape
    ts = TS_MIX
    n_tiles = B * S // ts
    row = lambda v: v.reshape(1, -1)
    consts = [w_in, pool_w, row(pool_scale), conv_w, row(conv_b), row(cln_g), row(cln_b), conv_pw, w_out,
              row(ln_g), row(ln_b)]
    out = pl.pallas_call(
        functools.partial(_mixer_kernel, alpha, S // ts),
        grid=(n_tiles + 1,),
        in_specs=[pl.BlockSpec((ts, D), lambda t: (jnp.minimum(t, n_tiles - 1), 0))]
                 + [_const_spec(c.shape) for c in consts],
        out_specs=pl.BlockSpec((ts, D), lambda t: (jnp.maximum(t - 1, 0), 0)),
        out_shape=jax.ShapeDtypeStruct((B * S, D), F32),
        scratch_shapes=[pltpu.VMEM((HALO + ts, D_POOL), F32),
                        pltpu.VMEM((HALO + ts, D_CONV), F32),
                        pltpu.VMEM((ts, D_IN_PROJ), F32),
                        pltpu.VMEM((ts, D_CONV), BF16),
                        pltpu.VMEM((ts, D_POOL), BF16),
                        pltpu.VMEM((ts, D), F32),
                        pltpu.VMEM(w_in.shape, BF16), pltpu.VMEM(pool_w.shape, BF16),
                        pltpu.VMEM(conv_pw.shape, BF16), pltpu.VMEM(w_out.shape, BF16)],
        compiler_params=pltpu.CompilerParams(
            dimension_semantics=("arbitrary",), vmem_limit_bytes=VMEM_LIMIT_BYTES),
        name="mixer",
    )(x.reshape(B * S, D), *consts)
    return out.reshape(B, S, D)


def _norm_and_embed(alpha, x, f, p, ln_g, ln_b, gate_w, ple_w):
    h = _layer_norm(alpha * x + f, ln_g, ln_b)
    gate = _sigmoid(_dot(h.astype(BF16), gate_w))
    return h + gate * _dot(p.astype(BF16), ple_w)


def _dense_ffn_kernel(alpha, x_ref, p_ref, w1_ref, w3_ref, w2_ref, ln_g_ref, ln_b_ref, gate_w_ref, ple_w_ref,
                      o_ref, acc_ref):
    x = x_ref[...]
    xb = x.astype(BF16)
    ff = w1_ref.shape[1]
    for c0 in range(0, ff, FF_CHUNK_DENSE):
        c1 = min(c0 + FF_CHUNK_DENSE, ff)
        a = _dot(xb, w1_ref[:, c0:c1])
        b = _dot(xb, w3_ref[:, c0:c1])
        h = (a * _sigmoid(a) * b).astype(BF16)
        part = _dot(h, w2_ref[c0:c1, :])
        if c0 == 0:
            acc_ref[...] = part
        else:
            acc_ref[...] += part
    o_ref[...] = _norm_and_embed(alpha, x, acc_ref[...], p_ref[...], ln_g_ref[...], ln_b_ref[...],
                                 gate_w_ref[...], ple_w_ref[...])


def _dense_layer(alpha, x, p, w1, w3, w2, ln_g, ln_b, gate_w, ple_w):
    T, D = x.shape
    tm = TM_DENSE
    row = lambda v: v.reshape(1, -1)
    consts = [w1.astype(BF16), w3.astype(BF16), w2.astype(BF16), row(ln_g), row(ln_b),
              gate_w.astype(BF16), ple_w.astype(BF16)]
    return pl.pallas_call(
        functools.partial(_dense_ffn_kernel, alpha),
        grid=(T // tm,),
        in_specs=[pl.BlockSpec((tm, D), lambda i: (i, 0)),
                  pl.BlockSpec((tm, D_PLE), lambda i: (i, 0))] + [_const_spec(c.shape) for c in consts],
        out_specs=pl.BlockSpec((tm, D), lambda i: (i, 0)),
        out_shape=jax.ShapeDtypeStruct((T, D), F32),
        scratch_shapes=[pltpu.VMEM((tm, D), F32)],
        compiler_params=pltpu.CompilerParams(
            dimension_semantics=("arbitrary",), vmem_limit_bytes=VMEM_LIMIT_BYTES),
        name="dense_ffn",
    )(x, p, *consts)


def _router_kernel(x_ref, wt_ref, idx_ref, gate_ref, cnt_ref, carry_ref):
    i = pl.program_id(0)
    tr = x_ref.shape[0]

    @pl.when(i == 0)
    def _():
        carry_ref[...] = jnp.zeros_like(carry_ref)

    x = x_ref[...]
    wt = wt_ref[...]
    xh = x.astype(BF16)
    xl = (x - xh.astype(F32)).astype(BF16)
    wh = wt.astype(BF16)
    wl = (wt - wh.astype(F32)).astype(BF16)
    nt_dims = (((1,), (1,)), ((), ()))
    dg = lambda a, b: lax.dot_general(a, b, nt_dims, preferred_element_type=F32)
    logits = dg(wh, xh) + (dg(wh, xl) + dg(wl, xh))

    eid = lax.broadcasted_iota(jnp.int32, logits.shape, 0)
    m1 = jnp.max(logits, axis=0, keepdims=True)
    i1 = jnp.min(jnp.where(logits == m1, eid, N_EXPERTS), axis=0, keepdims=True)
    rest = jnp.where(eid == i1, -jnp.inf, logits)
    m2 = jnp.max(rest, axis=0, keepdims=True)
    i2 = jnp.min(jnp.where(rest == m2, eid, N_EXPERTS), axis=0, keepdims=True)
    e2 = jnp.exp(m2 - m1)
    g1 = 1.0 / (1.0 + e2)
    g2 = e2 / (1.0 + e2)

    oh1 = (eid == i1).astype(F32)
    oh2 = (eid == i2).astype(F32)
    chosen = oh1 + oh2
    r_i = lax.broadcasted_iota(jnp.int32, (tr, tr), 0)
    c_i = lax.broadcasted_iota(jnp.int32, (tr, tr), 1)
    before = (r_i < c_i).astype(BF16)
    excl = _dot(chosen.astype(BF16), before) + carry_ref[:, 0:1]
    rank1 = jnp.sum(oh1 * excl, axis=0, keepdims=True)
    rank2 = jnp.sum(oh2 * excl, axis=0, keepdims=True)

    total = carry_ref[:, 0:1] + jnp.sum(chosen, axis=1, keepdims=True)
    carry_ref[...] = jnp.broadcast_to(total, carry_ref.shape)
    cnt_ref[...] = jnp.broadcast_to(total, cnt_ref.shape).astype(jnp.int32)

    zi = jnp.zeros((SUBLANES - 4, tr), jnp.int32)
    idx_ref[...] = jnp.concatenate(
        [i1, i2, rank1.astype(jnp.int32), rank2.astype(jnp.int32), zi], axis=0)
    gate_ref[...] = jnp.concatenate([g1, g2, jnp.zeros((SUBLANES - 2, tr), F32)], axis=0)


def _router(x, router_w):
    T, D = x.shape
    tr = TR_ROUTE
    return pl.pallas_call(
        _router_kernel,
        grid=(T // tr,),
        in_specs=[pl.BlockSpec((tr, D), lambda i: (i, 0)), _const_spec((N_EXPERTS, D))],
        out_specs=[pl.BlockSpec((SUBLANES, tr), lambda i: (0, i)),
                   pl.BlockSpec((SUBLANES, tr), lambda i: (0, i)),
                   pl.BlockSpec((N_EXPERTS, LANES), lambda i: (0, 0))],
        out_shape=[jax.ShapeDtypeStruct((SUBLANES, T), jnp.int32),
                   jax.ShapeDtypeStruct((SUBLANES, T), F32),
                   jax.ShapeDtypeStruct((N_EXPERTS, LANES), jnp.int32)],
        scratch_shapes=[pltpu.VMEM((N_EXPERTS, LANES), F32)],
        compiler_params=pltpu.CompilerParams(
            dimension_semantics=("arbitrary",), vmem_limit_bytes=VMEM_LIMIT_BYTES),
        name="router",
    )(x, router_w.T)


def _store_token_tiles(dst_ref, value):
    rows = value.shape[0]
    for c in range(ROW_CHUNKS):
        dst_ref[pl.ds(c, rows, stride=ROW_CHUNKS), :] = value[:, c * LANES:(c + 1) * LANES]


def _load_token_tile_chunk(src_ref, rows, c):
    return src_ref[pl.ds(c, rows, stride=ROW_CHUNKS), :]


def _load_token_tiles(src_ref, rows):
    return jnp.concatenate([_load_token_tile_chunk(src_ref, rows, c) for c in range(ROW_CHUNKS)], axis=-1)


def _tile_rows(row):
    return pl.ds(pl.multiple_of(row * ROW_CHUNKS, ROW_CHUNKS), ROW_CHUNKS)


def _dispatch_kernel(dest_ref, fill_ref, x_ref, xs_hbm, xt_ref, zero_ref, sems, fill_sem):
    i = pl.program_id(0)
    n = pl.num_programs(0)
    tc = x_ref.shape[0]
    n_tok = n * tc
    base = i * tc
    slot = i % 2
    stage = xt_ref.at[slot]

    @pl.when(i == 0)
    def _():
        zero_ref[...] = jnp.zeros_like(zero_ref)
        tile_rows = zero_ref.shape[0]

        def fill(j):
            start = pl.multiple_of(fill_ref[j] * tile_rows, tile_rows)
            return pltpu.make_async_copy(zero_ref, xs_hbm.at[pl.ds(start, tile_rows), :], fill_sem)

        for j in range(fill_ref.shape[0]):
            @pl.when(fill_ref[j] >= 0)
            def _():
                fill(j).start()

        for j in range(fill_ref.shape[0]):
            @pl.when(fill_ref[j] >= 0)
            def _():
                fill(j).wait()

    _store_token_tiles(stage, x_ref[...])

    def copy(k, r):
        d = dest_ref[k * n_tok + base + r]
        return pltpu.make_async_copy(stage.at[_tile_rows(r), :], xs_hbm.at[_tile_rows(d), :], sems.at[slot])

    def issue(rb, c):
        for j in range(ISSUE_UNROLL):
            r = rb * ISSUE_UNROLL + j
            for k in range(TOP_K):
                copy(k, r).start(priority=k)
        return c

    lax.fori_loop(0, tc // ISSUE_UNROLL, issue, 0)

    def wait_step(s):
        for _ in range(TOP_K):
            pltpu.make_async_copy(xt_ref.at[s], xs_hbm.at[pl.ds(0, tc * ROW_CHUNKS), :], sems.at[s]).wait()

    @pl.when(i > 0)
    def _():
        wait_step(1 - slot)

    @pl.when(i == n - 1)
    def _():
        wait_step(slot)


def _dispatch(x, dest, fill_tiles, n_rows):
    T, D = x.shape
    tc = TC_DISPATCH
    return pl.pallas_call(
        _dispatch_kernel,
        grid_spec=pltpu.PrefetchScalarGridSpec(
            num_scalar_prefetch=2, grid=(T // tc,),
            in_specs=[pl.BlockSpec((tc, D), lambda i, d, ft: (i, 0))],
            out_specs=pl.BlockSpec(memory_space=pl.ANY),
            scratch_shapes=[pltpu.VMEM((2, tc * ROW_CHUNKS, LANES), F32),
                            pltpu.VMEM((TM_GROUP * ROW_CHUNKS, LANES), F32),
                            pltpu.SemaphoreType.DMA((2,)), pltpu.SemaphoreType.DMA(())]),
        out_shape=jax.ShapeDtypeStruct((n_rows * ROW_CHUNKS, LANES), F32),
        compiler_params=pltpu.CompilerParams(dimension_semantics=("arbitrary",)),
        name="dispatch",
    )(dest, fill_tiles, x)


def _expert_ffn_kernel(te_ref, nt_ref, xs_ref, w1_ref, w3_ref, w2_ref, o_ref, xb_ref, acc_ref):
    i = pl.program_id(0)
    f = pl.program_id(1)
    nf = pl.num_programs(1)
    tm = acc_ref.shape[0]

    @pl.when(i < nt_ref[0])
    def _():
        @pl.when(f == 0)
        def _():
            for c in range(ROW_CHUNKS):
                xb_ref[:, c * LANES:(c + 1) * LANES] = _load_token_tile_chunk(xs_ref, tm, c).astype(BF16)

        xb = xb_ref[...]
        a = _dot(xb, w1_ref[...])
        b = _dot(xb, w3_ref[...])
        h = (a * _sigmoid(a) * b).astype(BF16)
        part = _dot(h, w2_ref[...])

        @pl.when(f == 0)
        def _():
            acc_ref[...] = part

        @pl.when((f > 0) & (f < nf - 1))
        def _():
            acc_ref[...] += part

        @pl.when(f == nf - 1)
        def _():
            _store_token_tiles(o_ref, acc_ref[...] + part)

    @pl.when((i >= nt_ref[0]) & (f == 0))
    def _():
        o_ref[...] = jnp.zeros_like(o_ref)


def _expert_ffn(xs, te, nt, w1, w3, w2):
    E, D, FF = w1.shape
    tm, cf = TM_GROUP, FF_CHUNK_MOE
    nf = FF // cf
    max_tiles = xs.shape[0] // (tm * ROW_CHUNKS)

    def tile(i, nt_ref):
        return jnp.minimum(i, nt_ref[0] - 1)

    def chunk(i, f, nt_ref):
        return jnp.where(i < nt_ref[0], f, nf - 1)

    return pl.pallas_call(
        _expert_ffn_kernel,
        grid_spec=pltpu.PrefetchScalarGridSpec(
            num_scalar_prefetch=2, grid=(max_tiles, nf),
            in_specs=[pl.BlockSpec((tm * ROW_CHUNKS, LANES), lambda i, f, te, nt: (tile(i, nt), 0)),
                      pl.BlockSpec((None, D, cf), lambda i, f, te, nt: (te[tile(i, nt)], 0, chunk(i, f, nt))),
                      pl.BlockSpec((None, D, cf), lambda i, f, te, nt: (te[tile(i, nt)], 0, chunk(i, f, nt))),
                      pl.BlockSpec((None, cf, D), lambda i, f, te, nt: (te[tile(i, nt)], chunk(i, f, nt), 0))],
            out_specs=pl.BlockSpec((tm * ROW_CHUNKS, LANES), lambda i, f, te, nt: (i, 0)),
            scratch_shapes=[pltpu.VMEM((tm, D), BF16), pltpu.VMEM((tm, D), F32)]),
        out_shape=jax.ShapeDtypeStruct(xs.shape, F32),
        compiler_params=pltpu.CompilerParams(
            dimension_semantics=("arbitrary", "arbitrary"), vmem_limit_bytes=VMEM_LIMIT_BYTES),
        name="expert_ffn",
    )(te, nt, xs, w1, w3, w2)


def _sorted_layout(cnt, n_assign):
    tm = TM_GROUP
    max_tiles = n_assign // tm + N_EXPERTS
    tiles_e = (cnt + tm - 1) // tm
    tile_end = jnp.cumsum(tiles_e)
    row_start = (tile_end - tiles_e) * tm
    nt = tile_end[-1]
    tile_ids = jnp.minimum(jnp.arange(max_tiles, dtype=jnp.int32), nt - 1)
    te = jnp.sum((tile_end[None, :] <= tile_ids[:, None]).astype(jnp.int32), axis=1)
    te = jnp.minimum(te, N_EXPERTS - 1)
    last_tile = jnp.where(tiles_e > 0, tile_end - 1, -1)
    tail = nt + jnp.arange(N_EXPERTS, dtype=jnp.int32)
    tail = jnp.where(tail < max_tiles, tail, -1)
    fill_tiles = jnp.concatenate([last_tile, tail]).astype(jnp.int32)
    return row_start, te, nt.reshape(1).astype(jnp.int32), fill_tiles, max_tiles * tm


def _combine_kernel(alpha, dest_ref, x_ref, p_ref, gate_ref, ys_hbm, ln_g_ref, ln_b_ref, gate_w_ref, ple_w_ref,
                    o_ref, yb_ref, sems):
    i = pl.program_id(0)
    n = pl.num_programs(0)
    tc = x_ref.shape[0]
    n_tok = n * tc
    slot = i % 2

    def issue_step(step, s):
        base = step * tc

        def issue(rb, c):
            for j in range(ISSUE_UNROLL):
                r = rb * ISSUE_UNROLL + j
                for k in range(TOP_K):
                    d = dest_ref[k * n_tok + base + r]
                    pltpu.make_async_copy(ys_hbm.at[_tile_rows(d), :], yb_ref.at[s, k, _tile_rows(r), :],
                                          sems.at[s]).start()
            return c

        lax.fori_loop(0, tc // ISSUE_UNROLL, issue, 0)

    @pl.when(i == 0)
    def _():
        issue_step(0, 0)

    @pl.when(i + 1 < n)
    def _():
        issue_step(i + 1, 1 - slot)

    for k in range(TOP_K):
        pltpu.make_async_copy(ys_hbm.at[pl.ds(0, tc * ROW_CHUNKS), :], yb_ref.at[slot, k], sems.at[slot]).wait()

    g = gate_ref[...]
    f = (g[:, 0:1] * _load_token_tiles(yb_ref.at[slot, 0], tc)
         + g[:, 1:2] * _load_token_tiles(yb_ref.at[slot, 1], tc))
    o_ref[...] = _norm_and_embed(alpha, x_ref[...], f, p_ref[...], ln_g_ref[...], ln_b_ref[...],
                                 gate_w_ref[...], ple_w_ref[...])


def _combine_layer(alpha, x, p, gates, dest, ys, ln_g, ln_b, gate_w, ple_w):
    T, D = x.shape
    tc = TC_COMBINE
    row = lambda v: v.reshape(1, -1)
    consts = [row(ln_g), row(ln_b), gate_w.astype(BF16), ple_w.astype(BF16)]
    return pl.pallas_call(
        functools.partial(_combine_kernel, alpha),
        grid_spec=pltpu.PrefetchScalarGridSpec(
            num_scalar_prefetch=1, grid=(T // tc,),
            in_specs=[pl.BlockSpec((tc, D), lambda i, d: (i, 0)),
                      pl.BlockSpec((tc, D_PLE), lambda i, d: (i, 0)),
                      pl.BlockSpec((tc, TOP_K), lambda i, d: (i, 0)),
                      pl.BlockSpec(memory_space=pl.ANY)]
                     + [pl.BlockSpec(c.shape, lambda i, d: (0, 0), pipeline_mode=pl.Buffered(1)) for c in consts],
            out_specs=pl.BlockSpec((tc, D), lambda i, d: (i, 0)),
            scratch_shapes=[pltpu.VMEM((2, TOP_K, tc * ROW_CHUNKS, LANES), F32),
                            pltpu.SemaphoreType.DMA((2,))]),
        out_shape=jax.ShapeDtypeStruct((T, D), F32),
        compiler_params=pltpu.CompilerParams(
            dimension_semantics=("arbitrary",), vmem_limit_bytes=VMEM_LIMIT_BYTES),
        name="combine",
    )(dest, x, p, gates, ys, *consts)


def _moe_layer(alpha, x, p, router_w, w1, w3, w2, ln_g, ln_b, gate_w, ple_w):
    T, D = x.shape
    route_i, route_g, counts = _router(x, router_w)
    idx = route_i[0:TOP_K]
    rank = route_i[TOP_K:2 * TOP_K]
    gates = route_g[0:TOP_K].T

    start, te, nt, fill_tiles, n_rows = _sorted_layout(counts[:, 0], TOP_K * T)
    start_of = jnp.sum(jnp.where(idx[:, :, None] == jnp.arange(N_EXPERTS), start, 0), axis=-1)
    dest = (start_of + rank).astype(jnp.int32).reshape(-1)

    xs = _dispatch(x, dest, fill_tiles, n_rows)
    ys = _expert_ffn(xs, te, nt, w1.astype(BF16), w3.astype(BF16), w2.astype(BF16))
    return _combine_layer(alpha, x, p, gates, dest, ys, ln_g, ln_b, gate_w, ple_w)


def kernel(x, p, w_in, pool_w, pool_scale, conv_w, conv_b, conv_ln_g, conv_ln_b, conv_pw, w_out, ln1_g, ln1_b,
           dense_w1, dense_w3, dense_w2, router_w, exp_w1, exp_w3, exp_w2, ln2_g, ln2_b, ple_gate_w, ple_w):
    depth = w_in.shape[0]
    alpha = (2.0 * depth) ** 0.25
    B, S, D = x.shape
    for i in range(depth):
        x = _mixer(alpha, x, w_in[i], pool_w[i], pool_scale[i], conv_w[i], conv_b[i], conv_ln_g[i],
                   conv_ln_b[i], conv_pw[i], w_out[i], ln1_g[i], ln1_b[i])
        xt = x.reshape(B * S, D)
        pt = p[i].reshape(B * S, D_PLE)
        j = i // 2
        if i % 2 == 0:
            xt = _dense_layer(alpha, xt, pt, dense_w1[j], dense_w3[j], dense_w2[j], ln2_g[i], ln2_b[i],
                              ple_gate_w[i], ple_w[i])
        else:
            xt = _moe_layer(alpha, xt, pt, router_w[j], exp_w1[j], exp_w3[j], exp_w2[j], ln2_g[i], ln2_b[i],
                            ple_gate_w[i], ple_w[i])
        x = xt.reshape(B, S, D)
    return x
```

```python
import functools

import jax
import jax.numpy as jnp
from jax import lax
from jax.experimental import pallas as pl
from jax.experimental.pallas import tpu as pltpu

D_MODEL = 1024
D_PLE = 256
D_POOL = 512
D_CONV = D_MODEL - D_POOL
POOL_WINDOWS = (2, 4, 8, 16)
POOL_GROUP_DIM = D_POOL // len(POOL_WINDOWS)
CONV_WIDTH = 31
D_IN_PROJ = D_POOL + 2 * D_CONV
N_EXPERTS = 8
TOP_K = 2
LN_EPS = 1e-5

F32 = jnp.float32
BF16 = jnp.bfloat16

SUBLANES = 8
LANES = 128
VMEM_LIMIT_BYTES = 56 * 1024 * 1024

HALO = 32
TS_MIX = 512
RB_CONV = 64
TM_DENSE = 512
FF_CHUNK_DENSE = 1024
WIDE_STAGE_ROWS = 128
NARROW_STAGE_ROWS = 256
TR_ROUTE = 512
TM_GROUP = 512
FF_CHUNK_MOE = 1792
TC_DISPATCH = 256
TC_COMBINE = 256
ROW_CHUNKS = D_MODEL // LANES
ISSUE_UNROLL = 8


def _sigmoid(z):
    return 1.0 / (1.0 + jnp.exp(-z))


def _layer_norm(h, g, b):
    mu = jnp.mean(h, axis=-1, keepdims=True)
    c = h - mu
    var = jnp.mean(c * c, axis=-1, keepdims=True)
    return c * lax.rsqrt(var + LN_EPS) * g + b


def _dot(a, b):
    return jnp.dot(a, b, preferred_element_type=F32)


def _mixer_kernel(alpha, tiles_per_seq, x_ref, w_in_f32, pool_w_f32, pool_scale_ref, conv_w_ref, conv_b_ref,
                  cln_g_ref, cln_b_ref, conv_pw_f32, w_out_f32, ln_g_ref, ln_b_ref,
                  o_ref, p_scr, v_scr, u_scr, y_scr, ypool_scr, xprev_scr, w_in_ref, pool_w_ref, conv_pw_ref,
                  w_out_ref):
    ts = x_ref.shape[0]
    t = pl.program_id(0)

    @pl.when(t == 0)
    def _():
        w_in_ref[...] = w_in_f32[...].astype(BF16)
        pool_w_ref[...] = pool_w_f32[...].astype(BF16)
        conv_pw_ref[...] = conv_pw_f32[...].astype(BF16)
        w_out_ref[...] = w_out_f32[...].astype(BF16)
        p_scr[...] = jnp.zeros_like(p_scr)
        v_scr[...] = jnp.zeros_like(v_scr)
        ypool_scr[...] = jnp.zeros_like(ypool_scr)
        xprev_scr[...] = jnp.zeros_like(xprev_scr)

    for rb in range(ts // RB_CONV):
        r0 = rb * RB_CONV
        cols = []
        for lc in range(D_CONV // LANES):
            lanes = slice(lc * LANES, (lc + 1) * LANES)
            e = v_scr[r0:r0 + RB_CONV + HALO, lanes]
            acc = jnp.zeros((RB_CONV, LANES), F32)
            for b in range(SUBLANES):
                eb = e if b == 0 else pltpu.roll(e, b, axis=0)
                for a in range(HALO // SUBLANES):
                    dist = SUBLANES * a + b
                    if dist > CONV_WIDTH - 1:
                        continue
                    k = CONV_WIDTH - 1 - dist
                    lo = HALO - SUBLANES * a
                    acc = acc + eb[lo:lo + RB_CONV, :] * conv_w_ref[k:k + 1, lanes]
            cols.append(acc)
        y = jnp.concatenate(cols, axis=-1) + conv_b_ref[...]
        z = _layer_norm(y, cln_g_ref[...], cln_b_ref[...])
        y_scr[r0:r0 + RB_CONV, :] = (z * _sigmoid(z)).astype(BF16)

    y_conv = _dot(y_scr[...], conv_pw_ref[...])
    heads = jnp.concatenate([ypool_scr[...], y_conv.astype(BF16)], axis=-1)
    mix = _dot(heads, w_out_ref[...])
    o_ref[...] = _layer_norm(alpha * xprev_scr[...] + mix, ln_g_ref[...], ln_b_ref[...])

    tile_in_seq = t % tiles_per_seq
    starts_seq = tile_in_seq == 0
    p_scr[0:HALO, :] = jnp.where(starts_seq, 0.0, p_scr[ts:ts + HALO, :])
    v_scr[0:HALO, :] = jnp.where(starts_seq, 0.0, v_scr[ts:ts + HALO, :])

    x = x_ref[...]
    xprev_scr[...] = x
    u_scr[...] = _dot(x.astype(BF16), w_in_ref[...])
    p_scr[HALO:, :] = u_scr[:, :D_POOL]
    v_scr[HALO:, :] = u_scr[:, D_POOL:D_POOL + D_CONV] * _sigmoid(u_scr[:, D_POOL + D_CONV:])

    pos = tile_in_seq * ts + lax.broadcasted_iota(jnp.int32, (ts, POOL_GROUP_DIM), 0)
    for g, w in enumerate(POOL_WINDOWS):
        lanes = slice(g * POOL_GROUP_DIM, (g + 1) * POOL_GROUP_DIM)
        pe = p_scr[:, lanes]
        acc = pe
        step = 1
        while step < w:
            acc = acc + pltpu.roll(acc, step, axis=0)
            step *= 2
        cnt = jnp.minimum(pos + 1, w).astype(F32)
        d = acc[HALO:, :] / cnt - pe[HALO:, :]
        yg = _dot(d.astype(BF16), pool_w_ref[g]) * pool_scale_ref[:, lanes]
        ypool_scr[:, lanes] = yg.astype(BF16)


def _const_spec(shape):
    nd = len(shape)
    return pl.BlockSpec(shape, lambda *_: (0,) * nd, pipeline_mode=pl.Buffered(1))


def _per_layer(v):
    return v.reshape(v.shape[0], 1, v.shape[1]) if v.ndim == 2 else v


def _layer_spec(v, layer, **kwargs):
    nd = v.ndim
    return pl.BlockSpec((None,) + v.shape[1:], lambda *_: (layer,) + (0,) * (nd - 1),
                        pipeline_mode=pl.Buffered(1), **kwargs)


def _mixer(alpha, layer, x, w_in, pool_w, pool_scale, conv_w, conv_b, cln_g, cln_b, conv_pw, w_out, ln_g, ln_b):
    B, S, D = x.shape
    ts = TS_MIX
    n_tiles = B * S // ts
    consts = [_per_layer(c) for c in (w_in, pool_w, pool_scale, conv_w, conv_b, cln_g, cln_b, conv_pw, w_out,
                                      ln_g, ln_b)]
    out = pl.pallas_call(
        functools.partial(_mixer_kernel, alpha, S // ts),
        grid=(n_tiles + 1,),
        in_specs=[pl.BlockSpec((ts, D), lambda t: (jnp.minimum(t, n_tiles - 1), 0))]
                 + [_layer_spec(c, layer) for c in consts],
        out_specs=pl.BlockSpec((ts, D), lambda t: (jnp.maximum(t - 1, 0), 0)),
        out_shape=jax.ShapeDtypeStruct((B * S, D), F32),
        scratch_shapes=[pltpu.VMEM((HALO + ts, D_POOL), F32),
                        pltpu.VMEM((HALO + ts, D_CONV), F32),
                        pltpu.VMEM((ts, D_IN_PROJ), F32),
                        pltpu.VMEM((ts, D_CONV), BF16),
                        pltpu.VMEM((ts, D_POOL), BF16),
                        pltpu.VMEM((ts, D), F32),
                        pltpu.VMEM(w_in.shape[1:], BF16), pltpu.VMEM(pool_w.shape[1:], BF16),
                        pltpu.VMEM(conv_pw.shape[1:], BF16), pltpu.VMEM(w_out.shape[1:], BF16)],
        compiler_params=pltpu.CompilerParams(
            dimension_semantics=("arbitrary",), vmem_limit_bytes=VMEM_LIMIT_BYTES),
        name="mixer",
    )(x.reshape(B * S, D), *consts)
    return out.reshape(B, S, D)


def _norm_and_embed(alpha, x, f, p, ln_g, ln_b, gate_w, ple_w):
    h = _layer_norm(alpha * x + f, ln_g, ln_b)
    gate = _sigmoid(_dot(h.astype(BF16), gate_w))
    return h + gate * _dot(p.astype(BF16), ple_w)


def _load_bf16_weights(jobs, stage_ref, sems):
    rows = stage_ref.shape[1]
    chunks = [(src, dst, r0) for src, dst in jobs for r0 in range(0, dst.shape[0], rows)]

    def copy(n):
        src, dst, r0 = chunks[n]
        return pltpu.make_async_copy(src.at[pl.ds(r0, rows), :], stage_ref.at[n % 2], sems.at[n % 2])

    copy(0).start()
    for n, (src, dst, r0) in enumerate(chunks):
        if n + 1 < len(chunks):
            copy(n + 1).start()
        copy(n).wait()
        dst[r0:r0 + rows, :] = stage_ref[n % 2].astype(BF16)


def _dense_ffn_kernel(alpha, layer, x_ref, p_ref, w1_hbm, w3_hbm, w2_hbm, ln_g_ref, ln_b_ref, gate_w_hbm,
                      ple_w_hbm, o_ref, acc_ref, w1_ref, w3_ref, w2_ref, gate_w_ref, ple_w_ref,
                      wide_stage, narrow_stage, sems):
    @pl.when(pl.program_id(0) == 0)
    def _():
        j = layer // 2
        _load_bf16_weights([(w1_hbm.at[j], w1_ref), (w3_hbm.at[j], w3_ref)], wide_stage, sems)
        _load_bf16_weights([(w2_hbm.at[j], w2_ref), (gate_w_hbm.at[layer], gate_w_ref),
                            (ple_w_hbm.at[layer], ple_w_ref)], narrow_stage, sems)

    x = x_ref[...]
    xb = x.astype(BF16)
    ff = w1_ref.shape[1]
    for c0 in range(0, ff, FF_CHUNK_DENSE):
        c1 = min(c0 + FF_CHUNK_DENSE, ff)
        a = _dot(xb, w1_ref[:, c0:c1])
        b = _dot(xb, w3_ref[:, c0:c1])
        h = (a * _sigmoid(a) * b).astype(BF16)
        part = _dot(h, w2_ref[c0:c1, :])
        if c0 == 0:
            acc_ref[...] = part
        else:
            acc_ref[...] += part
    o_ref[...] = _norm_and_embed(alpha, x, acc_ref[...], p_ref[...], ln_g_ref[...], ln_b_ref[...],
                                 gate_w_ref[...], ple_w_ref[...])


def _dense_layer(alpha, layer, x, p, w1, w3, w2, ln_g, ln_b, gate_w, ple_w):
    T, D = x.shape
    tm = TM_DENSE
    FF = w1.shape[2]
    ln_g, ln_b = _per_layer(ln_g), _per_layer(ln_b)
    hbm = pl.BlockSpec(memory_space=pl.ANY)
    return pl.pallas_call(
        functools.partial(_dense_ffn_kernel, alpha, layer),
        grid=(T // tm,),
        in_specs=[pl.BlockSpec((tm, D), lambda i: (i, 0)),
                  pl.BlockSpec((None, tm, D_PLE), lambda i: (layer, i, 0)),
                  hbm, hbm, hbm, _layer_spec(ln_g, layer), _layer_spec(ln_b, layer), hbm, hbm],
        out_specs=pl.BlockSpec((tm, D), lambda i: (i, 0)),
        out_shape=jax.ShapeDtypeStruct((T, D), F32),
        scratch_shapes=[pltpu.VMEM((tm, D), F32),
                        pltpu.VMEM((D, FF), BF16), pltpu.VMEM((D, FF), BF16), pltpu.VMEM((FF, D), BF16),
                        pltpu.VMEM((D, D), BF16), pltpu.VMEM((D_PLE, D), BF16),
                        pltpu.VMEM((2, WIDE_STAGE_ROWS, FF), F32), pltpu.VMEM((2, NARROW_STAGE_ROWS, D), F32),
                        pltpu.SemaphoreType.DMA((2,))],
        compiler_params=pltpu.CompilerParams(
            dimension_semantics=("arbitrary",), vmem_limit_bytes=VMEM_LIMIT_BYTES),
        name="dense_ffn",
    )(x, p, w1, w3, w2, ln_g, ln_b, gate_w, ple_w)


def _router_kernel(x_ref, wt_ref, idx_ref, gate_ref, cnt_ref, carry_ref):
    i = pl.program_id(0)
    tr = x_ref.shape[0]

    @pl.when(i == 0)
    def _():
        carry_ref[...] = jnp.zeros_like(carry_ref)

    x = x_ref[...]
    wt = wt_ref[...]
    xh = x.astype(BF16)
    xl = (x - xh.astype(F32)).astype(BF16)
    wh = wt.astype(BF16)
    wl = (wt - wh.astype(F32)).astype(BF16)
    nt_dims = (((1,), (1,)), ((), ()))
    dg = lambda a, b: lax.dot_general(a, b, nt_dims, preferred_element_type=F32)
    logits = dg(wh, xh) + (dg(wh, xl) + dg(wl, xh))

    eid = lax.broadcasted_iota(jnp.int32, logits.shape, 0)
    m1 = jnp.max(logits, axis=0, keepdims=True)
    i1 = jnp.min(jnp.where(logits == m1, eid, N_EXPERTS), axis=0, keepdims=True)
    rest = jnp.where(eid == i1, -jnp.inf, logits)
    m2 = jnp.max(rest, axis=0, keepdims=True)
    i2 = jnp.min(jnp.where(rest == m2, eid, N_EXPERTS), axis=0, keepdims=True)
    e2 = jnp.exp(m2 - m1)
    g1 = 1.0 / (1.0 + e2)
    g2 = e2 / (1.0 + e2)

    oh1 = (eid == i1).astype(F32)
    oh2 = (eid == i2).astype(F32)
    chosen = oh1 + oh2
    r_i = lax.broadcasted_iota(jnp.int32, (tr, tr), 0)
    c_i = lax.broadcasted_iota(jnp.int32, (tr, tr), 1)
    before = (r_i < c_i).astype(BF16)
    excl = _dot(chosen.astype(BF16), before) + carry_ref[:, 0:1]
    rank1 = jnp.sum(oh1 * excl, axis=0, keepdims=True)
    rank2 = jnp.sum(oh2 * excl, axis=0, keepdims=True)

    total = carry_ref[:, 0:1] + jnp.sum(chosen, axis=1, keepdims=True)
    carry_ref[...] = jnp.broadcast_to(total, carry_ref.shape)
    cnt_ref[...] = jnp.broadcast_to(total, cnt_ref.shape).astype(jnp.int32)

    zi = jnp.zeros((SUBLANES - 4, tr), jnp.int32)
    idx_ref[...] = jnp.concatenate(
        [i1, i2, rank1.astype(jnp.int32), rank2.astype(jnp.int32), zi], axis=0)
    gate_ref[...] = jnp.concatenate([g1, g2, jnp.zeros((SUBLANES - 2, tr), F32)], axis=0)


def _router(x, router_w):
    T, D = x.shape
    tr = TR_ROUTE
    return pl.pallas_call(
        _router_kernel,
        grid=(T // tr,),
        in_specs=[pl.BlockSpec((tr, D), lambda i: (i, 0)), _const_spec((N_EXPERTS, D))],
        out_specs=[pl.BlockSpec((SUBLANES, tr), lambda i: (0, i)),
                   pl.BlockSpec((SUBLANES, tr), lambda i: (0, i)),
                   pl.BlockSpec((N_EXPERTS, LANES), lambda i: (0, 0))],
        out_shape=[jax.ShapeDtypeStruct((SUBLANES, T), jnp.int32),
                   jax.ShapeDtypeStruct((SUBLANES, T), F32),
                   jax.ShapeDtypeStruct((N_EXPERTS, LANES), jnp.int32)],
        scratch_shapes=[pltpu.VMEM((N_EXPERTS, LANES), F32)],
        compiler_params=pltpu.CompilerParams(
            dimension_semantics=("arbitrary",), vmem_limit_bytes=VMEM_LIMIT_BYTES),
        name="router",
    )(x, router_w.T)


def _store_token_tiles(dst_ref, value):
    rows = value.shape[0]
    for c in range(ROW_CHUNKS):
        dst_ref[pl.ds(c, rows, stride=ROW_CHUNKS), :] = value[:, c * LANES:(c + 1) * LANES]


def _load_token_tile_chunk(src_ref, rows, c):
    return src_ref[pl.ds(c, rows, stride=ROW_CHUNKS), :]


def _load_token_tiles(src_ref, rows):
    return jnp.concatenate([_load_token_tile_chunk(src_ref, rows, c) for c in range(ROW_CHUNKS)], axis=-1)


def _tile_rows(row):
    return pl.ds(pl.multiple_of(row * ROW_CHUNKS, ROW_CHUNKS), ROW_CHUNKS)


def _dispatch_kernel(dest_ref, fill_ref, x_ref, xs_hbm, xt_ref, zero_ref, sems, fill_sem):
    i = pl.program_id(0)
    n = pl.num_programs(0)
    tc = x_ref.shape[0]
    n_tok = n * tc
    base = i * tc
    slot = i % 2
    stage = xt_ref.at[slot]

    @pl.when(i == 0)
    def _():
        zero_ref[...] = jnp.zeros_like(zero_ref)
        tile_rows = zero_ref.shape[0]

        def fill(j):
            start = pl.multiple_of(fill_ref[j] * tile_rows, tile_rows)
            return pltpu.make_async_copy(zero_ref, xs_hbm.at[pl.ds(start, tile_rows), :], fill_sem)

        for j in range(fill_ref.shape[0]):
            @pl.when(fill_ref[j] >= 0)
            def _():
                fill(j).start()

        for j in range(fill_ref.shape[0]):
            @pl.when(fill_ref[j] >= 0)
            def _():
                fill(j).wait()

    _store_token_tiles(stage, x_ref[...])

    def copy(k, r):
        d = dest_ref[k * n_tok + base + r]
        return pltpu.make_async_copy(stage.at[_tile_rows(r), :], xs_hbm.at[_tile_rows(d), :], sems.at[slot])

    def issue(rb, c):
        for j in range(ISSUE_UNROLL):
            r = rb * ISSUE_UNROLL + j
            for k in range(TOP_K):
                copy(k, r).start(priority=k)
        return c

    lax.fori_loop(0, tc // ISSUE_UNROLL, issue, 0)

    def wait_step(s):
        for _ in range(TOP_K):
            pltpu.make_async_copy(xt_ref.at[s], xs_hbm.at[pl.ds(0, tc * ROW_CHUNKS), :], sems.at[s]).wait()

    @pl.when(i > 0)
    def _():
        wait_step(1 - slot)

    @pl.when(i == n - 1)
    def _():
        wait_step(slot)


def _dispatch(x, dest, fill_tiles, n_rows):
    T, D = x.shape
    tc = TC_DISPATCH
    return pl.pallas_call(
        _dispatch_kernel,
        grid_spec=pltpu.PrefetchScalarGridSpec(
            num_scalar_prefetch=2, grid=(T // tc,),
            in_specs=[pl.BlockSpec((tc, D), lambda i, d, ft: (i, 0))],
            out_specs=pl.BlockSpec(memory_space=pl.ANY),
            scratch_shapes=[pltpu.VMEM((2, tc * ROW_CHUNKS, LANES), F32),
                            pltpu.VMEM((TM_GROUP * ROW_CHUNKS, LANES), F32),
                            pltpu.SemaphoreType.DMA((2,)), pltpu.SemaphoreType.DMA(())]),
        out_shape=jax.ShapeDtypeStruct((n_rows * ROW_CHUNKS, LANES), F32),
        compiler_params=pltpu.CompilerParams(dimension_semantics=("arbitrary",)),
        name="dispatch",
    )(dest, fill_tiles, x)


def _expert_ffn_kernel(te_ref, nt_ref, xs_ref, w1_ref, w3_ref, w2_ref, o_ref, xb_ref, acc_ref):
    i = pl.program_id(0)
    f = pl.program_id(1)
    nf = pl.num_programs(1)
    tm = acc_ref.shape[0]

    @pl.when(i < nt_ref[0])
    def _():
        @pl.when(f == 0)
        def _():
            for c in range(ROW_CHUNKS):
                xb_ref[:, c * LANES:(c + 1) * LANES] = _load_token_tile_chunk(xs_ref, tm, c).astype(BF16)

        xb = xb_ref[...]
        a = _dot(xb, w1_ref[...])
        b = _dot(xb, w3_ref[...])
        h = (a * _sigmoid(a) * b).astype(BF16)
        part = _dot(h, w2_ref[...])

        @pl.when(f == 0)
        def _():
            acc_ref[...] = part

        @pl.when((f > 0) & (f < nf - 1))
        def _():
            acc_ref[...] += part

        @pl.when(f == nf - 1)
        def _():
            _store_token_tiles(o_ref, acc_ref[...] + part)

    @pl.when((i >= nt_ref[0]) & (f == 0))
    def _():
        o_ref[...] = jnp.zeros_like(o_ref)


def _expert_ffn(xs, te, nt, w1, w3, w2):
    E, D, FF = w1.shape
    tm, cf = TM_GROUP, FF_CHUNK_MOE
    nf = FF // cf
    max_tiles = xs.shape[0] // (tm * ROW_CHUNKS)

    def tile(i, nt_ref):
        return jnp.minimum(i, nt_ref[0] - 1)

    def chunk(i, f, nt_ref):
        return jnp.where(i < nt_ref[0], f, nf - 1)

    return pl.pallas_call(
        _expert_ffn_kernel,
        grid_spec=pltpu.PrefetchScalarGridSpec(
            num_scalar_prefetch=2, grid=(max_tiles, nf),
            in_specs=[pl.BlockSpec((tm * ROW_CHUNKS, LANES), lambda i, f, te, nt: (tile(i, nt), 0)),
                      pl.BlockSpec((None, D, cf), lambda i, f, te, nt: (te[tile(i, nt)], 0, chunk(i, f, nt))),
                      pl.BlockSpec((None, D, cf), lambda i, f, te, nt: (te[tile(i, nt)], 0, chunk(i, f, nt))),
                      pl.BlockSpec((None, cf, D), lambda i, f, te, nt: (te[tile(i, nt)], chunk(i, f, nt), 0))],
            out_specs=pl.BlockSpec((tm * ROW_CHUNKS, LANES), lambda i, f, te, nt: (i, 0)),
            scratch_shapes=[pltpu.VMEM((tm, D), BF16), pltpu.VMEM((tm, D), F32)]),
        out_shape=jax.ShapeDtypeStruct(xs.shape, F32),
        compiler_params=pltpu.CompilerParams(
            dimension_semantics=("arbitrary", "arbitrary"), vmem_limit_bytes=VMEM_LIMIT_BYTES),
        name="expert_ffn",
    )(te, nt, xs, w1, w3, w2)


def _sorted_layout(cnt, n_assign):
    tm = TM_GROUP
    max_tiles = n_assign // tm + N_EXPERTS
    tiles_e = (cnt + tm - 1) // tm
    tile_end = jnp.cumsum(tiles_e)
    row_start = (tile_end - tiles_e) * tm
    nt = tile_end[-1]
    tile_ids = jnp.minimum(jnp.arange(max_tiles, dtype=jnp.int32), nt - 1)
    te = jnp.sum((tile_end[None, :] <= tile_ids[:, None]).astype(jnp.int32), axis=1)
    te = jnp.minimum(te, N_EXPERTS - 1)
    last_tile = jnp.where(tiles_e > 0, tile_end - 1, -1)
    tail = nt + jnp.arange(N_EXPERTS, dtype=jnp.int32)
    tail = jnp.where(tail < max_tiles, tail, -1)
    fill_tiles = jnp.concatenate([last_tile, tail]).astype(jnp.int32)
    return row_start, te, nt.reshape(1).astype(jnp.int32), fill_tiles, max_tiles * tm


def _combine_kernel(alpha, dest_ref, x_ref, p_ref, gate_ref, ys_hbm, ln_g_ref, ln_b_ref, gate_w_f32, ple_w_f32,
                    o_ref, yb_ref, gate_w_ref, ple_w_ref, sems):
    i = pl.program_id(0)
    n = pl.num_programs(0)
    tc = x_ref.shape[0]
    n_tok = n * tc
    slot = i % 2

    @pl.when(i == 0)
    def _():
        gate_w_ref[...] = gate_w_f32[...].astype(BF16)
        ple_w_ref[...] = ple_w_f32[...].astype(BF16)

    def issue_step(step, s):
        base = step * tc

        def issue(rb, c):
            for j in range(ISSUE_UNROLL):
                r = rb * ISSUE_UNROLL + j
                for k in range(TOP_K):
                    d = dest_ref[k * n_tok + base + r]
                    pltpu.make_async_copy(ys_hbm.at[_tile_rows(d), :], yb_ref.at[s, k, _tile_rows(r), :],
                                          sems.at[s]).start()
            return c

        lax.fori_loop(0, tc // ISSUE_UNROLL, issue, 0)

    @pl.when(i == 0)
    def _():
        issue_step(0, 0)

    @pl.when(i + 1 < n)
    def _():
        issue_step(i + 1, 1 - slot)

    for k in range(TOP_K):
        pltpu.make_async_copy(ys_hbm.at[pl.ds(0, tc * ROW_CHUNKS), :], yb_ref.at[slot, k], sems.at[slot]).wait()

    g = gate_ref[...]
    f = (g[:, 0:1] * _load_token_tiles(yb_ref.at[slot, 0], tc)
         + g[:, 1:2] * _load_token_tiles(yb_ref.at[slot, 1], tc))
    o_ref[...] = _norm_and_embed(alpha, x_ref[...], f, p_ref[...], ln_g_ref[...], ln_b_ref[...],
                                 gate_w_ref[...], ple_w_ref[...])


def _combine_layer(alpha, layer, x, p, gates, dest, ys, ln_g, ln_b, gate_w, ple_w):
    T, D = x.shape
    tc = TC_COMBINE
    consts = [_per_layer(c) for c in (ln_g, ln_b, gate_w, ple_w)]
    return pl.pallas_call(
        functools.partial(_combine_kernel, alpha),
        grid_spec=pltpu.PrefetchScalarGridSpec(
            num_scalar_prefetch=1, grid=(T // tc,),
            in_specs=[pl.BlockSpec((tc, D), lambda i, d: (i, 0)),
                      pl.BlockSpec((None, tc, D_PLE), lambda i, d: (layer, i, 0)),
                      pl.BlockSpec((tc, TOP_K), lambda i, d: (i, 0)),
                      pl.BlockSpec(memory_space=pl.ANY)]
                     + [_layer_spec(c, layer) for c in consts],
            out_specs=pl.BlockSpec((tc, D), lambda i, d: (i, 0)),
            scratch_shapes=[pltpu.VMEM((2, TOP_K, tc * ROW_CHUNKS, LANES), F32),
                            pltpu.VMEM((D, D), BF16), pltpu.VMEM((D_PLE, D), BF16),
                            pltpu.SemaphoreType.DMA((2,))]),
        out_shape=jax.ShapeDtypeStruct((T, D), F32),
        compiler_params=pltpu.CompilerParams(
            dimension_semantics=("arbitrary",), vmem_limit_bytes=VMEM_LIMIT_BYTES),
        name="combine",
    )(dest, x, p, gates, ys, *consts)


def _moe_layer(alpha, layer, x, p, router_w, w1, w3, w2, ln_g, ln_b, gate_w, ple_w):
    T, D = x.shape
    route_i, route_g, counts = _router(x, router_w)
    idx = route_i[0:TOP_K]
    rank = route_i[TOP_K:2 * TOP_K]
    gates = route_g[0:TOP_K].T

    start, te, nt, fill_tiles, n_rows = _sorted_layout(counts[:, 0], TOP_K * T)
    start_of = sum(jnp.where(idx == e, start[e], 0) for e in range(N_EXPERTS))
    dest = (start_of + rank).astype(jnp.int32).reshape(-1)

    xs = _dispatch(x, dest, fill_tiles, n_rows)
    ys = _expert_ffn(xs, te, nt, w1.astype(BF16), w3.astype(BF16), w2.astype(BF16))
    return _combine_layer(alpha, layer, x, p, gates, dest, ys, ln_g, ln_b, gate_w, ple_w)


def kernel(x, p, w_in, pool_w, pool_scale, conv_w, conv_b, conv_ln_g, conv_ln_b, conv_pw, w_out, ln1_g, ln1_b,
           dense_w1, dense_w3, dense_w2, router_w, exp_w1, exp_w3, exp_w2, ln2_g, ln2_b, ple_gate_w, ple_w):
    depth = w_in.shape[0]
    alpha = (2.0 * depth) ** 0.25
    B, S, D = x.shape
    pt = p.reshape(depth, B * S, D_PLE)
    for i in range(depth):
        x = _mixer(alpha, i, x, w_in, pool_w, pool_scale, conv_w, conv_b, conv_ln_g, conv_ln_b, conv_pw, w_out,
                   ln1_g, ln1_b)
        xt = x.reshape(B * S, D)
        j = i // 2
        if i % 2 == 0:
            xt = _dense_layer(alpha, i, xt, pt, dense_w1, dense_w3, dense_w2, ln2_g, ln2_b, ple_gate_w, ple_w)
        else:
            xt = _moe_layer(alpha, i, xt, pt, router_w[j], exp_w1[j], exp_w3[j], exp_w2[j], ln2_g, ln2_b,
                            ple_gate_w, ple_w)
        x = xt.reshape(B, S, D)
    return x
```

```python
import functools

import jax
import jax.numpy as jnp
from jax import lax
from jax.experimental import pallas as pl
from jax.experimental.pallas import tpu as pltpu

D_MODEL = 1024
D_PLE = 256
D_POOL = 512
D_CONV = D_MODEL - D_POOL
POOL_WINDOWS = (2, 4, 8, 16)
POOL_GROUP_DIM = D_POOL // len(POOL_WINDOWS)
CONV_WIDTH = 31
D_IN_PROJ = D_POOL + 2 * D_CONV
N_EXPERTS = 8
TOP_K = 2
LN_EPS = 1e-5

F32 = jnp.float32
BF16 = jnp.bfloat16

SUBLANES = 8
LANES = 128
VMEM_LIMIT_BYTES = 56 * 1024 * 1024

HALO = 32
TS_MIX = 512
RB_CONV = 128
IN_PROJ_CHUNK = 256
MIX_ROW_GROUPS = 2
TM_DENSE = 512
FF_CHUNK_DENSE = 1024
WIDE_STAGE_ROWS = 128
NARROW_STAGE_ROWS = 256
TR_ROUTE = 512
TM_GROUP = 512
FF_CHUNK_MOE = 1792
TC_DISPATCH = 256
TC_COMBINE = 256
ROW_CHUNKS = D_MODEL // LANES
ISSUE_UNROLL = 8


def _sigmoid(z):
    return 1.0 / (1.0 + jnp.exp(-z))


def _layer_norm(h, g, b):
    mu = jnp.mean(h, axis=-1, keepdims=True)
    c = h - mu
    var = jnp.mean(c * c, axis=-1, keepdims=True)
    return c * lax.rsqrt(var + LN_EPS) * g + b


def _dot(a, b):
    return jnp.dot(a, b, preferred_element_type=F32)


def _mixer_kernel(alpha, tiles_per_seq, x_ref, w_in_f32, pool_w_f32, pool_scale_ref, conv_w_ref, conv_b_ref,
                  cln_g_ref, cln_b_ref, conv_pw_f32, w_out_f32, ln_g_ref, ln_b_ref,
                  o_ref, p_scr, v_scr, u_scr, y_scr, ypool_scr, xprev_scr, w_in_ref, pool_w_ref, conv_pw_ref,
                  w_out_ref):
    ts = x_ref.shape[0]
    t = pl.program_id(0)

    @pl.when(t == 0)
    def _():
        w_in_ref[...] = w_in_f32[...].astype(BF16)
        pool_w_ref[...] = pool_w_f32[...].astype(BF16)
        conv_pw_ref[...] = conv_pw_f32[...].astype(BF16)
        w_out_ref[...] = w_out_f32[...].astype(BF16)
        p_scr[...] = jnp.zeros_like(p_scr)
        v_scr[...] = jnp.zeros_like(v_scr)
        ypool_scr[...] = jnp.zeros_like(ypool_scr)
        xprev_scr[...] = jnp.zeros_like(xprev_scr)

    x = x_ref[...]
    xb = x.astype(BF16)
    n_conv_blocks = ts // RB_CONV
    in_proj_chunks = [(c0, min(c0 + IN_PROJ_CHUNK, D_IN_PROJ)) for c0 in range(0, D_IN_PROJ, IN_PROJ_CHUNK)]

    blocks_per_group = n_conv_blocks // MIX_ROW_GROUPS
    for rb in range(n_conv_blocks):
        r0 = rb * RB_CONV
        cols = []
        for lc in range(D_CONV // LANES):
            lanes = slice(lc * LANES, (lc + 1) * LANES)
            acc = jnp.zeros((RB_CONV, LANES), F32)
            for k in range(CONV_WIDTH):
                first = r0 + HALO - (CONV_WIDTH - 1 - k)
                acc = acc + v_scr[lc, first:first + RB_CONV, :] * conv_w_ref[k:k + 1, lanes]
            cols.append(acc)
        y = jnp.concatenate(cols, axis=-1) + conv_b_ref[...]
        z = _layer_norm(y, cln_g_ref[...], cln_b_ref[...])
        y_scr[r0:r0 + RB_CONV, :] = (z * _sigmoid(z)).astype(BF16)
        if rb < len(in_proj_chunks):
            c0, c1 = in_proj_chunks[rb]
            u_scr[:, c0:c1] = _dot(xb, w_in_ref[:, c0:c1])
        if (rb + 1) % blocks_per_group == 0:
            rows = slice((rb + 1 - blocks_per_group) * RB_CONV, (rb + 1) * RB_CONV)
            y_conv = _dot(y_scr[rows, :], conv_pw_ref[...])
            heads = jnp.concatenate([ypool_scr[rows, :], y_conv.astype(BF16)], axis=-1)
            mix = _dot(heads, w_out_ref[...])
            o_ref[rows, :] = _layer_norm(alpha * xprev_scr[rows, :] + mix, ln_g_ref[...], ln_b_ref[...])
    for c0, c1 in in_proj_chunks[n_conv_blocks:]:
        u_scr[:, c0:c1] = _dot(xb, w_in_ref[:, c0:c1])

    tile_in_seq = t % tiles_per_seq
    starts_seq = tile_in_seq == 0
    p_scr[:, 0:HALO, :] = jnp.where(starts_seq, 0.0, p_scr[:, ts:ts + HALO, :])
    v_scr[:, 0:HALO, :] = jnp.where(starts_seq, 0.0, v_scr[:, ts:ts + HALO, :])
    for c in range(D_CONV // LANES):
        val = u_scr[:, D_POOL + c * LANES:D_POOL + (c + 1) * LANES]
        gate = u_scr[:, D_POOL + D_CONV + c * LANES:D_POOL + D_CONV + (c + 1) * LANES]
        v_scr[c, HALO:, :] = val * _sigmoid(gate)

    pos = tile_in_seq * ts + lax.broadcasted_iota(jnp.int32, (ts, POOL_GROUP_DIM), 0)
    for g, w in enumerate(POOL_WINDOWS):
        lanes = slice(g * POOL_GROUP_DIM, (g + 1) * POOL_GROUP_DIM)
        cur = u_scr[:, lanes]
        p_scr[g, HALO:, :] = cur
        win = cur
        for back in range(1, w):
            win = win + p_scr[g, HALO - back:HALO - back + ts, :]
        cnt = jnp.minimum(pos + 1, w).astype(F32)
        d = win / cnt - cur
        yg = _dot(d.astype(BF16), pool_w_ref[g]) * pool_scale_ref[:, lanes]
        ypool_scr[:, lanes] = yg.astype(BF16)

    xprev_scr[...] = x


def _const_spec(shape):
    nd = len(shape)
    return pl.BlockSpec(shape, lambda *_: (0,) * nd, pipeline_mode=pl.Buffered(1))


def _per_layer(v):
    return v.reshape(v.shape[0], 1, v.shape[1]) if v.ndim == 2 else v


def _layer_spec(v, layer, **kwargs):
    nd = v.ndim
    return pl.BlockSpec((None,) + v.shape[1:], lambda *_: (layer,) + (0,) * (nd - 1),
                        pipeline_mode=pl.Buffered(1), **kwargs)


def _mixer(alpha, layer, x, w_in, pool_w, pool_scale, conv_w, conv_b, cln_g, cln_b, conv_pw, w_out, ln_g, ln_b):
    B, S, D = x.shape
    ts = TS_MIX
    n_tiles = B * S // ts
    consts = [_per_layer(c) for c in (w_in, pool_w, pool_scale, conv_w, conv_b, cln_g, cln_b, conv_pw, w_out,
                                      ln_g, ln_b)]
    out = pl.pallas_call(
        functools.partial(_mixer_kernel, alpha, S // ts),
        grid=(n_tiles + 1,),
        in_specs=[pl.BlockSpec((ts, D), lambda t: (jnp.minimum(t, n_tiles - 1), 0))]
                 + [_layer_spec(c, layer) for c in consts],
        out_specs=pl.BlockSpec((ts, D), lambda t: (jnp.maximum(t - 1, 0), 0)),
        out_shape=jax.ShapeDtypeStruct((B * S, D), F32),
        scratch_shapes=[pltpu.VMEM((D_POOL // LANES, HALO + ts, LANES), F32),
                        pltpu.VMEM((D_CONV // LANES, HALO + ts, LANES), F32),
                        pltpu.VMEM((ts, D_IN_PROJ), F32),
                        pltpu.VMEM((ts, D_CONV), BF16),
                        pltpu.VMEM((ts, D_POOL), BF16),
                        pltpu.VMEM((ts, D), F32),
                        pltpu.VMEM(w_in.shape[1:], BF16), pltpu.VMEM(pool_w.shape[1:], BF16),
                        pltpu.VMEM(conv_pw.shape[1:], BF16), pltpu.VMEM(w_out.shape[1:], BF16)],
        compiler_params=pltpu.CompilerParams(
            dimension_semantics=("arbitrary",), vmem_limit_bytes=VMEM_LIMIT_BYTES),
        name="mixer",
    )(x.reshape(B * S, D), *consts)
    return out.reshape(B, S, D)


def _norm_and_embed(alpha, x, f, p, ln_g, ln_b, gate_w, ple_w):
    h = _layer_norm(alpha * x + f, ln_g, ln_b)
    gate = _sigmoid(_dot(h.astype(BF16), gate_w))
    return h + gate * _dot(p.astype(BF16), ple_w)


def _load_bf16_weights(jobs, stage_ref, sems):
    rows = stage_ref.shape[1]
    chunks = [(src, dst, r0) for src, dst in jobs for r0 in range(0, dst.shape[0], rows)]

    def copy(n):
        src, dst, r0 = chunks[n]
        return pltpu.make_async_copy(src.at[pl.ds(r0, rows), :], stage_ref.at[n % 2], sems.at[n % 2])

    copy(0).start()
    for n, (src, dst, r0) in enumerate(chunks):
        if n + 1 < len(chunks):
            copy(n + 1).start()
        copy(n).wait()
        dst[r0:r0 + rows, :] = stage_ref[n % 2].astype(BF16)


def _dense_ffn_kernel(alpha, layer, x_ref, p_ref, w1_hbm, w3_hbm, w2_hbm, ln_g_ref, ln_b_ref, gate_w_hbm,
                      ple_w_hbm, o_ref, acc_ref, w1_ref, w3_ref, w2_ref, gate_w_ref, ple_w_ref,
                      wide_stage, narrow_stage, sems):
    @pl.when(pl.program_id(0) == 0)
    def _():
        j = layer // 2
        _load_bf16_weights([(w1_hbm.at[j], w1_ref), (w3_hbm.at[j], w3_ref)], wide_stage, sems)
        _load_bf16_weights([(w2_hbm.at[j], w2_ref), (gate_w_hbm.at[layer], gate_w_ref),
                            (ple_w_hbm.at[layer], ple_w_ref)], narrow_stage, sems)

    x = x_ref[...]
    xb = x.astype(BF16)
    ff = w1_ref.shape[1]
    for c0 in range(0, ff, FF_CHUNK_DENSE):
        c1 = min(c0 + FF_CHUNK_DENSE, ff)
        a = _dot(xb, w1_ref[:, c0:c1])
        b = _dot(xb, w3_ref[:, c0:c1])
        h = (a * _sigmoid(a) * b).astype(BF16)
        part = _dot(h, w2_ref[c0:c1, :])
        if c0 == 0:
            acc_ref[...] = part
        else:
            acc_ref[...] += part
    o_ref[...] = _norm_and_embed(alpha, x, acc_ref[...], p_ref[...], ln_g_ref[...], ln_b_ref[...],
                                 gate_w_ref[...], ple_w_ref[...])


def _dense_layer(alpha, layer, x, p, w1, w3, w2, ln_g, ln_b, gate_w, ple_w):
    T, D = x.shape
    tm = TM_DENSE
    FF = w1.shape[2]
    ln_g, ln_b = _per_layer(ln_g), _per_layer(ln_b)
    hbm = pl.BlockSpec(memory_space=pl.ANY)
    return pl.pallas_call(
        functools.partial(_dense_ffn_kernel, alpha, layer),
        grid=(T // tm,),
        in_specs=[pl.BlockSpec((tm, D), lambda i: (i, 0)),
                  pl.BlockSpec((None, tm, D_PLE), lambda i: (layer, i, 0)),
                  hbm, hbm, hbm, _layer_spec(ln_g, layer), _layer_spec(ln_b, layer), hbm, hbm],
        out_specs=pl.BlockSpec((tm, D), lambda i: (i, 0)),
        out_shape=jax.ShapeDtypeStruct((T, D), F32),
        scratch_shapes=[pltpu.VMEM((tm, D), F32),
                        pltpu.VMEM((D, FF), BF16), pltpu.VMEM((D, FF), BF16), pltpu.VMEM((FF, D), BF16),
                        pltpu.VMEM((D, D), BF16), pltpu.VMEM((D_PLE, D), BF16),
                        pltpu.VMEM((2, WIDE_STAGE_ROWS, FF), F32), pltpu.VMEM((2, NARROW_STAGE_ROWS, D), F32),
                        pltpu.SemaphoreType.DMA((2,))],
        compiler_params=pltpu.CompilerParams(
            dimension_semantics=("arbitrary",), vmem_limit_bytes=VMEM_LIMIT_BYTES),
        name="dense_ffn",
    )(x, p, w1, w3, w2, ln_g, ln_b, gate_w, ple_w)


def _router_kernel(x_ref, wt_ref, idx_ref, gate_ref, cnt_ref, carry_ref):
    i = pl.program_id(0)
    tr = x_ref.shape[0]

    @pl.when(i == 0)
    def _():
        carry_ref[...] = jnp.zeros_like(carry_ref)

    x = x_ref[...]
    wt = wt_ref[...]
    xh = x.astype(BF16)
    xl = (x - xh.astype(F32)).astype(BF16)
    wh = wt.astype(BF16)
    wl = (wt - wh.astype(F32)).astype(BF16)
    nt_dims = (((1,), (1,)), ((), ()))
    dg = lambda a, b: lax.dot_general(a, b, nt_dims, preferred_element_type=F32)
    logits = dg(wh, xh) + (dg(wh, xl) + dg(wl, xh))

    eid = lax.broadcasted_iota(jnp.int32, logits.shape, 0)
    m1 = jnp.max(logits, axis=0, keepdims=True)
    i1 = jnp.min(jnp.where(logits == m1, eid, N_EXPERTS), axis=0, keepdims=True)
    rest = jnp.where(eid == i1, -jnp.inf, logits)
    m2 = jnp.max(rest, axis=0, keepdims=True)
    i2 = jnp.min(jnp.where(rest == m2, eid, N_EXPERTS), axis=0, keepdims=True)
    e2 = jnp.exp(m2 - m1)
    g1 = 1.0 / (1.0 + e2)
    g2 = e2 / (1.0 + e2)

    oh1 = (eid == i1).astype(F32)
    oh2 = (eid == i2).astype(F32)
    chosen = oh1 + oh2
    r_i = lax.broadcasted_iota(jnp.int32, (tr, tr), 0)
    c_i = lax.broadcasted_iota(jnp.int32, (tr, tr), 1)
    before = (r_i < c_i).astype(BF16)
    excl = _dot(chosen.astype(BF16), before) + carry_ref[:, 0:1]
    rank1 = jnp.sum(oh1 * excl, axis=0, keepdims=True)
    rank2 = jnp.sum(oh2 * excl, axis=0, keepdims=True)

    total = carry_ref[:, 0:1] + jnp.sum(chosen, axis=1, keepdims=True)
    carry_ref[...] = jnp.broadcast_to(total, carry_ref.shape)
    cnt_ref[...] = jnp.broadcast_to(total, cnt_ref.shape).astype(jnp.int32)

    zi = jnp.zeros((SUBLANES - 4, tr), jnp.int32)
    idx_ref[...] = jnp.concatenate(
        [i1, i2, rank1.astype(jnp.int32), rank2.astype(jnp.int32), zi], axis=0)
    gate_ref[...] = jnp.concatenate([g1, g2, jnp.zeros((SUBLANES - 2, tr), F32)], axis=0)


def _router(x, router_w):
    T, D = x.shape
    tr = TR_ROUTE
    return pl.pallas_call(
        _router_kernel,
        grid=(T // tr,),
        in_specs=[pl.BlockSpec((tr, D), lambda i: (i, 0)), _const_spec((N_EXPERTS, D))],
        out_specs=[pl.BlockSpec((SUBLANES, tr), lambda i: (0, i)),
                   pl.BlockSpec((SUBLANES, tr), lambda i: (0, i)),
                   pl.BlockSpec((N_EXPERTS, LANES), lambda i: (0, 0))],
        out_shape=[jax.ShapeDtypeStruct((SUBLANES, T), jnp.int32),
                   jax.ShapeDtypeStruct((SUBLANES, T), F32),
                   jax.ShapeDtypeStruct((N_EXPERTS, LANES), jnp.int32)],
        scratch_shapes=[pltpu.VMEM((N_EXPERTS, LANES), F32)],
        compiler_params=pltpu.CompilerParams(
            dimension_semantics=("arbitrary",), vmem_limit_bytes=VMEM_LIMIT_BYTES),
        name="router",
    )(x, router_w.T)


def _store_token_tiles(dst_ref, value):
    rows = value.shape[0]
    for c in range(ROW_CHUNKS):
        dst_ref[pl.ds(c, rows, stride=ROW_CHUNKS), :] = value[:, c * LANES:(c + 1) * LANES]


def _load_token_tile_chunk(src_ref, rows, c):
    return src_ref[pl.ds(c, rows, stride=ROW_CHUNKS), :]


def _load_token_tiles(src_ref, rows):
    return jnp.concatenate([_load_token_tile_chunk(src_ref, rows, c) for c in range(ROW_CHUNKS)], axis=-1)


def _tile_rows(row):
    return pl.ds(pl.multiple_of(row * ROW_CHUNKS, ROW_CHUNKS), ROW_CHUNKS)


def _dispatch_kernel(dest_ref, fill_ref, x_ref, xs_hbm, xt_ref, zero_ref, sems, fill_sem):
    i = pl.program_id(0)
    n = pl.num_programs(0)
    tc = x_ref.shape[0]
    n_tok = n * tc
    base = i * tc
    slot = i % 2
    stage = xt_ref.at[slot]

    @pl.when(i == 0)
    def _():
        zero_ref[...] = jnp.zeros_like(zero_ref)
        tile_rows = zero_ref.shape[0]

        def fill(j):
            start = pl.multiple_of(fill_ref[j] * tile_rows, tile_rows)
            return pltpu.make_async_copy(zero_ref, xs_hbm.at[pl.ds(start, tile_rows), :], fill_sem)

        for j in range(fill_ref.shape[0]):
            @pl.when(fill_ref[j] >= 0)
            def _():
                fill(j).start()

        for j in range(fill_ref.shape[0]):
            @pl.when(fill_ref[j] >= 0)
            def _():
                fill(j).wait()

    _store_token_tiles(stage, x_ref[...])

    def copy(k, r):
        d = dest_ref[k * n_tok + base + r]
        return pltpu.make_async_copy(stage.at[_tile_rows(r), :], xs_hbm.at[_tile_rows(d), :], sems.at[slot])

    def issue(rb, c):
        for j in range(ISSUE_UNROLL):
            r = rb * ISSUE_UNROLL + j
            for k in range(TOP_K):
                copy(k, r).start(priority=k)
        return c

    lax.fori_loop(0, tc // ISSUE_UNROLL, issue, 0)

    def wait_step(s):
        for _ in range(TOP_K):
            pltpu.make_async_copy(xt_ref.at[s], xs_hbm.at[pl.ds(0, tc * ROW_CHUNKS), :], sems.at[s]).wait()

    @pl.when(i > 0)
    def _():
        wait_step(1 - slot)

    @pl.when(i == n - 1)
    def _():
        wait_step(slot)


def _dispatch(x, dest, fill_tiles, n_rows):
    T, D = x.shape
    tc = TC_DISPATCH
    return pl.pallas_call(
        _dispatch_kernel,
        grid_spec=pltpu.PrefetchScalarGridSpec(
            num_scalar_prefetch=2, grid=(T // tc,),
            in_specs=[pl.BlockSpec((tc, D), lambda i, d, ft: (i, 0))],
            out_specs=pl.BlockSpec(memory_space=pl.ANY),
            scratch_shapes=[pltpu.VMEM((2, tc * ROW_CHUNKS, LANES), F32),
                            pltpu.VMEM((TM_GROUP * ROW_CHUNKS, LANES), F32),
                            pltpu.SemaphoreType.DMA((2,)), pltpu.SemaphoreType.DMA(())]),
        out_shape=jax.ShapeDtypeStruct((n_rows * ROW_CHUNKS, LANES), F32),
        compiler_params=pltpu.CompilerParams(dimension_semantics=("arbitrary",)),
        name="dispatch",
    )(dest, fill_tiles, x)


def _expert_ffn_kernel(te_ref, nt_ref, xs_ref, w1_ref, w3_ref, w2_ref, o_ref, xb_ref, acc_ref):
    i = pl.program_id(0)
    f = pl.program_id(1)
    nf = pl.num_programs(1)
    tm = acc_ref.shape[0]

    @pl.when(i < nt_ref[0])
    def _():
        @pl.when(f == 0)
        def _():
            for c in range(ROW_CHUNKS):
                xb_ref[:, c * LANES:(c + 1) * LANES] = _load_token_tile_chunk(xs_ref, tm, c).astype(BF16)

        xb = xb_ref[...]
        a = _dot(xb, w1_ref[...])
        b = _dot(xb, w3_ref[...])
        h = (a * _sigmoid(a) * b).astype(BF16)
        part = _dot(h, w2_ref[...])

        @pl.when(f == 0)
        def _():
            acc_ref[...] = part

        @pl.when((f > 0) & (f < nf - 1))
        def _():
            acc_ref[...] += part

        @pl.when(f == nf - 1)
        def _():
            _store_token_tiles(o_ref, acc_ref[...] + part)

    @pl.when((i >= nt_ref[0]) & (f == 0))
    def _():
        o_ref[...] = jnp.zeros_like(o_ref)


def _expert_ffn(xs, te, nt, w1, w3, w2):
    E, D, FF = w1.shape
    tm, cf = TM_GROUP, FF_CHUNK_MOE
    nf = FF // cf
    max_tiles = xs.shape[0] // (tm * ROW_CHUNKS)

    def tile(i, nt_ref):
        return jnp.minimum(i, nt_ref[0] - 1)

    def chunk(i, f, nt_ref):
        return jnp.where(i < nt_ref[0], f, nf - 1)

    return pl.pallas_call(
        _expert_ffn_kernel,
        grid_spec=pltpu.PrefetchScalarGridSpec(
            num_scalar_prefetch=2, grid=(max_tiles, nf),
            in_specs=[pl.BlockSpec((tm * ROW_CHUNKS, LANES), lambda i, f, te, nt: (tile(i, nt), 0)),
                      pl.BlockSpec((None, D, cf), lambda i, f, te, nt: (te[tile(i, nt)], 0, chunk(i, f, nt))),
                      pl.BlockSpec((None, D, cf), lambda i, f, te, nt: (te[tile(i, nt)], 0, chunk(i, f, nt))),
                      pl.BlockSpec((None, cf, D), lambda i, f, te, nt: (te[tile(i, nt)], chunk(i, f, nt), 0))],
            out_specs=pl.BlockSpec((tm * ROW_CHUNKS, LANES), lambda i, f, te, nt: (i, 0)),
            scratch_shapes=[pltpu.VMEM((tm, D), BF16), pltpu.VMEM((tm, D), F32)]),
        out_shape=jax.ShapeDtypeStruct(xs.shape, F32),
        compiler_params=pltpu.CompilerParams(
            dimension_semantics=("arbitrary", "arbitrary"), vmem_limit_bytes=VMEM_LIMIT_BYTES),
        name="expert_ffn",
    )(te, nt, xs, w1, w3, w2)


def _sorted_layout(cnt, n_assign):
    tm = TM_GROUP
    max_tiles = n_assign // tm + N_EXPERTS
    tiles_e = (cnt + tm - 1) // tm
    tile_end = jnp.cumsum(tiles_e)
    row_start = (tile_end - tiles_e) * tm
    nt = tile_end[-1]
    tile_ids = jnp.minimum(jnp.arange(max_tiles, dtype=jnp.int32), nt - 1)
    te = jnp.sum((tile_end[None, :] <= tile_ids[:, None]).astype(jnp.int32), axis=1)
    te = jnp.minimum(te, N_EXPERTS - 1)
    last_tile = jnp.where(tiles_e > 0, tile_end - 1, -1)
    tail = nt + jnp.arange(N_EXPERTS, dtype=jnp.int32)
    tail = jnp.where(tail < max_tiles, tail, -1)
    fill_tiles = jnp.concatenate([last_tile, tail]).astype(jnp.int32)
    return row_start, te, nt.reshape(1).astype(jnp.int32), fill_tiles, max_tiles * tm


def _combine_kernel(alpha, dest_ref, x_ref, p_ref, gate_ref, ys_hbm, ln_g_ref, ln_b_ref, gate_w_f32, ple_w_f32,
                    o_ref, yb_ref, gate_w_ref, ple_w_ref, sems):
    i = pl.program_id(0)
    n = pl.num_programs(0)
    tc = x_ref.shape[0]
    n_tok = n * tc
    slot = i % 2

    @pl.when(i == 0)
    def _():
        gate_w_ref[...] = gate_w_f32[...].astype(BF16)
        ple_w_ref[...] = ple_w_f32[...].astype(BF16)

    def issue_step(step, s):
        base = step * tc

        def issue(rb, c):
            for j in range(ISSUE_UNROLL):
                r = rb * ISSUE_UNROLL + j
                for k in range(TOP_K):
                    d = dest_ref[k * n_tok + base + r]
                    pltpu.make_async_copy(ys_hbm.at[_tile_rows(d), :], yb_ref.at[s, k, _tile_rows(r), :],
                                          sems.at[s]).start()
            return c

        lax.fori_loop(0, tc // ISSUE_UNROLL, issue, 0)

    @pl.when(i == 0)
    def _():
        issue_step(0, 0)

    @pl.when(i + 1 < n)
    def _():
        issue_step(i + 1, 1 - slot)

    for k in range(TOP_K):
        pltpu.make_async_copy(ys_hbm.at[pl.ds(0, tc * ROW_CHUNKS), :], yb_ref.at[slot, k], sems.at[slot]).wait()

    g = gate_ref[...]
    f = (g[:, 0:1] * _load_token_tiles(yb_ref.at[slot, 0], tc)
         + g[:, 1:2] * _load_token_tiles(yb_ref.at[slot, 1], tc))
    o_ref[...] = _norm_and_embed(alpha, x_ref[...], f, p_ref[...], ln_g_ref[...], ln_b_ref[...],
                                 gate_w_ref[...], ple_w_ref[...])


def _combine_layer(alpha, layer, x, p, gates, dest, ys, ln_g, ln_b, gate_w, ple_w):
    T, D = x.shape
    tc = TC_COMBINE
    consts = [_per_layer(c) for c in (ln_g, ln_b, gate_w, ple_w)]
    return pl.pallas_call(
        functools.partial(_combine_kernel, alpha),
        grid_spec=pltpu.PrefetchScalarGridSpec(
            num_scalar_prefetch=1, grid=(T // tc,),
            in_specs=[pl.BlockSpec((tc, D), lambda i, d: (i, 0)),
                      pl.BlockSpec((None, tc, D_PLE), lambda i, d: (layer, i, 0)),
                      pl.BlockSpec((tc, TOP_K), lambda i, d: (i, 0)),
                      pl.BlockSpec(memory_space=pl.ANY)]
                     + [_layer_spec(c, layer) for c in consts],
            out_specs=pl.BlockSpec((tc, D), lambda i, d: (i, 0)),
            scratch_shapes=[pltpu.VMEM((2, TOP_K, tc * ROW_CHUNKS, LANES), F32),
                            pltpu.VMEM((D, D), BF16), pltpu.VMEM((D_PLE, D), BF16),
                            pltpu.SemaphoreType.DMA((2,))]),
        out_shape=jax.ShapeDtypeStruct((T, D), F32),
        compiler_params=pltpu.CompilerParams(
            dimension_semantics=("arbitrary",), vmem_limit_bytes=VMEM_LIMIT_BYTES),
        name="combine",
    )(dest, x, p, gates, ys, *consts)


def _moe_layer(alpha, layer, x, p, router_w, w1, w3, w2, ln_g, ln_b, gate_w, ple_w):
    T, D = x.shape
    route_i, route_g, counts = _router(x, router_w)
    idx = route_i[0:TOP_K]
    rank = route_i[TOP_K:2 * TOP_K]
    gates = route_g[0:TOP_K].T

    start, te, nt, fill_tiles, n_rows = _sorted_layout(counts[:, 0], TOP_K * T)
    start_of = sum(jnp.where(idx == e, start[e], 0) for e in range(N_EXPERTS))
    dest = (start_of + rank).astype(jnp.int32).reshape(-1)

    xs = _dispatch(x, dest, fill_tiles, n_rows)
    ys = _expert_ffn(xs, te, nt, w1.astype(BF16), w3.astype(BF16), w2.astype(BF16))
    return _combine_layer(alpha, layer, x, p, gates, dest, ys, ln_g, ln_b, gate_w, ple_w)


def kernel(x, p, w_in, pool_w, pool_scale, conv_w, conv_b, conv_ln_g, conv_ln_b, conv_pw, w_out, ln1_g, ln1_b,
           dense_w1, dense_w3, dense_w2, router_w, exp_w1, exp_w3, exp_w2, ln2_g, ln2_b, ple_gate_w, ple_w):
    depth = w_in.shape[0]
    alpha = (2.0 * depth) ** 0.25
    B, S, D = x.shape
    pt = p.reshape(depth, B * S, D_PLE)
    for i in range(depth):
        x = _mixer(alpha, i, x, w_in, pool_w, pool_scale, conv_w, conv_b, conv_ln_g, conv_ln_b, conv_pw, w_out,
                   ln1_g, ln1_b)
        xt = x.reshape(B * S, D)
        j = i // 2
        if i % 2 == 0:
            xt = _dense_layer(alpha, i, xt, pt, dense_w1, dense_w3, dense_w2, ln2_g, ln2_b, ple_gate_w, ple_w)
        else:
            xt = _moe_layer(alpha, i, xt, pt, router_w[j], exp_w1[j], exp_w3[j], exp_w2[j], ln2_g, ln2_b,
                            ple_gate_w, ple_w)
        x = xt.reshape(B, S, D)
    return x
```

```python
import functools

import jax
import jax.numpy as jnp
from jax import lax
from jax.experimental import pallas as pl
from jax.experimental.pallas import tpu as pltpu

D_MODEL = 1024
D_PLE = 256
D_POOL = 512
D_CONV = D_MODEL - D_POOL
POOL_WINDOWS = (2, 4, 8, 16)
POOL_GROUP_DIM = D_POOL // len(POOL_WINDOWS)
CONV_WIDTH = 31
D_IN_PROJ = D_POOL + 2 * D_CONV
N_EXPERTS = 8
TOP_K = 2
LN_EPS = 1e-5

F32 = jnp.float32
BF16 = jnp.bfloat16

SUBLANES = 8
LANES = 128
VMEM_LIMIT_BYTES = 56 * 1024 * 1024

HALO = 32
TS_MIX = 512
RB_CONV = 128
IN_PROJ_CHUNK = 256
MIX_ROW_GROUPS = 2
TM_DENSE = 512
FF_CHUNK_DENSE = 1024
WIDE_STAGE_ROWS = 128
NARROW_STAGE_ROWS = 256
TR_ROUTE = 512
TM_GROUP = 1024
FF_CHUNK_MOE = 512
TC_DISPATCH = 256
TC_COMBINE = 256
ROW_CHUNKS = D_MODEL // LANES
ISSUE_UNROLL = 8


def _sigmoid(z):
    return 1.0 / (1.0 + jnp.exp(-z))


def _layer_norm(h, g, b):
    mu = jnp.mean(h, axis=-1, keepdims=True)
    c = h - mu
    var = jnp.mean(c * c, axis=-1, keepdims=True)
    return c * lax.rsqrt(var + LN_EPS) * g + b


def _dot(a, b):
    return jnp.dot(a, b, preferred_element_type=F32)


def _mixer_kernel(alpha, tiles_per_seq, x_ref, w_in_f32, pool_w_f32, pool_scale_ref, conv_w_ref, conv_b_ref,
                  cln_g_ref, cln_b_ref, conv_pw_f32, w_out_f32, ln_g_ref, ln_b_ref,
                  o_ref, p_scr, v_scr, u_scr, y_scr, ypool_scr, xprev_scr, w_in_ref, pool_w_ref, conv_pw_ref,
                  w_out_ref):
    ts = x_ref.shape[0]
    t = pl.program_id(0)

    @pl.when(t == 0)
    def _():
        w_in_ref[...] = w_in_f32[...].astype(BF16)
        pool_w_ref[...] = pool_w_f32[...].astype(BF16)
        conv_pw_ref[...] = conv_pw_f32[...].astype(BF16)
        w_out_ref[...] = w_out_f32[...].astype(BF16)
        p_scr[...] = jnp.zeros_like(p_scr)
        v_scr[...] = jnp.zeros_like(v_scr)
        ypool_scr[...] = jnp.zeros_like(ypool_scr)
        xprev_scr[...] = jnp.zeros_like(xprev_scr)

    x = x_ref[...]
    xb = x.astype(BF16)
    n_conv_blocks = ts // RB_CONV
    in_proj_chunks = [(c0, min(c0 + IN_PROJ_CHUNK, D_IN_PROJ)) for c0 in range(0, D_IN_PROJ, IN_PROJ_CHUNK)]

    blocks_per_group = n_conv_blocks // MIX_ROW_GROUPS
    for rb in range(n_conv_blocks):
        r0 = rb * RB_CONV
        cols = []
        for lc in range(D_CONV // LANES):
            lanes = slice(lc * LANES, (lc + 1) * LANES)
            acc = jnp.zeros((RB_CONV, LANES), F32)
            for k in range(CONV_WIDTH):
                first = r0 + HALO - (CONV_WIDTH - 1 - k)
                acc = acc + v_scr[lc, first:first + RB_CONV, :] * conv_w_ref[k:k + 1, lanes]
            cols.append(acc)
        y = jnp.concatenate(cols, axis=-1) + conv_b_ref[...]
        z = _layer_norm(y, cln_g_ref[...], cln_b_ref[...])
        y_scr[r0:r0 + RB_CONV, :] = (z * _sigmoid(z)).astype(BF16)
        if rb < len(in_proj_chunks):
            c0, c1 = in_proj_chunks[rb]
            u_scr[:, c0:c1] = _dot(xb, w_in_ref[:, c0:c1])
        if (rb + 1) % blocks_per_group == 0:
            rows = slice((rb + 1 - blocks_per_group) * RB_CONV, (rb + 1) * RB_CONV)
            y_conv = _dot(y_scr[rows, :], conv_pw_ref[...])
            heads = jnp.concatenate([ypool_scr[rows, :], y_conv.astype(BF16)], axis=-1)
            mix = _dot(heads, w_out_ref[...])
            o_ref[rows, :] = _layer_norm(alpha * xprev_scr[rows, :] + mix, ln_g_ref[...], ln_b_ref[...])
    for c0, c1 in in_proj_chunks[n_conv_blocks:]:
        u_scr[:, c0:c1] = _dot(xb, w_in_ref[:, c0:c1])

    tile_in_seq = t % tiles_per_seq
    starts_seq = tile_in_seq == 0
    p_scr[:, 0:HALO, :] = jnp.where(starts_seq, 0.0, p_scr[:, ts:ts + HALO, :])
    v_scr[:, 0:HALO, :] = jnp.where(starts_seq, 0.0, v_scr[:, ts:ts + HALO, :])
    for c in range(D_CONV // LANES):
        val = u_scr[:, D_POOL + c * LANES:D_POOL + (c + 1) * LANES]
        gate = u_scr[:, D_POOL + D_CONV + c * LANES:D_POOL + D_CONV + (c + 1) * LANES]
        v_scr[c, HALO:, :] = val * _sigmoid(gate)

    pos = tile_in_seq * ts + lax.broadcasted_iota(jnp.int32, (ts, POOL_GROUP_DIM), 0)
    for g, w in enumerate(POOL_WINDOWS):
        lanes = slice(g * POOL_GROUP_DIM, (g + 1) * POOL_GROUP_DIM)
        cur = u_scr[:, lanes]
        p_scr[g, HALO:, :] = cur
        win = cur
        for back in range(1, w):
            win = win + p_scr[g, HALO - back:HALO - back + ts, :]
        cnt = jnp.minimum(pos + 1, w).astype(F32)
        d = win / cnt - cur
        yg = _dot(d.astype(BF16), pool_w_ref[g]) * pool_scale_ref[:, lanes]
        ypool_scr[:, lanes] = yg.astype(BF16)

    xprev_scr[...] = x


def _const_spec(shape):
    nd = len(shape)
    return pl.BlockSpec(shape, lambda *_: (0,) * nd, pipeline_mode=pl.Buffered(1))


def _per_layer(v):
    return v.reshape(v.shape[0], 1, v.shape[1]) if v.ndim == 2 else v


def _layer_spec(v, layer, **kwargs):
    nd = v.ndim
    return pl.BlockSpec((None,) + v.shape[1:], lambda *_: (layer,) + (0,) * (nd - 1),
                        pipeline_mode=pl.Buffered(1), **kwargs)


def _mixer(alpha, layer, x, w_in, pool_w, pool_scale, conv_w, conv_b, cln_g, cln_b, conv_pw, w_out, ln_g, ln_b):
    B, S, D = x.shape
    ts = TS_MIX
    n_tiles = B * S // ts
    consts = [_per_layer(c) for c in (w_in, pool_w, pool_scale, conv_w, conv_b, cln_g, cln_b, conv_pw, w_out,
                                      ln_g, ln_b)]
    out = pl.pallas_call(
        functools.partial(_mixer_kernel, alpha, S // ts),
        grid=(n_tiles + 1,),
        in_specs=[pl.BlockSpec((ts, D), lambda t: (jnp.minimum(t, n_tiles - 1), 0))]
                 + [_layer_spec(c, layer) for c in consts],
        out_specs=pl.BlockSpec((ts, D), lambda t: (jnp.maximum(t - 1, 0), 0)),
        out_shape=jax.ShapeDtypeStruct((B * S, D), F32),
        scratch_shapes=[pltpu.VMEM((D_POOL // LANES, HALO + ts, LANES), F32),
                        pltpu.VMEM((D_CONV // LANES, HALO + ts, LANES), F32),
                        pltpu.VMEM((ts, D_IN_PROJ), F32),
                        pltpu.VMEM((ts, D_CONV), BF16),
                        pltpu.VMEM((ts, D_POOL), BF16),
                        pltpu.VMEM((ts, D), F32),
                        pltpu.VMEM(w_in.shape[1:], BF16), pltpu.VMEM(pool_w.shape[1:], BF16),
                        pltpu.VMEM(conv_pw.shape[1:], BF16), pltpu.VMEM(w_out.shape[1:], BF16)],
        compiler_params=pltpu.CompilerParams(
            dimension_semantics=("arbitrary",), vmem_limit_bytes=VMEM_LIMIT_BYTES),
        name="mixer",
    )(x.reshape(B * S, D), *consts)
    return out.reshape(B, S, D)


def _norm_and_embed(alpha, x, f, p, ln_g, ln_b, gate_w, ple_w):
    h = _layer_norm(alpha * x + f, ln_g, ln_b)
    gate = _sigmoid(_dot(h.astype(BF16), gate_w))
    return h + gate * _dot(p.astype(BF16), ple_w)


def _load_bf16_weights(jobs, stage_ref, sems):
    rows = stage_ref.shape[1]
    chunks = [(src, dst, r0) for src, dst in jobs for r0 in range(0, dst.shape[0], rows)]

    def copy(n):
        src, dst, r0 = chunks[n]
        return pltpu.make_async_copy(src.at[pl.ds(r0, rows), :], stage_ref.at[n % 2], sems.at[n % 2])

    copy(0).start()
    for n, (src, dst, r0) in enumerate(chunks):
        if n + 1 < len(chunks):
            copy(n + 1).start()
        copy(n).wait()
        dst[r0:r0 + rows, :] = stage_ref[n % 2].astype(BF16)


def _dense_ffn_kernel(alpha, layer, x_ref, p_ref, w1_hbm, w3_hbm, w2_hbm, ln_g_ref, ln_b_ref, gate_w_hbm,
                      ple_w_hbm, o_ref, acc_ref, w1_ref, w3_ref, w2_ref, gate_w_ref, ple_w_ref,
                      wide_stage, narrow_stage, sems):
    @pl.when(pl.program_id(0) == 0)
    def _():
        j = layer // 2
        _load_bf16_weights([(w1_hbm.at[j], w1_ref), (w3_hbm.at[j], w3_ref)], wide_stage, sems)
        _load_bf16_weights([(w2_hbm.at[j], w2_ref), (gate_w_hbm.at[layer], gate_w_ref),
                            (ple_w_hbm.at[layer], ple_w_ref)], narrow_stage, sems)

    x = x_ref[...]
    xb = x.astype(BF16)
    ff = w1_ref.shape[1]
    for c0 in range(0, ff, FF_CHUNK_DENSE):
        c1 = min(c0 + FF_CHUNK_DENSE, ff)
        a = _dot(xb, w1_ref[:, c0:c1])
        b = _dot(xb, w3_ref[:, c0:c1])
        h = (a * _sigmoid(a) * b).astype(BF16)
        part = _dot(h, w2_ref[c0:c1, :])
        if c0 == 0:
            acc_ref[...] = part
        else:
            acc_ref[...] += part
    o_ref[...] = _norm_and_embed(alpha, x, acc_ref[...], p_ref[...], ln_g_ref[...], ln_b_ref[...],
                                 gate_w_ref[...], ple_w_ref[...])


def _dense_layer(alpha, layer, x, p, w1, w3, w2, ln_g, ln_b, gate_w, ple_w):
    T, D = x.shape
    tm = TM_DENSE
    FF = w1.shape[2]
    ln_g, ln_b = _per_layer(ln_g), _per_layer(ln_b)
    hbm = pl.BlockSpec(memory_space=pl.ANY)
    return pl.pallas_call(
        functools.partial(_dense_ffn_kernel, alpha, layer),
        grid=(T // tm,),
        in_specs=[pl.BlockSpec((tm, D), lambda i: (i, 0)),
                  pl.BlockSpec((None, tm, D_PLE), lambda i: (layer, i, 0)),
                  hbm, hbm, hbm, _layer_spec(ln_g, layer), _layer_spec(ln_b, layer), hbm, hbm],
        out_specs=pl.BlockSpec((tm, D), lambda i: (i, 0)),
        out_shape=jax.ShapeDtypeStruct((T, D), F32),
        scratch_shapes=[pltpu.VMEM((tm, D), F32),
                        pltpu.VMEM((D, FF), BF16), pltpu.VMEM((D, FF), BF16), pltpu.VMEM((FF, D), BF16),
                        pltpu.VMEM((D, D), BF16), pltpu.VMEM((D_PLE, D), BF16),
                        pltpu.VMEM((2, WIDE_STAGE_ROWS, FF), F32), pltpu.VMEM((2, NARROW_STAGE_ROWS, D), F32),
                        pltpu.SemaphoreType.DMA((2,))],
        compiler_params=pltpu.CompilerParams(
            dimension_semantics=("arbitrary",), vmem_limit_bytes=VMEM_LIMIT_BYTES),
        name="dense_ffn",
    )(x, p, w1, w3, w2, ln_g, ln_b, gate_w, ple_w)


def _router_kernel(x_ref, wt_ref, idx_ref, gate_ref, cnt_ref, carry_ref):
    i = pl.program_id(0)
    tr = x_ref.shape[0]

    @pl.when(i == 0)
    def _():
        carry_ref[...] = jnp.zeros_like(carry_ref)

    x = x_ref[...]
    wt = wt_ref[...]
    xh = x.astype(BF16)
    xl = (x - xh.astype(F32)).astype(BF16)
    wh = wt.astype(BF16)
    wl = (wt - wh.astype(F32)).astype(BF16)
    nt_dims = (((1,), (1,)), ((), ()))
    dg = lambda a, b: lax.dot_general(a, b, nt_dims, preferred_element_type=F32)
    logits = dg(wh, xh) + (dg(wh, xl) + dg(wl, xh))

    eid = lax.broadcasted_iota(jnp.int32, logits.shape, 0)
    m1 = jnp.max(logits, axis=0, keepdims=True)
    i1 = jnp.min(jnp.where(logits == m1, eid, N_EXPERTS), axis=0, keepdims=True)
    rest = jnp.where(eid == i1, -jnp.inf, logits)
    m2 = jnp.max(rest, axis=0, keepdims=True)
    i2 = jnp.min(jnp.where(rest == m2, eid, N_EXPERTS), axis=0, keepdims=True)
    e2 = jnp.exp(m2 - m1)
    g1 = 1.0 / (1.0 + e2)
    g2 = e2 / (1.0 + e2)

    oh1 = (eid == i1).astype(F32)
    oh2 = (eid == i2).astype(F32)
    chosen = oh1 + oh2
    r_i = lax.broadcasted_iota(jnp.int32, (tr, tr), 0)
    c_i = lax.broadcasted_iota(jnp.int32, (tr, tr), 1)
    before = (r_i < c_i).astype(BF16)
    excl = _dot(chosen.astype(BF16), before) + carry_ref[:, 0:1]
    rank1 = jnp.sum(oh1 * excl, axis=0, keepdims=True)
    rank2 = jnp.sum(oh2 * excl, axis=0, keepdims=True)

    total = carry_ref[:, 0:1] + jnp.sum(chosen, axis=1, keepdims=True)
    carry_ref[...] = jnp.broadcast_to(total, carry_ref.shape)
    cnt_ref[...] = jnp.broadcast_to(total, cnt_ref.shape).astype(jnp.int32)

    zi = jnp.zeros((SUBLANES - 4, tr), jnp.int32)
    idx_ref[...] = jnp.concatenate(
        [i1, i2, rank1.astype(jnp.int32), rank2.astype(jnp.int32), zi], axis=0)
    gate_ref[...] = jnp.concatenate([g1, g2, jnp.zeros((SUBLANES - 2, tr), F32)], axis=0)


def _router(x, router_w):
    T, D = x.shape
    tr = TR_ROUTE
    return pl.pallas_call(
        _router_kernel,
        grid=(T // tr,),
        in_specs=[pl.BlockSpec((tr, D), lambda i: (i, 0)), _const_spec((N_EXPERTS, D))],
        out_specs=[pl.BlockSpec((SUBLANES, tr), lambda i: (0, i)),
                   pl.BlockSpec((SUBLANES, tr), lambda i: (0, i)),
                   pl.BlockSpec((N_EXPERTS, LANES), lambda i: (0, 0))],
        out_shape=[jax.ShapeDtypeStruct((SUBLANES, T), jnp.int32),
                   jax.ShapeDtypeStruct((SUBLANES, T), F32),
                   jax.ShapeDtypeStruct((N_EXPERTS, LANES), jnp.int32)],
        scratch_shapes=[pltpu.VMEM((N_EXPERTS, LANES), F32)],
        compiler_params=pltpu.CompilerParams(
            dimension_semantics=("arbitrary",), vmem_limit_bytes=VMEM_LIMIT_BYTES),
        name="router",
    )(x, router_w.T)


def _store_token_tiles(dst_ref, value):
    rows = value.shape[0]
    for c in range(ROW_CHUNKS):
        dst_ref[pl.ds(c, rows, stride=ROW_CHUNKS), :] = value[:, c * LANES:(c + 1) * LANES]


def _load_token_tile_chunk(src_ref, rows, c):
    return src_ref[pl.ds(c, rows, stride=ROW_CHUNKS), :]


def _load_token_tiles(src_ref, rows):
    return jnp.concatenate([_load_token_tile_chunk(src_ref, rows, c) for c in range(ROW_CHUNKS)], axis=-1)


def _tile_rows(row):
    return pl.ds(pl.multiple_of(row * ROW_CHUNKS, ROW_CHUNKS), ROW_CHUNKS)


def _dispatch_kernel(dest_ref, fill_ref, x_ref, xs_hbm, xt_ref, zero_ref, sems, fill_sem):
    i = pl.program_id(0)
    n = pl.num_programs(0)
    tc = x_ref.shape[0]
    n_tok = n * tc
    base = i * tc
    slot = i % 2
    stage = xt_ref.at[slot]

    @pl.when(i == 0)
    def _():
        zero_ref[...] = jnp.zeros_like(zero_ref)
        tile_rows = zero_ref.shape[0]

        def fill(j):
            start = pl.multiple_of(fill_ref[j] * tile_rows, tile_rows)
            return pltpu.make_async_copy(zero_ref, xs_hbm.at[pl.ds(start, tile_rows), :], fill_sem)

        for j in range(fill_ref.shape[0]):
            @pl.when(fill_ref[j] >= 0)
            def _():
                fill(j).start()

        for j in range(fill_ref.shape[0]):
            @pl.when(fill_ref[j] >= 0)
            def _():
                fill(j).wait()

    _store_token_tiles(stage, x_ref[...])

    def copy(k, r):
        d = dest_ref[k * n_tok + base + r]
        return pltpu.make_async_copy(stage.at[_tile_rows(r), :], xs_hbm.at[_tile_rows(d), :], sems.at[slot])

    def issue(rb, c):
        for j in range(ISSUE_UNROLL):
            r = rb * ISSUE_UNROLL + j
            for k in range(TOP_K):
                copy(k, r).start(priority=k)
        return c

    lax.fori_loop(0, tc // ISSUE_UNROLL, issue, 0)

    def wait_step(s):
        for _ in range(TOP_K):
            pltpu.make_async_copy(xt_ref.at[s], xs_hbm.at[pl.ds(0, tc * ROW_CHUNKS), :], sems.at[s]).wait()

    @pl.when(i > 0)
    def _():
        wait_step(1 - slot)

    @pl.when(i == n - 1)
    def _():
        wait_step(slot)


def _dispatch(x, dest, fill_tiles, n_rows):
    T, D = x.shape
    tc = TC_DISPATCH
    return pl.pallas_call(
        _dispatch_kernel,
        grid_spec=pltpu.PrefetchScalarGridSpec(
            num_scalar_prefetch=2, grid=(T // tc,),
            in_specs=[pl.BlockSpec((tc, D), lambda i, d, ft: (i, 0))],
            out_specs=pl.BlockSpec(memory_space=pl.ANY),
            scratch_shapes=[pltpu.VMEM((2, tc * ROW_CHUNKS, LANES), F32),
                            pltpu.VMEM((TM_GROUP * ROW_CHUNKS, LANES), F32),
                            pltpu.SemaphoreType.DMA((2,)), pltpu.SemaphoreType.DMA(())]),
        out_shape=jax.ShapeDtypeStruct((n_rows * ROW_CHUNKS, LANES), F32),
        compiler_params=pltpu.CompilerParams(dimension_semantics=("arbitrary",)),
        name="dispatch",
    )(dest, fill_tiles, x)


def _expert_ffn_kernel(te_ref, nt_ref, xs_ref, w1_ref, w3_ref, w2_ref, o_ref, xb_ref, acc_ref):
    i = pl.program_id(0)
    f = pl.program_id(1)
    nf = pl.num_programs(1)
    tm = acc_ref.shape[0]

    @pl.when(i < nt_ref[0])
    def _():
        @pl.when(f == 0)
        def _():
            for c in range(ROW_CHUNKS):
                xb_ref[:, c * LANES:(c + 1) * LANES] = _load_token_tile_chunk(xs_ref, tm, c).astype(BF16)

        xb = xb_ref[...]
        a = _dot(xb, w1_ref[...].astype(BF16))
        b = _dot(xb, w3_ref[...].astype(BF16))
        h = (a * _sigmoid(a) * b).astype(BF16)
        part = _dot(h, w2_ref[...].astype(BF16))

        @pl.when(f == 0)
        def _():
            acc_ref[...] = part

        @pl.when((f > 0) & (f < nf - 1))
        def _():
            acc_ref[...] += part

        @pl.when(f == nf - 1)
        def _():
            _store_token_tiles(o_ref, acc_ref[...] + part)

    @pl.when((i >= nt_ref[0]) & (f == 0))
    def _():
        o_ref[...] = jnp.zeros_like(o_ref)


def _expert_ffn(xs, te, nt, moe_layer, w1, w3, w2):
    _, E, D, FF = w1.shape
    tm, cf = TM_GROUP, FF_CHUNK_MOE
    nf = FF // cf
    max_tiles = xs.shape[0] // (tm * ROW_CHUNKS)

    def tile(i, nt_ref):
        return jnp.minimum(i, nt_ref[0] - 1)

    def chunk(i, f, nt_ref):
        return jnp.where(i < nt_ref[0], f, nf - 1)

    return pl.pallas_call(
        _expert_ffn_kernel,
        grid_spec=pltpu.PrefetchScalarGridSpec(
            num_scalar_prefetch=2, grid=(max_tiles, nf),
            in_specs=[pl.BlockSpec((tm * ROW_CHUNKS, LANES), lambda i, f, te, nt: (tile(i, nt), 0)),
                      pl.BlockSpec((None, None, D, cf),
                                   lambda i, f, te, nt: (moe_layer, te[tile(i, nt)], 0, chunk(i, f, nt))),
                      pl.BlockSpec((None, None, D, cf),
                                   lambda i, f, te, nt: (moe_layer, te[tile(i, nt)], 0, chunk(i, f, nt))),
                      pl.BlockSpec((None, None, cf, D),
                                   lambda i, f, te, nt: (moe_layer, te[tile(i, nt)], chunk(i, f, nt), 0))],
            out_specs=pl.BlockSpec((tm * ROW_CHUNKS, LANES), lambda i, f, te, nt: (i, 0)),
            scratch_shapes=[pltpu.VMEM((tm, D), BF16), pltpu.VMEM((tm, D), F32)]),
        out_shape=jax.ShapeDtypeStruct(xs.shape, F32),
        compiler_params=pltpu.CompilerParams(
            dimension_semantics=("arbitrary", "arbitrary"), vmem_limit_bytes=VMEM_LIMIT_BYTES),
        name="expert_ffn",
    )(te, nt, xs, w1, w3, w2)


def _sorted_layout(cnt, n_assign):
    tm = TM_GROUP
    max_tiles = n_assign // tm + N_EXPERTS
    tiles_e = (cnt + tm - 1) // tm
    tile_end = jnp.cumsum(tiles_e)
    row_start = (tile_end - tiles_e) * tm
    nt = tile_end[-1]
    tile_ids = jnp.minimum(jnp.arange(max_tiles, dtype=jnp.int32), nt - 1)
    te = jnp.sum((tile_end[None, :] <= tile_ids[:, None]).astype(jnp.int32), axis=1)
    te = jnp.minimum(te, N_EXPERTS - 1)
    last_tile = jnp.where(tiles_e > 0, tile_end - 1, -1)
    tail = nt + jnp.arange(N_EXPERTS, dtype=jnp.int32)
    tail = jnp.where(tail < max_tiles, tail, -1)
    fill_tiles = jnp.concatenate([last_tile, tail]).astype(jnp.int32)
    return row_start, te, nt.reshape(1).astype(jnp.int32), fill_tiles, max_tiles * tm


def _combine_kernel(alpha, dest_ref, x_ref, p_ref, gate_ref, ys_hbm, ln_g_ref, ln_b_ref, gate_w_f32, ple_w_f32,
                    o_ref, yb_ref, gate_w_ref, ple_w_ref, sems):
    i = pl.program_id(0)
    n = pl.num_programs(0)
    tc = x_ref.shape[0]
    n_tok = n * tc
    slot = i % 2

    @pl.when(i == 0)
    def _():
        gate_w_ref[...] = gate_w_f32[...].astype(BF16)
        ple_w_ref[...] = ple_w_f32[...].astype(BF16)

    def issue_step(step, s):
        base = step * tc

        def issue(rb, c):
            for j in range(ISSUE_UNROLL):
                r = rb * ISSUE_UNROLL + j
                for k in range(TOP_K):
                    d = dest_ref[k * n_tok + base + r]
                    pltpu.make_async_copy(ys_hbm.at[_tile_rows(d), :], yb_ref.at[s, k, _tile_rows(r), :],
                                          sems.at[s]).start()
            return c

        lax.fori_loop(0, tc // ISSUE_UNROLL, issue, 0)

    @pl.when(i == 0)
    def _():
        issue_step(0, 0)

    @pl.when(i + 1 < n)
    def _():
        issue_step(i + 1, 1 - slot)

    for k in range(TOP_K):
        pltpu.make_async_copy(ys_hbm.at[pl.ds(0, tc * ROW_CHUNKS), :], yb_ref.at[slot, k], sems.at[slot]).wait()

    g = gate_ref[...]
    f = (g[:, 0:1] * _load_token_tiles(yb_ref.at[slot, 0], tc)
         + g[:, 1:2] * _load_token_tiles(yb_ref.at[slot, 1], tc))
    o_ref[...] = _norm_and_embed(alpha, x_ref[...], f, p_ref[...], ln_g_ref[...], ln_b_ref[...],
                                 gate_w_ref[...], ple_w_ref[...])


def _combine_layer(alpha, layer, x, p, gates, dest, ys, ln_g, ln_b, gate_w, ple_w):
    T, D = x.shape
    tc = TC_COMBINE
    consts = [_per_layer(c) for c in (ln_g, ln_b, gate_w, ple_w)]
    return pl.pallas_call(
        functools.partial(_combine_kernel, alpha),
        grid_spec=pltpu.PrefetchScalarGridSpec(
            num_scalar_prefetch=1, grid=(T // tc,),
            in_specs=[pl.BlockSpec((tc, D), lambda i, d: (i, 0)),
                      pl.BlockSpec((None, tc, D_PLE), lambda i, d: (layer, i, 0)),
                      pl.BlockSpec((tc, TOP_K), lambda i, d: (i, 0)),
                      pl.BlockSpec(memory_space=pl.ANY)]
                     + [_layer_spec(c, layer) for c in consts],
            out_specs=pl.BlockSpec((tc, D), lambda i, d: (i, 0)),
            scratch_shapes=[pltpu.VMEM((2, TOP_K, tc * ROW_CHUNKS, LANES), F32),
                            pltpu.VMEM((D, D), BF16), pltpu.VMEM((D_PLE, D), BF16),
                            pltpu.SemaphoreType.DMA((2,))]),
        out_shape=jax.ShapeDtypeStruct((T, D), F32),
        compiler_params=pltpu.CompilerParams(
            dimension_semantics=("arbitrary",), vmem_limit_bytes=VMEM_LIMIT_BYTES),
        name="combine",
    )(dest, x, p, gates, ys, *consts)


def _moe_layer(alpha, layer, x, p, router_w, w1, w3, w2, ln_g, ln_b, gate_w, ple_w):
    T, D = x.shape
    route_i, route_g, counts = _router(x, router_w)
    idx = route_i[0:TOP_K]
    rank = route_i[TOP_K:2 * TOP_K]
    gates = route_g[0:TOP_K].T

    start, te, nt, fill_tiles, n_rows = _sorted_layout(counts[:, 0], TOP_K * T)
    start_of = sum(jnp.where(idx == e, start[e], 0) for e in range(N_EXPERTS))
    dest = (start_of + rank).astype(jnp.int32).reshape(-1)

    xs = _dispatch(x, dest, fill_tiles, n_rows)
    ys = _expert_ffn(xs, te, nt, layer // 2, w1, w3, w2)
    return _combine_layer(alpha, layer, x, p, gates, dest, ys, ln_g, ln_b, gate_w, ple_w)


def kernel(x, p, w_in, pool_w, pool_scale, conv_w, conv_b, conv_ln_g, conv_ln_b, conv_pw, w_out, ln1_g, ln1_b,
           dense_w1, dense_w3, dense_w2, router_w, exp_w1, exp_w3, exp_w2, ln2_g, ln2_b, ple_gate_w, ple_w):
    depth = w_in.shape[0]
    alpha = (2.0 * depth) ** 0.25
    B, S, D = x.shape
    pt = p.reshape(depth, B * S, D_PLE)
    for i in range(depth):
        x = _mixer(alpha, i, x, w_in, pool_w, pool_scale, conv_w, conv_b, conv_ln_g, conv_ln_b, conv_pw, w_out,
                   ln1_g, ln1_b)
        xt = x.reshape(B * S, D)
        j = i // 2
        if i % 2 == 0:
            xt = _dense_layer(alpha, i, xt, pt, dense_w1, dense_w3, dense_w2, ln2_g, ln2_b, ple_gate_w, ple_w)
        else:
            xt = _moe_layer(alpha, i, xt, pt, router_w[j], exp_w1, exp_w3, exp_w2, ln2_g, ln2_b,
                            ple_gate_w, ple_w)
        x = xt.reshape(B, S, D)
    return x
```

```python
import functools

import jax
import jax.numpy as jnp
from jax import lax
from jax.experimental import pallas as pl
from jax.experimental.pallas import tpu as pltpu

D_MODEL = 1024
D_PLE = 256
D_POOL = 512
D_CONV = D_MODEL - D_POOL
POOL_WINDOWS = (2, 4, 8, 16)
POOL_GROUP_DIM = D_POOL // len(POOL_WINDOWS)
CONV_WIDTH = 31
D_IN_PROJ = D_POOL + 2 * D_CONV
N_EXPERTS = 8
TOP_K = 2
LN_EPS = 1e-5

F32 = jnp.float32
BF16 = jnp.bfloat16

SUBLANES = 8
LANES = 128
VMEM_LIMIT_BYTES = 56 * 1024 * 1024

HALO = 32
TS_MIX = 512
RB_CONV = 128
IN_PROJ_CHUNK = 256
MIX_ROW_GROUPS = 2
TM_DENSE = 512
FF_CHUNK_DENSE = 1024
WIDE_STAGE_ROWS = 128
NARROW_STAGE_ROWS = 256
TR_ROUTE = 512
TM_GROUP = 512
FF_CHUNK_MOE = 1792
TC_DISPATCH = 256
TC_COMBINE = 256
ROW_CHUNKS = D_MODEL // LANES
ISSUE_UNROLL = 8


def _sigmoid(z):
    return 1.0 / (1.0 + jnp.exp(-z))


def _layer_norm(h, g, b):
    mu = jnp.mean(h, axis=-1, keepdims=True)
    c = h - mu
    var = jnp.mean(c * c, axis=-1, keepdims=True)
    return c * lax.rsqrt(var + LN_EPS) * g + b


def _dot(a, b):
    return jnp.dot(a, b, preferred_element_type=F32)


def _expert_cast_copies(moe_layer, q, src_hbm, dst_hbm, in_bufs, out_bufs, in_sems, out_sems):
    loads, stores = [], []
    for a in range(len(src_hbm)):
        rows = in_bufs[a].shape[0]
        per_expert = src_hbm[a].shape[2] // rows
        e = q // per_expert
        r0 = pl.multiple_of((q % per_expert) * rows, rows)
        loads.append(pltpu.make_async_copy(src_hbm[a].at[moe_layer, e, pl.ds(r0, rows), :], in_bufs[a],
                                           in_sems.at[a]))
        stores.append(pltpu.make_async_copy(out_bufs[a], dst_hbm[a].at[e, pl.ds(r0, rows), :], out_sems.at[a]))
    return loads, stores


def _mixer_kernel(alpha, tiles_per_seq, cast_moe_layer, x_ref, w_in_f32, pool_w_f32, pool_scale_ref, conv_w_ref,
                  conv_b_ref, cln_g_ref, cln_b_ref, conv_pw_f32, w_out_f32, ln_g_ref, ln_b_ref, *rest):
    if cast_moe_layer is None:
        ew_f32, ew_bf16, cast_scratch = (), (), ()
        (o_ref, p_scr, v_scr, u_scr, y_scr, ypool_scr, xprev_scr, w_in_ref, pool_w_ref, conv_pw_ref,
         w_out_ref) = rest
    else:
        ew_f32, rest = rest[:3], rest[3:]
        o_ref, ew_bf16, rest = rest[0], rest[1:4], rest[4:]
        (p_scr, v_scr, u_scr, y_scr, ypool_scr, xprev_scr, w_in_ref, pool_w_ref, conv_pw_ref,
         w_out_ref) = rest[:10]
        cast_scratch = rest[10:]
    ts = x_ref.shape[0]
    t = pl.program_id(0)
    n_cast_chunks = pl.num_programs(0) - 1

    if cast_moe_layer is not None:
        in_bufs, out_bufs, in_sems, out_sems = cast_scratch[0:3], cast_scratch[3:6], cast_scratch[6], cast_scratch[7]

        def cast_copies(q):
            return _expert_cast_copies(cast_moe_layer, q, ew_f32, ew_bf16, in_bufs, out_bufs, in_sems, out_sems)

        @pl.when(t == 0)
        def _():
            for load in cast_copies(0)[0]:
                load.start()

    @pl.when(t == 0)
    def _():
        w_in_ref[...] = w_in_f32[...].astype(BF16)
        pool_w_ref[...] = pool_w_f32[...].astype(BF16)
        conv_pw_ref[...] = conv_pw_f32[...].astype(BF16)
        w_out_ref[...] = w_out_f32[...].astype(BF16)
        p_scr[...] = jnp.zeros_like(p_scr)
        v_scr[...] = jnp.zeros_like(v_scr)
        ypool_scr[...] = jnp.zeros_like(ypool_scr)
        xprev_scr[...] = jnp.zeros_like(xprev_scr)

    x = x_ref[...]
    xb = x.astype(BF16)
    n_conv_blocks = ts // RB_CONV
    in_proj_chunks = [(c0, min(c0 + IN_PROJ_CHUNK, D_IN_PROJ)) for c0 in range(0, D_IN_PROJ, IN_PROJ_CHUNK)]

    blocks_per_group = n_conv_blocks // MIX_ROW_GROUPS
    for rb in range(n_conv_blocks):
        r0 = rb * RB_CONV
        cols = []
        for lc in range(D_CONV // LANES):
            lanes = slice(lc * LANES, (lc + 1) * LANES)
            acc = jnp.zeros((RB_CONV, LANES), F32)
            for k in range(CONV_WIDTH):
                first = r0 + HALO - (CONV_WIDTH - 1 - k)
                acc = acc + v_scr[lc, first:first + RB_CONV, :] * conv_w_ref[k:k + 1, lanes]
            cols.append(acc)
        y = jnp.concatenate(cols, axis=-1) + conv_b_ref[...]
        z = _layer_norm(y, cln_g_ref[...], cln_b_ref[...])
        y_scr[r0:r0 + RB_CONV, :] = (z * _sigmoid(z)).astype(BF16)
        if rb < len(in_proj_chunks):
            c0, c1 = in_proj_chunks[rb]
            u_scr[:, c0:c1] = _dot(xb, w_in_ref[:, c0:c1])
        if (rb + 1) % blocks_per_group == 0:
            rows = slice((rb + 1 - blocks_per_group) * RB_CONV, (rb + 1) * RB_CONV)
            y_conv = _dot(y_scr[rows, :], conv_pw_ref[...])
            heads = jnp.concatenate([ypool_scr[rows, :], y_conv.astype(BF16)], axis=-1)
            mix = _dot(heads, w_out_ref[...])
            o_ref[rows, :] = _layer_norm(alpha * xprev_scr[rows, :] + mix, ln_g_ref[...], ln_b_ref[...])
    for c0, c1 in in_proj_chunks[n_conv_blocks:]:
        u_scr[:, c0:c1] = _dot(xb, w_in_ref[:, c0:c1])

    tile_in_seq = t % tiles_per_seq
    starts_seq = tile_in_seq == 0
    p_scr[:, 0:HALO, :] = jnp.where(starts_seq, 0.0, p_scr[:, ts:ts + HALO, :])
    v_scr[:, 0:HALO, :] = jnp.where(starts_seq, 0.0, v_scr[:, ts:ts + HALO, :])
    for c in range(D_CONV // LANES):
        val = u_scr[:, D_POOL + c * LANES:D_POOL + (c + 1) * LANES]
        gate = u_scr[:, D_POOL + D_CONV + c * LANES:D_POOL + D_CONV + (c + 1) * LANES]
        v_scr[c, HALO:, :] = val * _sigmoid(gate)

    pos = tile_in_seq * ts + lax.broadcasted_iota(jnp.int32, (ts, POOL_GROUP_DIM), 0)
    for g, w in enumerate(POOL_WINDOWS):
        lanes = slice(g * POOL_GROUP_DIM, (g + 1) * POOL_GROUP_DIM)
        cur = u_scr[:, lanes]
        p_scr[g, HALO:, :] = cur
        win = cur
        for back in range(1, w):
            win = win + p_scr[g, HALO - back:HALO - back + ts, :]
        cnt = jnp.minimum(pos + 1, w).astype(F32)
        d = win / cnt - cur
        yg = _dot(d.astype(BF16), pool_w_ref[g]) * pool_scale_ref[:, lanes]
        ypool_scr[:, lanes] = yg.astype(BF16)

    xprev_scr[...] = x

    if cast_moe_layer is not None:
        @pl.when(t < n_cast_chunks)
        def _():
            loads, stores = cast_copies(t)
            for load in loads:
                load.wait()

            @pl.when(t > 0)
            def _():
                for store in cast_copies(t - 1)[1]:
                    store.wait()

            for src, dst in zip(in_bufs, out_bufs):
                dst[...] = src[...].astype(BF16)
            for store in stores:
                store.start()

            @pl.when(t + 1 < n_cast_chunks)
            def _():
                for load in cast_copies(t + 1)[0]:
                    load.start()

        @pl.when(t == n_cast_chunks)
        def _():
            for store in cast_copies(t - 1)[1]:
                store.wait()


def _const_spec(shape):
    nd = len(shape)
    return pl.BlockSpec(shape, lambda *_: (0,) * nd, pipeline_mode=pl.Buffered(1))


def _per_layer(v):
    return v.reshape(v.shape[0], 1, v.shape[1]) if v.ndim == 2 else v


def _layer_spec(v, layer, **kwargs):
    nd = v.ndim
    return pl.BlockSpec((None,) + v.shape[1:], lambda *_: (layer,) + (0,) * (nd - 1),
                        pipeline_mode=pl.Buffered(1), **kwargs)


def _mixer(alpha, layer, x, w_in, pool_w, pool_scale, conv_w, conv_b, cln_g, cln_b, conv_pw, w_out, ln_g, ln_b,
           expert_weights=None):
    B, S, D = x.shape
    ts = TS_MIX
    n_tiles = B * S // ts
    consts = [_per_layer(c) for c in (w_in, pool_w, pool_scale, conv_w, conv_b, cln_g, cln_b, conv_pw, w_out,
                                      ln_g, ln_b)]
    x_spec = pl.BlockSpec((ts, D), lambda t: (jnp.minimum(t, n_tiles - 1), 0))
    o_spec = pl.BlockSpec((ts, D), lambda t: (jnp.maximum(t - 1, 0), 0))
    o_shape = jax.ShapeDtypeStruct((B * S, D), F32)
    scratch = [pltpu.VMEM((D_POOL // LANES, HALO + ts, LANES), F32),
               pltpu.VMEM((D_CONV // LANES, HALO + ts, LANES), F32),
               pltpu.VMEM((ts, D_IN_PROJ), F32),
               pltpu.VMEM((ts, D_CONV), BF16),
               pltpu.VMEM((ts, D_POOL), BF16),
               pltpu.VMEM((ts, D), F32),
               pltpu.VMEM(w_in.shape[1:], BF16), pltpu.VMEM(pool_w.shape[1:], BF16),
               pltpu.VMEM(conv_pw.shape[1:], BF16), pltpu.VMEM(w_out.shape[1:], BF16)]
    params = pltpu.CompilerParams(dimension_semantics=("arbitrary",), vmem_limit_bytes=VMEM_LIMIT_BYTES)
    in_specs = [x_spec] + [_layer_spec(c, layer) for c in consts]
    if expert_weights is None:
        out = pl.pallas_call(
            functools.partial(_mixer_kernel, alpha, S // ts, None),
            grid=(n_tiles + 1,), in_specs=in_specs, out_specs=o_spec, out_shape=o_shape,
            scratch_shapes=scratch, compiler_params=params, name="mixer",
        )(x.reshape(B * S, D), *consts)
        return out.reshape(B, S, D), None

    moe_layer, mats = expert_weights
    hbm = pl.BlockSpec(memory_space=pl.ANY)
    n_experts = mats[0].shape[1]
    chunk_rows = [m.shape[2] * n_experts // n_tiles for m in mats]
    for m, rows in zip(mats, chunk_rows):
        assert rows * n_tiles == m.shape[2] * n_experts and m.shape[2] % rows == 0 and rows % (2 * SUBLANES) == 0
    cast_scratch = ([pltpu.VMEM((rows, m.shape[3]), F32) for m, rows in zip(mats, chunk_rows)]
                    + [pltpu.VMEM((rows, m.shape[3]), BF16) for m, rows in zip(mats, chunk_rows)]
                    + [pltpu.SemaphoreType.DMA((len(mats),)), pltpu.SemaphoreType.DMA((len(mats),))])
    out, *mats_bf16 = pl.pallas_call(
        functools.partial(_mixer_kernel, alpha, S // ts, moe_layer),
        grid=(n_tiles + 1,), in_specs=in_specs + [hbm] * len(mats),
        out_specs=[o_spec] + [hbm] * len(mats),
        out_shape=[o_shape] + [jax.ShapeDtypeStruct(m.shape[1:], BF16) for m in mats],
        scratch_shapes=scratch + cast_scratch, compiler_params=params, name="mixer_cast",
    )(x.reshape(B * S, D), *consts, *mats)
    return out.reshape(B, S, D), mats_bf16


def _norm_and_embed(alpha, x, f, p, ln_g, ln_b, gate_w, ple_w):
    h = _layer_norm(alpha * x + f, ln_g, ln_b)
    gate = _sigmoid(_dot(h.astype(BF16), gate_w))
    return h + gate * _dot(p.astype(BF16), ple_w)


def _load_bf16_weights(jobs, stage_ref, sems):
    rows = stage_ref.shape[1]
    chunks = [(src, dst, r0) for src, dst in jobs for r0 in range(0, dst.shape[0], rows)]

    def copy(n):
        src, dst, r0 = chunks[n]
        return pltpu.make_async_copy(src.at[pl.ds(r0, rows), :], stage_ref.at[n % 2], sems.at[n % 2])

    copy(0).start()
    for n, (src, dst, r0) in enumerate(chunks):
        if n + 1 < len(chunks):
            copy(n + 1).start()
        copy(n).wait()
        dst[r0:r0 + rows, :] = stage_ref[n % 2].astype(BF16)


def _dense_ffn_kernel(alpha, layer, x_ref, p_ref, w1_hbm, w3_hbm, w2_hbm, ln_g_ref, ln_b_ref, gate_w_hbm,
                      ple_w_hbm, o_ref, acc_ref, w1_ref, w3_ref, w2_ref, gate_w_ref, ple_w_ref,
                      wide_stage, narrow_stage, sems):
    @pl.when(pl.program_id(0) == 0)
    def _():
        j = layer // 2
        _load_bf16_weights([(w1_hbm.at[j], w1_ref), (w3_hbm.at[j], w3_ref)], wide_stage, sems)
        _load_bf16_weights([(w2_hbm.at[j], w2_ref), (gate_w_hbm.at[layer], gate_w_ref),
                            (ple_w_hbm.at[layer], ple_w_ref)], narrow_stage, sems)

    x = x_ref[...]
    xb = x.astype(BF16)
    ff = w1_ref.shape[1]
    for c0 in range(0, ff, FF_CHUNK_DENSE):
        c1 = min(c0 + FF_CHUNK_DENSE, ff)
        a = _dot(xb, w1_ref[:, c0:c1])
        b = _dot(xb, w3_ref[:, c0:c1])
        h = (a * _sigmoid(a) * b).astype(BF16)
        part = _dot(h, w2_ref[c0:c1, :])
        if c0 == 0:
            acc_ref[...] = part
        else:
            acc_ref[...] += part
    o_ref[...] = _norm_and_embed(alpha, x, acc_ref[...], p_ref[...], ln_g_ref[...], ln_b_ref[...],
                                 gate_w_ref[...], ple_w_ref[...])


def _dense_layer(alpha, layer, x, p, w1, w3, w2, ln_g, ln_b, gate_w, ple_w):
    T, D = x.shape
    tm = TM_DENSE
    FF = w1.shape[2]
    ln_g, ln_b = _per_layer(ln_g), _per_layer(ln_b)
    hbm = pl.BlockSpec(memory_space=pl.ANY)
    return pl.pallas_call(
        functools.partial(_dense_ffn_kernel, alpha, layer),
        grid=(T // tm,),
        in_specs=[pl.BlockSpec((tm, D), lambda i: (i, 0)),
                  pl.BlockSpec((None, tm, D_PLE), lambda i: (layer, i, 0)),
                  hbm, hbm, hbm, _layer_spec(ln_g, layer), _layer_spec(ln_b, layer), hbm, hbm],
        out_specs=pl.BlockSpec((tm, D), lambda i: (i, 0)),
        out_shape=jax.ShapeDtypeStruct((T, D), F32),
        scratch_shapes=[pltpu.VMEM((tm, D), F32),
                        pltpu.VMEM((D, FF), BF16), pltpu.VMEM((D, FF), BF16), pltpu.VMEM((FF, D), BF16),
                        pltpu.VMEM((D, D), BF16), pltpu.VMEM((D_PLE, D), BF16),
                        pltpu.VMEM((2, WIDE_STAGE_ROWS, FF), F32), pltpu.VMEM((2, NARROW_STAGE_ROWS, D), F32),
                        pltpu.SemaphoreType.DMA((2,))],
        compiler_params=pltpu.CompilerParams(
            dimension_semantics=("arbitrary",), vmem_limit_bytes=VMEM_LIMIT_BYTES),
        name="dense_ffn",
    )(x, p, w1, w3, w2, ln_g, ln_b, gate_w, ple_w)


def _router_kernel(x_ref, wt_ref, idx_ref, gate_ref, cnt_ref, carry_ref):
    i = pl.program_id(0)
    tr = x_ref.shape[0]

    @pl.when(i == 0)
    def _():
        carry_ref[...] = jnp.zeros_like(carry_ref)

    x = x_ref[...]
    wt = wt_ref[...]
    xh = x.astype(BF16)
    xl = (x - xh.astype(F32)).astype(BF16)
    wh = wt.astype(BF16)
    wl = (wt - wh.astype(F32)).astype(BF16)
    nt_dims = (((1,), (1,)), ((), ()))
    dg = lambda a, b: lax.dot_general(a, b, nt_dims, preferred_element_type=F32)
    logits = dg(wh, xh) + (dg(wh, xl) + dg(wl, xh))

    eid = lax.broadcasted_iota(jnp.int32, logits.shape, 0)
    m1 = jnp.max(logits, axis=0, keepdims=True)
    i1 = jnp.min(jnp.where(logits == m1, eid, N_EXPERTS), axis=0, keepdims=True)
    rest = jnp.where(eid == i1, -jnp.inf, logits)
    m2 = jnp.max(rest, axis=0, keepdims=True)
    i2 = jnp.min(jnp.where(rest == m2, eid, N_EXPERTS), axis=0, keepdims=True)
    e2 = jnp.exp(m2 - m1)
    g1 = 1.0 / (1.0 + e2)
    g2 = e2 / (1.0 + e2)

    oh1 = (eid == i1).astype(F32)
    oh2 = (eid == i2).astype(F32)
    chosen = oh1 + oh2
    r_i = lax.broadcasted_iota(jnp.int32, (tr, tr), 0)
    c_i = lax.broadcasted_iota(jnp.int32, (tr, tr), 1)
    before = (r_i < c_i).astype(BF16)
    excl = _dot(chosen.astype(BF16), before) + carry_ref[:, 0:1]
    rank1 = jnp.sum(oh1 * excl, axis=0, keepdims=True)
    rank2 = jnp.sum(oh2 * excl, axis=0, keepdims=True)

    total = carry_ref[:, 0:1] + jnp.sum(chosen, axis=1, keepdims=True)
    carry_ref[...] = jnp.broadcast_to(total, carry_ref.shape)
    cnt_ref[...] = jnp.broadcast_to(total, cnt_ref.shape).astype(jnp.int32)

    zi = jnp.zeros((SUBLANES - 4, tr), jnp.int32)
    idx_ref[...] = jnp.concatenate(
        [i1, i2, rank1.astype(jnp.int32), rank2.astype(jnp.int32), zi], axis=0)
    gate_ref[...] = jnp.concatenate([g1, g2, jnp.zeros((SUBLANES - 2, tr), F32)], axis=0)


def _router(x, router_w):
    T, D = x.shape
    tr = TR_ROUTE
    return pl.pallas_call(
        _router_kernel,
        grid=(T // tr,),
        in_specs=[pl.BlockSpec((tr, D), lambda i: (i, 0)), _const_spec((N_EXPERTS, D))],
        out_specs=[pl.BlockSpec((SUBLANES, tr), lambda i: (0, i)),
                   pl.BlockSpec((SUBLANES, tr), lambda i: (0, i)),
                   pl.BlockSpec((N_EXPERTS, LANES), lambda i: (0, 0))],
        out_shape=[jax.ShapeDtypeStruct((SUBLANES, T), jnp.int32),
                   jax.ShapeDtypeStruct((SUBLANES, T), F32),
                   jax.ShapeDtypeStruct((N_EXPERTS, LANES), jnp.int32)],
        scratch_shapes=[pltpu.VMEM((N_EXPERTS, LANES), F32)],
        compiler_params=pltpu.CompilerParams(
            dimension_semantics=("arbitrary",), vmem_limit_bytes=VMEM_LIMIT_BYTES),
        name="router",
    )(x, router_w.T)


def _store_token_tiles(dst_ref, value):
    rows = value.shape[0]
    for c in range(ROW_CHUNKS):
        dst_ref[pl.ds(c, rows, stride=ROW_CHUNKS), :] = value[:, c * LANES:(c + 1) * LANES]


def _load_token_tile_chunk(src_ref, rows, c):
    return src_ref[pl.ds(c, rows, stride=ROW_CHUNKS), :]


def _load_token_tiles(src_ref, rows):
    return jnp.concatenate([_load_token_tile_chunk(src_ref, rows, c) for c in range(ROW_CHUNKS)], axis=-1)


def _tile_rows(row):
    return pl.ds(pl.multiple_of(row * ROW_CHUNKS, ROW_CHUNKS), ROW_CHUNKS)


def _dispatch_kernel(dest_ref, fill_ref, x_ref, xs_hbm, xt_ref, zero_ref, sems, fill_sem):
    i = pl.program_id(0)
    n = pl.num_programs(0)
    tc = x_ref.shape[0]
    n_tok = n * tc
    base = i * tc
    slot = i % 2
    stage = xt_ref.at[slot]

    @pl.when(i == 0)
    def _():
        zero_ref[...] = jnp.zeros_like(zero_ref)
        tile_rows = zero_ref.shape[0]

        def fill(j):
            start = pl.multiple_of(fill_ref[j] * tile_rows, tile_rows)
            return pltpu.make_async_copy(zero_ref, xs_hbm.at[pl.ds(start, tile_rows), :], fill_sem)

        for j in range(fill_ref.shape[0]):
            @pl.when(fill_ref[j] >= 0)
            def _():
                fill(j).start()

        for j in range(fill_ref.shape[0]):
            @pl.when(fill_ref[j] >= 0)
            def _():
                fill(j).wait()

    _store_token_tiles(stage, x_ref[...])

    def copy(k, r):
        d = dest_ref[k * n_tok + base + r]
        return pltpu.make_async_copy(stage.at[_tile_rows(r), :], xs_hbm.at[_tile_rows(d), :], sems.at[slot])

    def issue(rb, c):
        for j in range(ISSUE_UNROLL):
            r = rb * ISSUE_UNROLL + j
            for k in range(TOP_K):
                copy(k, r).start(priority=k)
        return c

    lax.fori_loop(0, tc // ISSUE_UNROLL, issue, 0)

    def wait_step(s):
        for _ in range(TOP_K):
            pltpu.make_async_copy(xt_ref.at[s], xs_hbm.at[pl.ds(0, tc * ROW_CHUNKS), :], sems.at[s]).wait()

    @pl.when(i > 0)
    def _():
        wait_step(1 - slot)

    @pl.when(i == n - 1)
    def _():
        wait_step(slot)


def _dispatch(x, dest, fill_tiles, n_rows):
    T, D = x.shape
    tc = TC_DISPATCH
    return pl.pallas_call(
        _dispatch_kernel,
        grid_spec=pltpu.PrefetchScalarGridSpec(
            num_scalar_prefetch=2, grid=(T // tc,),
            in_specs=[pl.BlockSpec((tc, D), lambda i, d, ft: (i, 0))],
            out_specs=pl.BlockSpec(memory_space=pl.ANY),
            scratch_shapes=[pltpu.VMEM((2, tc * ROW_CHUNKS, LANES), F32),
                            pltpu.VMEM((TM_GROUP * ROW_CHUNKS, LANES), F32),
                            pltpu.SemaphoreType.DMA((2,)), pltpu.SemaphoreType.DMA(())]),
        out_shape=jax.ShapeDtypeStruct((n_rows * ROW_CHUNKS, LANES), F32),
        compiler_params=pltpu.CompilerParams(dimension_semantics=("arbitrary",)),
        name="dispatch",
    )(dest, fill_tiles, x)


def _expert_ffn_kernel(te_ref, nt_ref, xs_ref, w1_ref, w3_ref, w2_ref, o_ref, xb_ref, acc_ref):
    i = pl.program_id(0)
    f = pl.program_id(1)
    nf = pl.num_programs(1)
    tm = acc_ref.shape[0]

    @pl.when(i < nt_ref[0])
    def _():
        @pl.when(f == 0)
        def _():
            for c in range(ROW_CHUNKS):
                xb_ref[:, c * LANES:(c + 1) * LANES] = _load_token_tile_chunk(xs_ref, tm, c).astype(BF16)

        xb = xb_ref[...]
        a = _dot(xb, w1_ref[...])
        b = _dot(xb, w3_ref[...])
        h = (a * _sigmoid(a) * b).astype(BF16)
        part = _dot(h, w2_ref[...])

        @pl.when(f == 0)
        def _():
            acc_ref[...] = part

        @pl.when((f > 0) & (f < nf - 1))
        def _():
            acc_ref[...] += part

        @pl.when(f == nf - 1)
        def _():
            _store_token_tiles(o_ref, acc_ref[...] + part)

    @pl.when((i >= nt_ref[0]) & (f == 0))
    def _():
        o_ref[...] = jnp.zeros_like(o_ref)


def _expert_ffn(xs, te, nt, w1, w3, w2):
    E, D, FF = w1.shape
    tm, cf = TM_GROUP, FF_CHUNK_MOE
    nf = FF // cf
    max_tiles = xs.shape[0] // (tm * ROW_CHUNKS)

    def tile(i, nt_ref):
        return jnp.minimum(i, nt_ref[0] - 1)

    def chunk(i, f, nt_ref):
        return jnp.where(i < nt_ref[0], f, nf - 1)

    return pl.pallas_call(
        _expert_ffn_kernel,
        grid_spec=pltpu.PrefetchScalarGridSpec(
            num_scalar_prefetch=2, grid=(max_tiles, nf),
            in_specs=[pl.BlockSpec((tm * ROW_CHUNKS, LANES), lambda i, f, te, nt: (tile(i, nt), 0)),
                      pl.BlockSpec((None, D, cf), lambda i, f, te, nt: (te[tile(i, nt)], 0, chunk(i, f, nt))),
                      pl.BlockSpec((None, D, cf), lambda i, f, te, nt: (te[tile(i, nt)], 0, chunk(i, f, nt))),
                      pl.BlockSpec((None, cf, D), lambda i, f, te, nt: (te[tile(i, nt)], chunk(i, f, nt), 0))],
            out_specs=pl.BlockSpec((tm * ROW_CHUNKS, LANES), lambda i, f, te, nt: (i, 0)),
            scratch_shapes=[pltpu.VMEM((tm, D), BF16), pltpu.VMEM((tm, D), F32)]),
        out_shape=jax.ShapeDtypeStruct(xs.shape, F32),
        compiler_params=pltpu.CompilerParams(
            dimension_semantics=("arbitrary", "arbitrary"), vmem_limit_bytes=VMEM_LIMIT_BYTES),
        name="expert_ffn",
    )(te, nt, xs, w1, w3, w2)


def _sorted_layout(cnt, n_assign):
    tm = TM_GROUP
    max_tiles = n_assign // tm + N_EXPERTS
    tiles_e = (cnt + tm - 1) // tm
    tile_end = jnp.cumsum(tiles_e)
    row_start = (tile_end - tiles_e) * tm
    nt = tile_end[-1]
    tile_ids = jnp.minimum(jnp.arange(max_tiles, dtype=jnp.int32), nt - 1)
    te = jnp.sum((tile_end[None, :] <= tile_ids[:, None]).astype(jnp.int32), axis=1)
    te = jnp.minimum(te, N_EXPERTS - 1)
    last_tile = jnp.where(tiles_e > 0, tile_end - 1, -1)
    tail = nt + jnp.arange(N_EXPERTS, dtype=jnp.int32)
    tail = jnp.where(tail < max_tiles, tail, -1)
    fill_tiles = jnp.concatenate([last_tile, tail]).astype(jnp.int32)
    return row_start, te, nt.reshape(1).astype(jnp.int32), fill_tiles, max_tiles * tm


def _combine_kernel(alpha, dest_ref, x_ref, p_ref, gate_ref, ys_hbm, ln_g_ref, ln_b_ref, gate_w_f32, ple_w_f32,
                    o_ref, yb_ref, gate_w_ref, ple_w_ref, sems):
    i = pl.program_id(0)
    n = pl.num_programs(0)
    tc = x_ref.shape[0]
    n_tok = n * tc
    slot = i % 2

    @pl.when(i == 0)
    def _():
        gate_w_ref[...] = gate_w_f32[...].astype(BF16)
        ple_w_ref[...] = ple_w_f32[...].astype(BF16)

    def issue_step(step, s):
        base = step * tc

        def issue(rb, c):
            for j in range(ISSUE_UNROLL):
                r = rb * ISSUE_UNROLL + j
                for k in range(TOP_K):
                    d = dest_ref[k * n_tok + base + r]
                    pltpu.make_async_copy(ys_hbm.at[_tile_rows(d), :], yb_ref.at[s, k, _tile_rows(r), :],
                                          sems.at[s]).start()
            return c

        lax.fori_loop(0, tc // ISSUE_UNROLL, issue, 0)

    @pl.when(i == 0)
    def _():
        issue_step(0, 0)

    @pl.when(i + 1 < n)
    def _():
        issue_step(i + 1, 1 - slot)

    for k in range(TOP_K):
        pltpu.make_async_copy(ys_hbm.at[pl.ds(0, tc * ROW_CHUNKS), :], yb_ref.at[slot, k], sems.at[slot]).wait()

    g = gate_ref[...]
    f = (g[:, 0:1] * _load_token_tiles(yb_ref.at[slot, 0], tc)
         + g[:, 1:2] * _load_token_tiles(yb_ref.at[slot, 1], tc))
    o_ref[...] = _norm_and_embed(alpha, x_ref[...], f, p_ref[...], ln_g_ref[...], ln_b_ref[...],
                                 gate_w_ref[...], ple_w_ref[...])


def _combine_layer(alpha, layer, x, p, gates, dest, ys, ln_g, ln_b, gate_w, ple_w):
    T, D = x.shape
    tc = TC_COMBINE
    consts = [_per_layer(c) for c in (ln_g, ln_b, gate_w, ple_w)]
    return pl.pallas_call(
        functools.partial(_combine_kernel, alpha),
        grid_spec=pltpu.PrefetchScalarGridSpec(
            num_scalar_prefetch=1, grid=(T // tc,),
            in_specs=[pl.BlockSpec((tc, D), lambda i, d: (i, 0)),
                      pl.BlockSpec((None, tc, D_PLE), lambda i, d: (layer, i, 0)),
                      pl.BlockSpec((tc, TOP_K), lambda i, d: (i, 0)),
                      pl.BlockSpec(memory_space=pl.ANY)]
                     + [_layer_spec(c, layer) for c in consts],
            out_specs=pl.BlockSpec((tc, D), lambda i, d: (i, 0)),
            scratch_shapes=[pltpu.VMEM((2, TOP_K, tc * ROW_CHUNKS, LANES), F32),
                            pltpu.VMEM((D, D), BF16), pltpu.VMEM((D_PLE, D), BF16),
                            pltpu.SemaphoreType.DMA((2,))]),
        out_shape=jax.ShapeDtypeStruct((T, D), F32),
        compiler_params=pltpu.CompilerParams(
            dimension_semantics=("arbitrary",), vmem_limit_bytes=VMEM_LIMIT_BYTES),
        name="combine",
    )(dest, x, p, gates, ys, *consts)


def _moe_layer(alpha, layer, x, p, router_w, w1, w3, w2, ln_g, ln_b, gate_w, ple_w):
    T, D = x.shape
    route_i, route_g, counts = _router(x, router_w)
    idx = route_i[0:TOP_K]
    rank = route_i[TOP_K:2 * TOP_K]
    gates = route_g[0:TOP_K].T

    start, te, nt, fill_tiles, n_rows = _sorted_layout(counts[:, 0], TOP_K * T)
    start_of = sum(jnp.where(idx == e, start[e], 0) for e in range(N_EXPERTS))
    dest = (start_of + rank).astype(jnp.int32).reshape(-1)

    xs = _dispatch(x, dest, fill_tiles, n_rows)
    ys = _expert_ffn(xs, te, nt, w1, w3, w2)
    return _combine_layer(alpha, layer, x, p, gates, dest, ys, ln_g, ln_b, gate_w, ple_w)


def kernel(x, p, w_in, pool_w, pool_scale, conv_w, conv_b, conv_ln_g, conv_ln_b, conv_pw, w_out, ln1_g, ln1_b,
           dense_w1, dense_w3, dense_w2, router_w, exp_w1, exp_w3, exp_w2, ln2_g, ln2_b, ple_gate_w, ple_w):
    depth = w_in.shape[0]
    alpha = (2.0 * depth) ** 0.25
    B, S, D = x.shape
    pt = p.reshape(depth, B * S, D_PLE)
    for i in range(depth):
        j = i // 2
        is_dense = i % 2 == 0
        x, experts_bf16 = _mixer(alpha, i, x, w_in, pool_w, pool_scale, conv_w, conv_b, conv_ln_g, conv_ln_b,
                                 conv_pw, w_out, ln1_g, ln1_b,
                                 expert_weights=None if is_dense else (j, (exp_w1, exp_w3, exp_w2)))
        xt = x.reshape(B * S, D)
        if is_dense:
            xt = _dense_layer(alpha, i, xt, pt, dense_w1, dense_w3, dense_w2, ln2_g, ln2_b, ple_gate_w, ple_w)
        else:
            xt = _moe_layer(alpha, i, xt, pt, router_w[j], *experts_bf16, ln2_g, ln2_b, ple_gate_w, ple_w)
        x = xt.reshape(B, S, D)
    return x
```

```python
import functools

import jax
import jax.numpy as jnp
from jax import lax
from jax.experimental import pallas as pl
from jax.experimental.pallas import tpu as pltpu

D_MODEL = 1024
D_PLE = 256
D_POOL = 512
D_CONV = D_MODEL - D_POOL
POOL_WINDOWS = (2, 4, 8, 16)
POOL_GROUP_DIM = D_POOL // len(POOL_WINDOWS)
CONV_WIDTH = 31
D_IN_PROJ = D_POOL + 2 * D_CONV
N_EXPERTS = 8
TOP_K = 2
LN_EPS = 1e-5

F32 = jnp.float32
BF16 = jnp.bfloat16

SUBLANES = 8
LANES = 128
VMEM_LIMIT_BYTES = 56 * 1024 * 1024

HALO = 32
TS_MIX = 512
RB_CONV = 128
IN_PROJ_CHUNK = 256
MIX_ROW_GROUPS = 2
TM_DENSE = 512
FF_CHUNK_DENSE = 1024
TR_ROUTE = 512
TM_GROUP = 512
FF_CHUNK_MOE = 1792
TC_DISPATCH = 512
TC_COMBINE = 512
ROW_CHUNKS = D_MODEL // LANES
ISSUE_UNROLL = 8


def _sigmoid(z):
    return 1.0 / (1.0 + jnp.exp(-z))


def _layer_norm(h, g, b):
    mu = jnp.mean(h, axis=-1, keepdims=True)
    c = h - mu
    var = jnp.mean(c * c, axis=-1, keepdims=True)
    return c * lax.rsqrt(var + LN_EPS) * g + b


def _dot(a, b):
    return jnp.dot(a, b, preferred_element_type=F32)


def _side_cast_copies(layers, q, src_hbm, dst_hbm, in_bufs, out_bufs, in_sems, out_sems):
    loads, stores = [], []
    for a in range(len(src_hbm)):
        rows = in_bufs[a].shape[0]
        per_expert = src_hbm[a].shape[2] // rows
        e = q // per_expert
        r0 = pl.multiple_of((q % per_expert) * rows, rows)
        loads.append(pltpu.make_async_copy(src_hbm[a].at[layers[a], e, pl.ds(r0, rows), :], in_bufs[a],
                                           in_sems.at[a]))
        stores.append(pltpu.make_async_copy(out_bufs[a], dst_hbm[a].at[e, pl.ds(r0, rows), :], out_sems.at[a]))
    return loads, stores


def _mixer_kernel(alpha, tiles_per_seq, cast_layers, n_cast_chunks, x_ref, w_in_f32, pool_w_f32, pool_scale_ref,
                  conv_w_ref, conv_b_ref, cln_g_ref, cln_b_ref, conv_pw_f32, w_out_f32, ln_g_ref, ln_b_ref, *rest):
    n_side = len(cast_layers)
    side_f32, rest = rest[:n_side], rest[n_side:]
    o_ref, side_bf16, rest = rest[0], rest[1:1 + n_side], rest[1 + n_side:]
    (p_scr, v_scr, u_scr, y_scr, ypool_scr, xprev_scr, w_in_ref, pool_w_ref, conv_pw_ref, w_out_ref) = rest[:10]
    cast_scratch = rest[10:]
    ts = x_ref.shape[0]
    t = pl.program_id(0)

    if n_side:
        in_bufs, out_bufs = cast_scratch[:n_side], cast_scratch[n_side:2 * n_side]
        in_sems, out_sems = cast_scratch[2 * n_side:]

        def cast_copies(q):
            return _side_cast_copies(cast_layers, q, side_f32, side_bf16, in_bufs, out_bufs, in_sems, out_sems)

        @pl.when(t == 0)
        def _():
            for load in cast_copies(0)[0]:
                load.start()

    @pl.when(t == 0)
    def _():
        w_in_ref[...] = w_in_f32[...].astype(BF16)
        pool_w_ref[...] = pool_w_f32[...].astype(BF16)
        conv_pw_ref[...] = conv_pw_f32[...].astype(BF16)
        w_out_ref[...] = w_out_f32[...].astype(BF16)
        p_scr[...] = jnp.zeros_like(p_scr)
        v_scr[...] = jnp.zeros_like(v_scr)
        ypool_scr[...] = jnp.zeros_like(ypool_scr)
        xprev_scr[...] = jnp.zeros_like(xprev_scr)

    x = x_ref[...]
    xb = x.astype(BF16)
    n_conv_blocks = ts // RB_CONV
    in_proj_chunks = [(c0, min(c0 + IN_PROJ_CHUNK, D_IN_PROJ)) for c0 in range(0, D_IN_PROJ, IN_PROJ_CHUNK)]

    blocks_per_group = n_conv_blocks // MIX_ROW_GROUPS
    for rb in range(n_conv_blocks):
        r0 = rb * RB_CONV
        cols = []
        for lc in range(D_CONV // LANES):
            lanes = slice(lc * LANES, (lc + 1) * LANES)
            acc = jnp.zeros((RB_CONV, LANES), F32)
            for k in range(CONV_WIDTH):
                first = r0 + HALO - (CONV_WIDTH - 1 - k)
                acc = acc + v_scr[lc, first:first + RB_CONV, :] * conv_w_ref[k:k + 1, lanes]
            cols.append(acc)
        y = jnp.concatenate(cols, axis=-1) + conv_b_ref[...]
        z = _layer_norm(y, cln_g_ref[...], cln_b_ref[...])
        y_scr[r0:r0 + RB_CONV, :] = (z * _sigmoid(z)).astype(BF16)
        if rb < len(in_proj_chunks):
            c0, c1 = in_proj_chunks[rb]
            u_scr[:, c0:c1] = _dot(xb, w_in_ref[:, c0:c1])
        if (rb + 1) % blocks_per_group == 0:
            rows = slice((rb + 1 - blocks_per_group) * RB_CONV, (rb + 1) * RB_CONV)
            y_conv = _dot(y_scr[rows, :], conv_pw_ref[...])
            heads = jnp.concatenate([ypool_scr[rows, :], y_conv.astype(BF16)], axis=-1)
            mix = _dot(heads, w_out_ref[...])
            o_ref[rows, :] = _layer_norm(alpha * xprev_scr[rows, :] + mix, ln_g_ref[...], ln_b_ref[...])
    for c0, c1 in in_proj_chunks[n_conv_blocks:]:
        u_scr[:, c0:c1] = _dot(xb, w_in_ref[:, c0:c1])

    tile_in_seq = t % tiles_per_seq
    starts_seq = tile_in_seq == 0
    p_scr[:, 0:HALO, :] = jnp.where(starts_seq, 0.0, p_scr[:, ts:ts + HALO, :])
    v_scr[:, 0:HALO, :] = jnp.where(starts_seq, 0.0, v_scr[:, ts:ts + HALO, :])
    for c in range(D_CONV // LANES):
        val = u_scr[:, D_POOL + c * LANES:D_POOL + (c + 1) * LANES]
        gate = u_scr[:, D_POOL + D_CONV + c * LANES:D_POOL + D_CONV + (c + 1) * LANES]
        v_scr[c, HALO:, :] = val * _sigmoid(gate)

    pos = tile_in_seq * ts + lax.broadcasted_iota(jnp.int32, (ts, POOL_GROUP_DIM), 0)
    for g, w in enumerate(POOL_WINDOWS):
        lanes = slice(g * POOL_GROUP_DIM, (g + 1) * POOL_GROUP_DIM)
        cur = u_scr[:, lanes]
        p_scr[g, HALO:, :] = cur
        win = cur
        for back in range(1, w):
            win = win + p_scr[g, HALO - back:HALO - back + ts, :]
        cnt = jnp.minimum(pos + 1, w).astype(F32)
        d = win / cnt - cur
        yg = _dot(d.astype(BF16), pool_w_ref[g]) * pool_scale_ref[:, lanes]
        ypool_scr[:, lanes] = yg.astype(BF16)

    xprev_scr[...] = x

    if n_side:
        @pl.when(t < n_cast_chunks)
        def _():
            loads, stores = cast_copies(t)
            for load in loads:
                load.wait()

            @pl.when(t > 0)
            def _():
                for store in cast_copies(t - 1)[1]:
                    store.wait()

            for src, dst in zip(in_bufs, out_bufs):
                dst[...] = src[...].astype(BF16)
            for store in stores:
                store.start()

            @pl.when(t + 1 < n_cast_chunks)
            def _():
                for load in cast_copies(t + 1)[0]:
                    load.start()

        @pl.when(t == n_cast_chunks)
        def _():
            for store in cast_copies(t - 1)[1]:
                store.wait()


def _const_spec(shape):
    nd = len(shape)
    return pl.BlockSpec(shape, lambda *_: (0,) * nd, pipeline_mode=pl.Buffered(1))


def _per_layer(v):
    return v.reshape(v.shape[0], 1, v.shape[1]) if v.ndim == 2 else v


def _layer_spec(v, layer, **kwargs):
    nd = v.ndim
    return pl.BlockSpec((None,) + v.shape[1:], lambda *_: (layer,) + (0,) * (nd - 1),
                        pipeline_mode=pl.Buffered(1), **kwargs)


def _mixer(alpha, layer, x, w_in, pool_w, pool_scale, conv_w, conv_b, cln_g, cln_b, conv_pw, w_out, ln_g, ln_b,
           side_weights):
    B, S, D = x.shape
    ts = TS_MIX
    n_tiles = B * S // ts
    consts = [_per_layer(c) for c in (w_in, pool_w, pool_scale, conv_w, conv_b, cln_g, cln_b, conv_pw, w_out,
                                      ln_g, ln_b)]
    x_spec = pl.BlockSpec((ts, D), lambda t: (jnp.minimum(t, n_tiles - 1), 0))
    o_spec = pl.BlockSpec((ts, D), lambda t: (jnp.maximum(t - 1, 0), 0))
    o_shape = jax.ShapeDtypeStruct((B * S, D), F32)
    scratch = [pltpu.VMEM((D_POOL // LANES, HALO + ts, LANES), F32),
               pltpu.VMEM((D_CONV // LANES, HALO + ts, LANES), F32),
               pltpu.VMEM((ts, D_IN_PROJ), F32),
               pltpu.VMEM((ts, D_CONV), BF16),
               pltpu.VMEM((ts, D_POOL), BF16),
               pltpu.VMEM((ts, D), F32),
               pltpu.VMEM(w_in.shape[1:], BF16), pltpu.VMEM(pool_w.shape[1:], BF16),
               pltpu.VMEM(conv_pw.shape[1:], BF16), pltpu.VMEM(w_out.shape[1:], BF16)]
    params = pltpu.CompilerParams(dimension_semantics=("arbitrary",), vmem_limit_bytes=VMEM_LIMIT_BYTES)
    in_specs = [x_spec] + [_layer_spec(c, layer) for c in consts]

    mats = [m for m, _ in side_weights]
    cast_layers = tuple(l for _, l in side_weights)
    hbm = pl.BlockSpec(memory_space=pl.ANY)
    bf16_rows = 2 * SUBLANES
    n_chunks = max(n for n in range(1, n_tiles + 1)
                   if all((m.shape[1] * m.shape[2]) % n == 0
                          and m.shape[2] % ((m.shape[1] * m.shape[2]) // n) == 0
                          and ((m.shape[1] * m.shape[2]) // n) % bf16_rows == 0 for m in mats))
    chunk_rows = [m.shape[1] * m.shape[2] // n_chunks for m in mats]
    cast_scratch = ([pltpu.VMEM((rows, m.shape[3]), F32) for m, rows in zip(mats, chunk_rows)]
                    + [pltpu.VMEM((rows, m.shape[3]), BF16) for m, rows in zip(mats, chunk_rows)]
                    + [pltpu.SemaphoreType.DMA((len(mats),)), pltpu.SemaphoreType.DMA((len(mats),))])
    out, *mats_bf16 = pl.pallas_call(
        functools.partial(_mixer_kernel, alpha, S // ts, cast_layers, n_chunks),
        grid=(n_tiles + 1,), in_specs=in_specs + [hbm] * len(mats),
        out_specs=[o_spec] + [hbm] * len(mats),
        out_shape=[o_shape] + [jax.ShapeDtypeStruct(m.shape[1:], BF16) for m in mats],
        scratch_shapes=scratch + cast_scratch, compiler_params=params, name="mixer",
    )(x.reshape(B * S, D), *consts, *mats)
    return out.reshape(B, S, D), mats_bf16


def _norm_and_embed(alpha, x, f, p, ln_g, ln_b, gate_w, ple_w):
    h = _layer_norm(alpha * x + f, ln_g, ln_b)
    gate = _sigmoid(_dot(h.astype(BF16), gate_w))
    return h + gate * _dot(p.astype(BF16), ple_w)


def _dense_ffn_kernel(alpha, x_ref, p_ref, w1_ref, w3_ref, w2_ref, ln_g_ref, ln_b_ref, gate_w_ref, ple_w_ref,
                      o_ref, acc_ref):
    x = x_ref[...]
    xb = x.astype(BF16)
    ff = w1_ref.shape[1]
    for c0 in range(0, ff, FF_CHUNK_DENSE):
        c1 = min(c0 + FF_CHUNK_DENSE, ff)
        a = _dot(xb, w1_ref[:, c0:c1])
        b = _dot(xb, w3_ref[:, c0:c1])
        h = (a * _sigmoid(a) * b).astype(BF16)
        part = _dot(h, w2_ref[c0:c1, :])
        if c0 == 0:
            acc_ref[...] = part
        else:
            acc_ref[...] += part
    o_ref[...] = _norm_and_embed(alpha, x, acc_ref[...], p_ref[...], ln_g_ref[...], ln_b_ref[...],
                                 gate_w_ref[...], ple_w_ref[...])


def _dense_layer(alpha, layer, x, p, w1, w3, w2, ln_g, ln_b, gate_w, ple_w):
    T, D = x.shape
    tm = TM_DENSE
    ln_g, ln_b = _per_layer(ln_g), _per_layer(ln_b)
    return pl.pallas_call(
        functools.partial(_dense_ffn_kernel, alpha),
        grid=(T // tm,),
        in_specs=[pl.BlockSpec((tm, D), lambda i: (i, 0)),
                  pl.BlockSpec((None, tm, D_PLE), lambda i: (layer, i, 0)),
                  _layer_spec(w1, 0), _layer_spec(w3, 0), _layer_spec(w2, 0),
                  _layer_spec(ln_g, layer), _layer_spec(ln_b, layer),
                  _layer_spec(gate_w, 0), _layer_spec(ple_w, 0)],
        out_specs=pl.BlockSpec((tm, D), lambda i: (i, 0)),
        out_shape=jax.ShapeDtypeStruct((T, D), F32),
        scratch_shapes=[pltpu.VMEM((tm, D), F32)],
        compiler_params=pltpu.CompilerParams(
            dimension_semantics=("arbitrary",), vmem_limit_bytes=VMEM_LIMIT_BYTES),
        name="dense_ffn",
    )(x, p, w1, w3, w2, ln_g, ln_b, gate_w, ple_w)


def _router_kernel(x_ref, wt_ref, idx_ref, gate_ref, cnt_ref, carry_ref):
    i = pl.program_id(0)
    tr = x_ref.shape[0]

    @pl.when(i == 0)
    def _():
        carry_ref[...] = jnp.zeros_like(carry_ref)

    x = x_ref[...]
    wt = wt_ref[...]
    xh = x.astype(BF16)
    xl = (x - xh.astype(F32)).astype(BF16)
    wh = wt.astype(BF16)
    wl = (wt - wh.astype(F32)).astype(BF16)
    nt_dims = (((1,), (1,)), ((), ()))
    dg = lambda a, b: lax.dot_general(a, b, nt_dims, preferred_element_type=F32)
    logits = dg(wh, xh) + (dg(wh, xl) + dg(wl, xh))

    eid = lax.broadcasted_iota(jnp.int32, logits.shape, 0)
    m1 = jnp.max(logits, axis=0, keepdims=True)
    i1 = jnp.min(jnp.where(logits == m1, eid, N_EXPERTS), axis=0, keepdims=True)
    rest = jnp.where(eid == i1, -jnp.inf, logits)
    m2 = jnp.max(rest, axis=0, keepdims=True)
    i2 = jnp.min(jnp.where(rest == m2, eid, N_EXPERTS), axis=0, keepdims=True)
    e2 = jnp.exp(m2 - m1)
    g1 = 1.0 / (1.0 + e2)
    g2 = e2 / (1.0 + e2)

    oh1 = (eid == i1).astype(F32)
    oh2 = (eid == i2).astype(F32)
    chosen = oh1 + oh2
    r_i = lax.broadcasted_iota(jnp.int32, (tr, tr), 0)
    c_i = lax.broadcasted_iota(jnp.int32, (tr, tr), 1)
    before = (r_i < c_i).astype(BF16)
    excl = _dot(chosen.astype(BF16), before) + carry_ref[:, 0:1]
    rank1 = jnp.sum(oh1 * excl, axis=0, keepdims=True)
    rank2 = jnp.sum(oh2 * excl, axis=0, keepdims=True)

    total = carry_ref[:, 0:1] + jnp.sum(chosen, axis=1, keepdims=True)
    carry_ref[...] = jnp.broadcast_to(total, carry_ref.shape)
    cnt_ref[...] = jnp.broadcast_to(total, cnt_ref.shape).astype(jnp.int32)

    zi = jnp.zeros((SUBLANES - 4, tr), jnp.int32)
    idx_ref[...] = jnp.concatenate(
        [i1, i2, rank1.astype(jnp.int32), rank2.astype(jnp.int32), zi], axis=0)
    gate_ref[...] = jnp.concatenate([g1, g2, jnp.zeros((SUBLANES - 2, tr), F32)], axis=0)


def _router(x, router_w):
    T, D = x.shape
    tr = TR_ROUTE
    return pl.pallas_call(
        _router_kernel,
        grid=(T // tr,),
        in_specs=[pl.BlockSpec((tr, D), lambda i: (i, 0)), _const_spec((N_EXPERTS, D))],
        out_specs=[pl.BlockSpec((SUBLANES, tr), lambda i: (0, i)),
                   pl.BlockSpec((SUBLANES, tr), lambda i: (0, i)),
                   pl.BlockSpec((N_EXPERTS, LANES), lambda i: (0, 0))],
        out_shape=[jax.ShapeDtypeStruct((SUBLANES, T), jnp.int32),
                   jax.ShapeDtypeStruct((SUBLANES, T), F32),
                   jax.ShapeDtypeStruct((N_EXPERTS, LANES), jnp.int32)],
        scratch_shapes=[pltpu.VMEM((N_EXPERTS, LANES), F32)],
        compiler_params=pltpu.CompilerParams(
            dimension_semantics=("arbitrary",), vmem_limit_bytes=VMEM_LIMIT_BYTES),
        name="router",
    )(x, router_w.T)


def _store_token_tiles(dst_ref, value):
    rows = value.shape[0]
    for c in range(ROW_CHUNKS):
        dst_ref[pl.ds(c, rows, stride=ROW_CHUNKS), :] = value[:, c * LANES:(c + 1) * LANES]


def _load_token_tile_chunk(src_ref, rows, c):
    return src_ref[pl.ds(c, rows, stride=ROW_CHUNKS), :]


def _load_token_tiles(src_ref, rows):
    return jnp.concatenate([_load_token_tile_chunk(src_ref, rows, c) for c in range(ROW_CHUNKS)], axis=-1)


def _tile_rows(row):
    return pl.ds(pl.multiple_of(row * ROW_CHUNKS, ROW_CHUNKS), ROW_CHUNKS)


def _dispatch_kernel(dest_ref, fill_ref, x_ref, xs_hbm, xt_ref, zero_ref, sems, fill_sem):
    i = pl.program_id(0)
    n = pl.num_programs(0)
    tc = x_ref.shape[0]
    n_tok = n * tc
    base = i * tc
    slot = i % 2
    stage = xt_ref.at[slot]

    @pl.when(i == 0)
    def _():
        zero_ref[...] = jnp.zeros_like(zero_ref)
        tile_rows = zero_ref.shape[0]

        def fill(j):
            start = pl.multiple_of(fill_ref[j] * tile_rows, tile_rows)
            return pltpu.make_async_copy(zero_ref, xs_hbm.at[pl.ds(start, tile_rows), :], fill_sem)

        for j in range(fill_ref.shape[0]):
            @pl.when(fill_ref[j] >= 0)
            def _():
                fill(j).start()

        for j in range(fill_ref.shape[0]):
            @pl.when(fill_ref[j] >= 0)
            def _():
                fill(j).wait()

    _store_token_tiles(stage, x_ref[...])

    def copy(k, r):
        d = dest_ref[k * n_tok + base + r]
        return pltpu.make_async_copy(stage.at[_tile_rows(r), :], xs_hbm.at[_tile_rows(d), :], sems.at[slot])

    def issue(rb, c):
        for j in range(ISSUE_UNROLL):
            r = rb * ISSUE_UNROLL + j
            for k in range(TOP_K):
                copy(k, r).start(priority=k)
        return c

    lax.fori_loop(0, tc // ISSUE_UNROLL, issue, 0)

    def wait_step(s):
        for _ in range(TOP_K):
            pltpu.make_async_copy(xt_ref.at[s], xs_hbm.at[pl.ds(0, tc * ROW_CHUNKS), :], sems.at[s]).wait()

    @pl.when(i > 0)
    def _():
        wait_step(1 - slot)

    @pl.when(i == n - 1)
    def _():
        wait_step(slot)


def _dispatch(x, dest, fill_tiles, n_rows):
    T, D = x.shape
    tc = TC_DISPATCH
    return pl.pallas_call(
        _dispatch_kernel,
        grid_spec=pltpu.PrefetchScalarGridSpec(
            num_scalar_prefetch=2, grid=(T // tc,),
            in_specs=[pl.BlockSpec((tc, D), lambda i, d, ft: (i, 0))],
            out_specs=pl.BlockSpec(memory_space=pl.ANY),
            scratch_shapes=[pltpu.VMEM((2, tc * ROW_CHUNKS, LANES), F32),
                            pltpu.VMEM((TM_GROUP * ROW_CHUNKS, LANES), F32),
                            pltpu.SemaphoreType.DMA((2,)), pltpu.SemaphoreType.DMA(())]),
        out_shape=jax.ShapeDtypeStruct((n_rows * ROW_CHUNKS, LANES), F32),
        compiler_params=pltpu.CompilerParams(dimension_semantics=("arbitrary",)),
        name="dispatch",
    )(dest, fill_tiles, x)


def _expert_ffn_kernel(te_ref, nt_ref, xs_ref, w1_ref, w3_ref, w2_ref, o_ref, xb_ref, acc_ref):
    i = pl.program_id(0)
    f = pl.program_id(1)
    nf = pl.num_programs(1)
    tm = acc_ref.shape[0]

    @pl.when(i < nt_ref[0])
    def _():
        @pl.when(f == 0)
        def _():
            for c in range(ROW_CHUNKS):
                xb_ref[:, c * LANES:(c + 1) * LANES] = _load_token_tile_chunk(xs_ref, tm, c).astype(BF16)

        xb = xb_ref[...]
        a = _dot(xb, w1_ref[...])
        b = _dot(xb, w3_ref[...])
        h = (a * _sigmoid(a) * b).astype(BF16)
        part = _dot(h, w2_ref[...])

        @pl.when(f == 0)
        def _():
            acc_ref[...] = part

        @pl.when((f > 0) & (f < nf - 1))
        def _():
            acc_ref[...] += part

        @pl.when(f == nf - 1)
        def _():
            _store_token_tiles(o_ref, acc_ref[...] + part)

    @pl.when((i >= nt_ref[0]) & (f == 0))
    def _():
        o_ref[...] = jnp.zeros_like(o_ref)


def _expert_ffn(xs, te, nt, w1, w3, w2):
    E, D, FF = w1.shape
    tm, cf = TM_GROUP, FF_CHUNK_MOE
    nf = FF // cf
    max_tiles = xs.shape[0] // (tm * ROW_CHUNKS)

    def tile(i, nt_ref):
        return jnp.minimum(i, nt_ref[0] - 1)

    def chunk(i, f, nt_ref):
        return jnp.where(i < nt_ref[0], f, nf - 1)

    return pl.pallas_call(
        _expert_ffn_kernel,
        grid_spec=pltpu.PrefetchScalarGridSpec(
            num_scalar_prefetch=2, grid=(max_tiles, nf),
            in_specs=[pl.BlockSpec((tm * ROW_CHUNKS, LANES), lambda i, f, te, nt: (tile(i, nt), 0)),
                      pl.BlockSpec((None, D, cf), lambda i, f, te, nt: (te[tile(i, nt)], 0, chunk(i, f, nt))),
                      pl.BlockSpec((None, D, cf), lambda i, f, te, nt: (te[tile(i, nt)], 0, chunk(i, f, nt))),
                      pl.BlockSpec((None, cf, D), lambda i, f, te, nt: (te[tile(i, nt)], chunk(i, f, nt), 0))],
            out_specs=pl.BlockSpec((tm * ROW_CHUNKS, LANES), lambda i, f, te, nt: (i, 0)),
            scratch_shapes=[pltpu.VMEM((tm, D), BF16), pltpu.VMEM((tm, D), F32)]),
        out_shape=jax.ShapeDtypeStruct(xs.shape, F32),
        compiler_params=pltpu.CompilerParams(
            dimension_semantics=("arbitrary", "arbitrary"), vmem_limit_bytes=VMEM_LIMIT_BYTES),
        name="expert_ffn",
    )(te, nt, xs, w1, w3, w2)


def _sorted_layout(cnt, n_assign):
    tm = TM_GROUP
    max_tiles = n_assign // tm + N_EXPERTS
    tiles_e = (cnt + tm - 1) // tm
    tile_end = jnp.cumsum(tiles_e)
    row_start = (tile_end - tiles_e) * tm
    nt = tile_end[-1]
    tile_ids = jnp.minimum(jnp.arange(max_tiles, dtype=jnp.int32), nt - 1)
    te = jnp.sum((tile_end[None, :] <= tile_ids[:, None]).astype(jnp.int32), axis=1)
    te = jnp.minimum(te, N_EXPERTS - 1)
    last_tile = jnp.where(tiles_e > 0, tile_end - 1, -1)
    tail = nt + jnp.arange(N_EXPERTS, dtype=jnp.int32)
    tail = jnp.where(tail < max_tiles, tail, -1)
    fill_tiles = jnp.concatenate([last_tile, tail]).astype(jnp.int32)
    return row_start, te, nt.reshape(1).astype(jnp.int32), fill_tiles, max_tiles * tm


def _combine_kernel(alpha, dest_ref, x_ref, p_ref, gate_ref, ys_hbm, ln_g_ref, ln_b_ref, gate_w_f32, ple_w_f32,
                    o_ref, yb_ref, gate_w_ref, ple_w_ref, sems):
    i = pl.program_id(0)
    n = pl.num_programs(0)
    tc = x_ref.shape[0]
    n_tok = n * tc
    slot = i % 2

    @pl.when(i == 0)
    def _():
        gate_w_ref[...] = gate_w_f32[...].astype(BF16)
        ple_w_ref[...] = ple_w_f32[...].astype(BF16)

    def issue_step(step, s):
        base = step * tc

        def issue(rb, c):
            for j in range(ISSUE_UNROLL):
                r = rb * ISSUE_UNROLL + j
                for k in range(TOP_K):
                    d = dest_ref[k * n_tok + base + r]
                    pltpu.make_async_copy(ys_hbm.at[_tile_rows(d), :], yb_ref.at[s, k, _tile_rows(r), :],
                                          sems.at[s]).start()
            return c

        lax.fori_loop(0, tc // ISSUE_UNROLL, issue, 0)

    @pl.when(i == 0)
    def _():
        issue_step(0, 0)

    @pl.when(i + 1 < n)
    def _():
        issue_step(i + 1, 1 - slot)

    for k in range(TOP_K):
        pltpu.make_async_copy(ys_hbm.at[pl.ds(0, tc * ROW_CHUNKS), :], yb_ref.at[slot, k], sems.at[slot]).wait()

    g = gate_ref[...]
    f = (g[:, 0:1] * _load_token_tiles(yb_ref.at[slot, 0], tc)
         + g[:, 1:2] * _load_token_tiles(yb_ref.at[slot, 1], tc))
    o_ref[...] = _norm_and_embed(alpha, x_ref[...], f, p_ref[...], ln_g_ref[...], ln_b_ref[...],
                                 gate_w_ref[...], ple_w_ref[...])


def _combine_layer(alpha, layer, x, p, gates, dest, ys, ln_g, ln_b, gate_w, ple_w):
    T, D = x.shape
    tc = TC_COMBINE
    consts = [_per_layer(c) for c in (ln_g, ln_b, gate_w, ple_w)]
    return pl.pallas_call(
        functools.partial(_combine_kernel, alpha),
        grid_spec=pltpu.PrefetchScalarGridSpec(
            num_scalar_prefetch=1, grid=(T // tc,),
            in_specs=[pl.BlockSpec((tc, D), lambda i, d: (i, 0)),
                      pl.BlockSpec((None, tc, D_PLE), lambda i, d: (layer, i, 0)),
                      pl.BlockSpec((tc, TOP_K), lambda i, d: (i, 0)),
                      pl.BlockSpec(memory_space=pl.ANY)]
                     + [_layer_spec(c, layer) for c in consts],
            out_specs=pl.BlockSpec((tc, D), lambda i, d: (i, 0)),
            scratch_shapes=[pltpu.VMEM((2, TOP_K, tc * ROW_CHUNKS, LANES), F32),
                            pltpu.VMEM((D, D), BF16), pltpu.VMEM((D_PLE, D), BF16),
                            pltpu.SemaphoreType.DMA((2,))]),
        out_shape=jax.ShapeDtypeStruct((T, D), F32),
        compiler_params=pltpu.CompilerParams(
            dimension_semantics=("arbitrary",), vmem_limit_bytes=VMEM_LIMIT_BYTES),
        name="combine",
    )(dest, x, p, gates, ys, *consts)


def _moe_layer(alpha, layer, x, p, router_w, w1, w3, w2, ln_g, ln_b, gate_w, ple_w):
    T, D = x.shape
    route_i, route_g, counts = _router(x, router_w)
    idx = route_i[0:TOP_K]
    rank = route_i[TOP_K:2 * TOP_K]
    gates = route_g[0:TOP_K].T

    start, te, nt, fill_tiles, n_rows = _sorted_layout(counts[:, 0], TOP_K * T)
    start_of = sum(jnp.where(idx == e, start[e], 0) for e in range(N_EXPERTS))
    dest = (start_of + rank).astype(jnp.int32).reshape(-1)

    xs = _dispatch(x, dest, fill_tiles, n_rows)
    ys = _expert_ffn(xs, te, nt, w1, w3, w2)
    return _combine_layer(alpha, layer, x, p, gates, dest, ys, ln_g, ln_b, gate_w, ple_w)


def kernel(x, p, w_in, pool_w, pool_scale, conv_w, conv_b, conv_ln_g, conv_ln_b, conv_pw, w_out, ln1_g, ln1_b,
           dense_w1, dense_w3, dense_w2, router_w, exp_w1, exp_w3, exp_w2, ln2_g, ln2_b, ple_gate_w, ple_w):
    depth = w_in.shape[0]
    alpha = (2.0 * depth) ** 0.25
    B, S, D = x.shape
    pt = p.reshape(depth, B * S, D_PLE)
    one_group = lambda m: m.reshape(m.shape[0], 1, *m.shape[1:])
    for i in range(depth):
        j = i // 2
        is_dense = i % 2 == 0
        if is_dense:
            side = [(one_group(dense_w1), j), (one_group(dense_w3), j), (one_group(dense_w2), j),
                    (one_group(ple_gate_w), i), (one_group(ple_w), i)]
        else:
            side = [(exp_w1, j), (exp_w3, j), (exp_w2, j)]
        x, side_bf16 = _mixer(alpha, i, x, w_in, pool_w, pool_scale, conv_w, conv_b, conv_ln_g, conv_ln_b,
                              conv_pw, w_out, ln1_g, ln1_b, side)
        xt = x.reshape(B * S, D)
        if is_dense:
            w1b, w3b, w2b, gate_b, ple_b = side_bf16
            xt = _dense_layer(alpha, i, xt, pt, w1b, w3b, w2b, ln2_g, ln2_b, gate_b, ple_b)
        else:
            xt = _moe_layer(alpha, i, xt, pt, router_w[j], *side_bf16, ln2_g, ln2_b, ple_gate_w, ple_w)
        x = xt.reshape(B, S, D)
    return x
```

```python
import functools

import jax
import jax.numpy as jnp
from jax import lax
from jax.experimental import pallas as pl
from jax.experimental.pallas import tpu as pltpu

D_MODEL = 1024
D_PLE = 256
D_POOL = 512
D_CONV = D_MODEL - D_POOL
POOL_WINDOWS = (2, 4, 8, 16)
POOL_GROUP_DIM = D_POOL // len(POOL_WINDOWS)
CONV_WIDTH = 31
D_IN_PROJ = D_POOL + 2 * D_CONV
N_EXPERTS = 8
TOP_K = 2
LN_EPS = 1e-5

F32 = jnp.float32
BF16 = jnp.bfloat16

SUBLANES = 8
LANES = 128
VMEM_LIMIT_BYTES = 56 * 1024 * 1024

HALO = 32
TS_MIX = 512
RB_CONV = 128
IN_PROJ_CHUNK = 256
MIX_ROW_GROUPS = 2
TM_DENSE = 512
FF_CHUNK_DENSE = 1024
TR_ROUTE = 512
TM_GROUP = 512
FF_CHUNK_MOE = 1792
TC_DISPATCH = 512
TC_COMBINE = 256
ROW_CHUNKS = D_MODEL // LANES
ISSUE_UNROLL = 8


def _sigmoid(z):
    return 1.0 / (1.0 + jnp.exp(-z))


def _layer_norm(h, g, b):
    mu = jnp.mean(h, axis=-1, keepdims=True)
    c = h - mu
    var = jnp.mean(c * c, axis=-1, keepdims=True)
    return c * lax.rsqrt(var + LN_EPS) * g + b


def _dot(a, b):
    return jnp.dot(a, b, preferred_element_type=F32)


def _side_cast_copies(layers, q, src_hbm, dst_hbm, in_bufs, out_bufs, in_sems, out_sems):
    loads, stores = [], []
    for a in range(len(src_hbm)):
        rows = in_bufs[a].shape[0]
        per_expert = src_hbm[a].shape[2] // rows
        e = q // per_expert
        r0 = pl.multiple_of((q % per_expert) * rows, rows)
        loads.append(pltpu.make_async_copy(src_hbm[a].at[layers[a], e, pl.ds(r0, rows), :], in_bufs[a],
                                           in_sems.at[a]))
        stores.append(pltpu.make_async_copy(out_bufs[a], dst_hbm[a].at[e, pl.ds(r0, rows), :], out_sems.at[a]))
    return loads, stores


def _mixer_kernel(alpha, tiles_per_seq, cast_layers, n_cast_chunks, x_ref, w_in_f32, pool_w_f32, pool_scale_ref,
                  conv_w_ref, conv_b_ref, cln_g_ref, cln_b_ref, conv_pw_f32, w_out_f32, ln_g_ref, ln_b_ref, *rest):
    n_side = len(cast_layers)
    side_f32, rest = rest[:n_side], rest[n_side:]
    o_ref, side_bf16, rest = rest[0], rest[1:1 + n_side], rest[1 + n_side:]
    (p_scr, v_scr, u_scr, y_scr, ypool_scr, xprev_scr, w_in_ref, pool_w_ref, conv_pw_ref, w_out_ref) = rest[:10]
    cast_scratch = rest[10:]
    ts = x_ref.shape[0]
    t = pl.program_id(0)

    if n_side:
        in_bufs, out_bufs = cast_scratch[:n_side], cast_scratch[n_side:2 * n_side]
        in_sems, out_sems = cast_scratch[2 * n_side:]

        def cast_copies(q):
            return _side_cast_copies(cast_layers, q, side_f32, side_bf16, in_bufs, out_bufs, in_sems, out_sems)

        @pl.when(t == 0)
        def _():
            for load in cast_copies(0)[0]:
                load.start()

    @pl.when(t == 0)
    def _():
        w_in_ref[...] = w_in_f32[...].astype(BF16)
        pool_w_ref[...] = pool_w_f32[...].astype(BF16)
        conv_pw_ref[...] = conv_pw_f32[...].astype(BF16)
        w_out_ref[...] = w_out_f32[...].astype(BF16)
        p_scr[...] = jnp.zeros_like(p_scr)
        v_scr[...] = jnp.zeros_like(v_scr)
        ypool_scr[...] = jnp.zeros_like(ypool_scr)
        xprev_scr[...] = jnp.zeros_like(xprev_scr)

    x = x_ref[...]
    xb = x.astype(BF16)
    n_conv_blocks = ts // RB_CONV
    in_proj_chunks = [(c0, min(c0 + IN_PROJ_CHUNK, D_IN_PROJ)) for c0 in range(0, D_IN_PROJ, IN_PROJ_CHUNK)]

    blocks_per_group = n_conv_blocks // MIX_ROW_GROUPS
    for rb in range(n_conv_blocks):
        r0 = rb * RB_CONV
        cols = []
        for lc in range(D_CONV // LANES):
            lanes = slice(lc * LANES, (lc + 1) * LANES)
            acc = jnp.zeros((RB_CONV, LANES), F32)
            for k in range(CONV_WIDTH):
                first = r0 + HALO - (CONV_WIDTH - 1 - k)
                acc = acc + v_scr[lc, first:first + RB_CONV, :] * conv_w_ref[k:k + 1, lanes]
            cols.append(acc)
        y = jnp.concatenate(cols, axis=-1) + conv_b_ref[...]
        z = _layer_norm(y, cln_g_ref[...], cln_b_ref[...])
        y_scr[r0:r0 + RB_CONV, :] = (z * _sigmoid(z)).astype(BF16)
        if rb < len(in_proj_chunks):
            c0, c1 = in_proj_chunks[rb]
            u_scr[:, c0:c1] = _dot(xb, w_in_ref[:, c0:c1])
        if (rb + 1) % blocks_per_group == 0:
            rows = slice((rb + 1 - blocks_per_group) * RB_CONV, (rb + 1) * RB_CONV)
            y_conv = _dot(y_scr[rows, :], conv_pw_ref[...])
            heads = jnp.concatenate([ypool_scr[rows, :], y_conv.astype(BF16)], axis=-1)
            mix = _dot(heads, w_out_ref[...])
            o_ref[rows, :] = _layer_norm(alpha * xprev_scr[rows, :] + mix, ln_g_ref[...], ln_b_ref[...])
    for c0, c1 in in_proj_chunks[n_conv_blocks:]:
        u_scr[:, c0:c1] = _dot(xb, w_in_ref[:, c0:c1])

    tile_in_seq = t % tiles_per_seq
    starts_seq = tile_in_seq == 0
    p_scr[:, 0:HALO, :] = jnp.where(starts_seq, 0.0, p_scr[:, ts:ts + HALO, :])
    v_scr[:, 0:HALO, :] = jnp.where(starts_seq, 0.0, v_scr[:, ts:ts + HALO, :])
    for c in range(D_CONV // LANES):
        val = u_scr[:, D_POOL + c * LANES:D_POOL + (c + 1) * LANES]
        gate = u_scr[:, D_POOL + D_CONV + c * LANES:D_POOL + D_CONV + (c + 1) * LANES]
        v_scr[c, HALO:, :] = val * _sigmoid(gate)

    pos = tile_in_seq * ts + lax.broadcasted_iota(jnp.int32, (ts, POOL_GROUP_DIM), 0)
    for g, w in enumerate(POOL_WINDOWS):
        lanes = slice(g * POOL_GROUP_DIM, (g + 1) * POOL_GROUP_DIM)
        cur = u_scr[:, lanes]
        p_scr[g, HALO:, :] = cur
        win = cur
        for back in range(1, w):
            win = win + p_scr[g, HALO - back:HALO - back + ts, :]
        cnt = jnp.minimum(pos + 1, w).astype(F32)
        d = win / cnt - cur
        yg = _dot(d.astype(BF16), pool_w_ref[g]) * pool_scale_ref[:, lanes]
        ypool_scr[:, lanes] = yg.astype(BF16)

    xprev_scr[...] = x

    if n_side:
        @pl.when(t < n_cast_chunks)
        def _():
            loads, stores = cast_copies(t)
            for load in loads:
                load.wait()

            @pl.when(t > 0)
            def _():
                for store in cast_copies(t - 1)[1]:
                    store.wait()

            for src, dst in zip(in_bufs, out_bufs):
                dst[...] = src[...].astype(BF16)
            for store in stores:
                store.start()

            @pl.when(t + 1 < n_cast_chunks)
            def _():
                for load in cast_copies(t + 1)[0]:
                    load.start()

        @pl.when(t == n_cast_chunks)
        def _():
            for store in cast_copies(t - 1)[1]:
                store.wait()


def _const_spec(shape):
    nd = len(shape)
    return pl.BlockSpec(shape, lambda *_: (0,) * nd, pipeline_mode=pl.Buffered(1))


def _per_layer(v):
    return v.reshape(v.shape[0], 1, v.shape[1]) if v.ndim == 2 else v


def _layer_spec(v, layer, **kwargs):
    nd = v.ndim
    return pl.BlockSpec((None,) + v.shape[1:], lambda *_: (layer,) + (0,) * (nd - 1),
                        pipeline_mode=pl.Buffered(1), **kwargs)


def _mixer(alpha, layer, x, w_in, pool_w, pool_scale, conv_w, conv_b, cln_g, cln_b, conv_pw, w_out, ln_g, ln_b,
           side_weights):
    B, S, D = x.shape
    ts = TS_MIX
    n_tiles = B * S // ts
    consts = [_per_layer(c) for c in (w_in, pool_w, pool_scale, conv_w, conv_b, cln_g, cln_b, conv_pw, w_out,
                                      ln_g, ln_b)]
    x_spec = pl.BlockSpec((ts, D), lambda t: (jnp.minimum(t, n_tiles - 1), 0))
    o_spec = pl.BlockSpec((ts, D), lambda t: (jnp.maximum(t - 1, 0), 0))
    o_shape = jax.ShapeDtypeStruct((B * S, D), F32)
    scratch = [pltpu.VMEM((D_POOL // LANES, HALO + ts, LANES), F32),
               pltpu.VMEM((D_CONV // LANES, HALO + ts, LANES), F32),
               pltpu.VMEM((ts, D_IN_PROJ), F32),
               pltpu.VMEM((ts, D_CONV), BF16),
               pltpu.VMEM((ts, D_POOL), BF16),
               pltpu.VMEM((ts, D), F32),
               pltpu.VMEM(w_in.shape[1:], BF16), pltpu.VMEM(pool_w.shape[1:], BF16),
               pltpu.VMEM(conv_pw.shape[1:], BF16), pltpu.VMEM(w_out.shape[1:], BF16)]
    params = pltpu.CompilerParams(dimension_semantics=("arbitrary",), vmem_limit_bytes=VMEM_LIMIT_BYTES)
    in_specs = [x_spec] + [_layer_spec(c, layer) for c in consts]

    mats = [m for m, _ in side_weights]
    cast_layers = tuple(l for _, l in side_weights)
    hbm = pl.BlockSpec(memory_space=pl.ANY)
    bf16_rows = 2 * SUBLANES
    n_chunks = max(n for n in range(1, n_tiles + 1)
                   if all((m.shape[1] * m.shape[2]) % n == 0
                          and m.shape[2] % ((m.shape[1] * m.shape[2]) // n) == 0
                          and ((m.shape[1] * m.shape[2]) // n) % bf16_rows == 0 for m in mats))
    chunk_rows = [m.shape[1] * m.shape[2] // n_chunks for m in mats]
    cast_scratch = ([pltpu.VMEM((rows, m.shape[3]), F32) for m, rows in zip(mats, chunk_rows)]
                    + [pltpu.VMEM((rows, m.shape[3]), BF16) for m, rows in zip(mats, chunk_rows)]
                    + [pltpu.SemaphoreType.DMA((len(mats),)), pltpu.SemaphoreType.DMA((len(mats),))])
    out, *mats_bf16 = pl.pallas_call(
        functools.partial(_mixer_kernel, alpha, S // ts, cast_layers, n_chunks),
        grid=(n_tiles + 1,), in_specs=in_specs + [hbm] * len(mats),
        out_specs=[o_spec] + [hbm] * len(mats),
        out_shape=[o_shape] + [jax.ShapeDtypeStruct(m.shape[1:], BF16) for m in mats],
        scratch_shapes=scratch + cast_scratch, compiler_params=params, name="mixer",
    )(x.reshape(B * S, D), *consts, *mats)
    return out.reshape(B, S, D), mats_bf16


def _norm_and_embed(alpha, x, f, p, ln_g, ln_b, gate_w, ple_w):
    h = _layer_norm(alpha * x + f, ln_g, ln_b)
    gate = _sigmoid(_dot(h.astype(BF16), gate_w))
    return h + gate * _dot(p.astype(BF16), ple_w)


def _dense_ffn_kernel(alpha, x_ref, p_ref, w1_ref, w3_ref, w2_ref, ln_g_ref, ln_b_ref, gate_w_ref, ple_w_ref,
                      o_ref, acc_ref):
    x = x_ref[...]
    xb = x.astype(BF16)
    ff = w1_ref.shape[1]
    for c0 in range(0, ff, FF_CHUNK_DENSE):
        c1 = min(c0 + FF_CHUNK_DENSE, ff)
        a = _dot(xb, w1_ref[:, c0:c1])
        b = _dot(xb, w3_ref[:, c0:c1])
        h = (a * _sigmoid(a) * b).astype(BF16)
        part = _dot(h, w2_ref[c0:c1, :])
        if c0 == 0:
            acc_ref[...] = part
        else:
            acc_ref[...] += part
    o_ref[...] = _norm_and_embed(alpha, x, acc_ref[...], p_ref[...], ln_g_ref[...], ln_b_ref[...],
                                 gate_w_ref[...], ple_w_ref[...])


def _dense_layer(alpha, layer, x, p, w1, w3, w2, ln_g, ln_b, gate_w, ple_w):
    T, D = x.shape
    tm = TM_DENSE
    ln_g, ln_b = _per_layer(ln_g), _per_layer(ln_b)
    return pl.pallas_call(
        functools.partial(_dense_ffn_kernel, alpha),
        grid=(T // tm,),
        in_specs=[pl.BlockSpec((tm, D), lambda i: (i, 0)),
                  pl.BlockSpec((None, tm, D_PLE), lambda i: (layer, i, 0)),
                  _layer_spec(w1, 0), _layer_spec(w3, 0), _layer_spec(w2, 0),
                  _layer_spec(ln_g, layer), _layer_spec(ln_b, layer),
                  _layer_spec(gate_w, 0), _layer_spec(ple_w, 0)],
        out_specs=pl.BlockSpec((tm, D), lambda i: (i, 0)),
        out_shape=jax.ShapeDtypeStruct((T, D), F32),
        scratch_shapes=[pltpu.VMEM((tm, D), F32)],
        compiler_params=pltpu.CompilerParams(
            dimension_semantics=("arbitrary",), vmem_limit_bytes=VMEM_LIMIT_BYTES),
        name="dense_ffn",
    )(x, p, w1, w3, w2, ln_g, ln_b, gate_w, ple_w)


def _router_kernel(x_ref, wt_ref, idx_ref, gate_ref, cnt_ref, carry_ref):
    i = pl.program_id(0)
    tr = x_ref.shape[0]

    @pl.when(i == 0)
    def _():
        carry_ref[...] = jnp.zeros_like(carry_ref)

    x = x_ref[...]
    wt = wt_ref[...]
    xh = x.astype(BF16)
    xl = (x - xh.astype(F32)).astype(BF16)
    wh = wt.astype(BF16)
    wl = (wt - wh.astype(F32)).astype(BF16)
    nt_dims = (((1,), (1,)), ((), ()))
    dg = lambda a, b: lax.dot_general(a, b, nt_dims, preferred_element_type=F32)
    logits = dg(wh, xh) + (dg(wh, xl) + dg(wl, xh))

    eid = lax.broadcasted_iota(jnp.int32, logits.shape, 0)
    m1 = jnp.max(logits, axis=0, keepdims=True)
    i1 = jnp.min(jnp.where(logits == m1, eid, N_EXPERTS), axis=0, keepdims=True)
    rest = jnp.where(eid == i1, -jnp.inf, logits)
    m2 = jnp.max(rest, axis=0, keepdims=True)
    i2 = jnp.min(jnp.where(rest == m2, eid, N_EXPERTS), axis=0, keepdims=True)
    e2 = jnp.exp(m2 - m1)
    g1 = 1.0 / (1.0 + e2)
    g2 = e2 / (1.0 + e2)

    oh1 = (eid == i1).astype(F32)
    oh2 = (eid == i2).astype(F32)
    chosen = oh1 + oh2
    r_i = lax.broadcasted_iota(jnp.int32, (tr, tr), 0)
    c_i = lax.broadcasted_iota(jnp.int32, (tr, tr), 1)
    before = (r_i < c_i).astype(BF16)
    excl = _dot(chosen.astype(BF16), before) + carry_ref[:, 0:1]
    rank1 = jnp.sum(oh1 * excl, axis=0, keepdims=True)
    rank2 = jnp.sum(oh2 * excl, axis=0, keepdims=True)

    total = carry_ref[:, 0:1] + jnp.sum(chosen, axis=1, keepdims=True)
    carry_ref[...] = jnp.broadcast_to(total, carry_ref.shape)
    cnt_ref[...] = jnp.broadcast_to(total, cnt_ref.shape).astype(jnp.int32)

    zi = jnp.zeros((SUBLANES - 4, tr), jnp.int32)
    idx_ref[...] = jnp.concatenate(
        [i1, i2, rank1.astype(jnp.int32), rank2.astype(jnp.int32), zi], axis=0)
    gate_ref[...] = jnp.concatenate([g1, g2, jnp.zeros((SUBLANES - 2, tr), F32)], axis=0)


def _router(x, router_w):
    T, D = x.shape
    tr = TR_ROUTE
    return pl.pallas_call(
        _router_kernel,
        grid=(T // tr,),
        in_specs=[pl.BlockSpec((tr, D), lambda i: (i, 0)), _const_spec((N_EXPERTS, D))],
        out_specs=[pl.BlockSpec((SUBLANES, tr), lambda i: (0, i)),
                   pl.BlockSpec((SUBLANES, tr), lambda i: (0, i)),
                   pl.BlockSpec((N_EXPERTS, LANES), lambda i: (0, 0))],
        out_shape=[jax.ShapeDtypeStruct((SUBLANES, T), jnp.int32),
                   jax.ShapeDtypeStruct((SUBLANES, T), F32),
                   jax.ShapeDtypeStruct((N_EXPERTS, LANES), jnp.int32)],
        scratch_shapes=[pltpu.VMEM((N_EXPERTS, LANES), F32)],
        compiler_params=pltpu.CompilerParams(
            dimension_semantics=("arbitrary",), vmem_limit_bytes=VMEM_LIMIT_BYTES),
        name="router",
    )(x, router_w.T)


def _store_token_tiles(dst_ref, value):
    rows = value.shape[0]
    for c in range(ROW_CHUNKS):
        dst_ref[pl.ds(c, rows, stride=ROW_CHUNKS), :] = value[:, c * LANES:(c + 1) * LANES]


def _load_token_tile_chunk(src_ref, rows, c):
    return src_ref[pl.ds(c, rows, stride=ROW_CHUNKS), :]


def _load_token_tiles(src_ref, rows):
    return jnp.concatenate([_load_token_tile_chunk(src_ref, rows, c) for c in range(ROW_CHUNKS)], axis=-1)


def _tile_rows(row):
    return pl.ds(pl.multiple_of(row * ROW_CHUNKS, ROW_CHUNKS), ROW_CHUNKS)


def _dispatch_kernel(dest_ref, fill_ref, x_ref, xs_hbm, xt_ref, zero_ref, sems, fill_sem):
    i = pl.program_id(0)
    n = pl.num_programs(0)
    tc = x_ref.shape[0]
    n_tok = n * tc
    base = i * tc
    slot = i % 2
    stage = xt_ref.at[slot]

    @pl.when(i == 0)
    def _():
        zero_ref[...] = jnp.zeros_like(zero_ref)
        tile_rows = zero_ref.shape[0]

        def fill(j):
            start = pl.multiple_of(fill_ref[j] * tile_rows, tile_rows)
            return pltpu.make_async_copy(zero_ref, xs_hbm.at[pl.ds(start, tile_rows), :], fill_sem)

        for j in range(fill_ref.shape[0]):
            @pl.when(fill_ref[j] >= 0)
            def _():
                fill(j).start()

        for j in range(fill_ref.shape[0]):
            @pl.when(fill_ref[j] >= 0)
            def _():
                fill(j).wait()

    _store_token_tiles(stage, x_ref[...])

    def copy(k, r):
        d = dest_ref[k * n_tok + base + r]
        return pltpu.make_async_copy(stage.at[_tile_rows(r), :], xs_hbm.at[_tile_rows(d), :], sems.at[slot])

    def issue(rb, c):
        for j in range(ISSUE_UNROLL):
            r = rb * ISSUE_UNROLL + j
            for k in range(TOP_K):
                copy(k, r).start(priority=k)
        return c

    lax.fori_loop(0, tc // ISSUE_UNROLL, issue, 0)

    def wait_step(s):
        for _ in range(TOP_K):
            pltpu.make_async_copy(xt_ref.at[s], xs_hbm.at[pl.ds(0, tc * ROW_CHUNKS), :], sems.at[s]).wait()

    @pl.when(i > 0)
    def _():
        wait_step(1 - slot)

    @pl.when(i == n - 1)
    def _():
        wait_step(slot)


def _dispatch(x, dest, fill_tiles, n_rows):
    T, D = x.shape
    tc = TC_DISPATCH
    return pl.pallas_call(
        _dispatch_kernel,
        grid_spec=pltpu.PrefetchScalarGridSpec(
            num_scalar_prefetch=2, grid=(T // tc,),
            in_specs=[pl.BlockSpec((tc, D), lambda i, d, ft: (i, 0))],
            out_specs=pl.BlockSpec(memory_space=pl.ANY),
            scratch_shapes=[pltpu.VMEM((2, tc * ROW_CHUNKS, LANES), F32),
                            pltpu.VMEM((TM_GROUP * ROW_CHUNKS, LANES), F32),
                            pltpu.SemaphoreType.DMA((2,)), pltpu.SemaphoreType.DMA(())]),
        out_shape=jax.ShapeDtypeStruct((n_rows * ROW_CHUNKS, LANES), F32),
        compiler_params=pltpu.CompilerParams(dimension_semantics=("arbitrary",)),
        name="dispatch",
    )(dest, fill_tiles, x)


def _expert_ffn_kernel(te_ref, valid_ref, nt_ref, xs_ref, w1_ref, w3_ref, w2_ref, o_ref, xb_ref, acc_ref):
    i = pl.program_id(0)
    f = pl.program_id(1)
    nf = pl.num_programs(1)
    tm = acc_ref.shape[0]

    def run(m):
        @pl.when(f == 0)
        def _():
            for c in range(ROW_CHUNKS):
                chunk = _load_token_tile_chunk(xs_ref.at[pl.ds(0, m * ROW_CHUNKS), :], m, c)
                xb_ref[0:m, c * LANES:(c + 1) * LANES] = chunk.astype(BF16)

        xb = xb_ref[0:m, :]
        a = _dot(xb, w1_ref[...])
        b = _dot(xb, w3_ref[...])
        h = (a * _sigmoid(a) * b).astype(BF16)
        part = _dot(h, w2_ref[...])

        @pl.when(f == 0)
        def _():
            acc_ref[0:m, :] = part

        @pl.when((f > 0) & (f < nf - 1))
        def _():
            acc_ref[0:m, :] += part

        @pl.when(f == nf - 1)
        def _():
            _store_token_tiles(o_ref.at[pl.ds(0, m * ROW_CHUNKS), :], acc_ref[0:m, :] + part)
            if m < tm:
                o_ref[m * ROW_CHUNKS:, :] = jnp.zeros(((tm - m) * ROW_CHUNKS, LANES), F32)

    live = i < nt_ref[0]
    few = valid_ref[i] <= tm // 2

    @pl.when(live & jnp.logical_not(few))
    def _():
        run(tm)

    @pl.when(live & few)
    def _():
        run(tm // 2)

    @pl.when(jnp.logical_not(live) & (f == 0))
    def _():
        o_ref[...] = jnp.zeros_like(o_ref)


def _expert_ffn(xs, te, valid, nt, w1, w3, w2):
    E, D, FF = w1.shape
    tm, cf = TM_GROUP, FF_CHUNK_MOE
    nf = FF // cf
    max_tiles = xs.shape[0] // (tm * ROW_CHUNKS)

    def tile(i, nt_ref):
        return jnp.minimum(i, nt_ref[0] - 1)

    def chunk(i, f, nt_ref):
        return jnp.where(i < nt_ref[0], f, nf - 1)

    def w13_map(i, f, te, valid, nt):
        return (te[tile(i, nt)], 0, chunk(i, f, nt))

    def w2_map(i, f, te, valid, nt):
        return (te[tile(i, nt)], chunk(i, f, nt), 0)

    return pl.pallas_call(
        _expert_ffn_kernel,
        grid_spec=pltpu.PrefetchScalarGridSpec(
            num_scalar_prefetch=3, grid=(max_tiles, nf),
            in_specs=[pl.BlockSpec((tm * ROW_CHUNKS, LANES), lambda i, f, te, valid, nt: (tile(i, nt), 0)),
                      pl.BlockSpec((None, D, cf), w13_map),
                      pl.BlockSpec((None, D, cf), w13_map),
                      pl.BlockSpec((None, cf, D), w2_map)],
            out_specs=pl.BlockSpec((tm * ROW_CHUNKS, LANES), lambda i, f, te, valid, nt: (i, 0)),
            scratch_shapes=[pltpu.VMEM((tm, D), BF16), pltpu.VMEM((tm, D), F32)]),
        out_shape=jax.ShapeDtypeStruct(xs.shape, F32),
        compiler_params=pltpu.CompilerParams(
            dimension_semantics=("arbitrary", "arbitrary"), vmem_limit_bytes=VMEM_LIMIT_BYTES),
        name="expert_ffn",
    )(te, valid, nt, xs, w1, w3, w2)


def _sorted_layout(cnt, n_assign):
    tm = TM_GROUP
    max_tiles = n_assign // tm + N_EXPERTS
    tiles_e = (cnt + tm - 1) // tm
    tile_end = jnp.cumsum(tiles_e)
    row_start = (tile_end - tiles_e) * tm
    nt = tile_end[-1]
    tile_ids = jnp.minimum(jnp.arange(max_tiles, dtype=jnp.int32), nt - 1)
    te = jnp.sum((tile_end[None, :] <= tile_ids[:, None]).astype(jnp.int32), axis=1)
    te = jnp.minimum(te, N_EXPERTS - 1)
    first_tile = tile_end - tiles_e
    valid = jnp.clip(cnt[te] - (tile_ids - first_tile[te]) * tm, 0, tm).astype(jnp.int32)
    last_tile = jnp.where(tiles_e > 0, tile_end - 1, -1)
    tail = nt + jnp.arange(N_EXPERTS, dtype=jnp.int32)
    tail = jnp.where(tail < max_tiles, tail, -1)
    fill_tiles = jnp.concatenate([last_tile, tail]).astype(jnp.int32)
    return row_start, te, valid, nt.reshape(1).astype(jnp.int32), fill_tiles, max_tiles * tm


def _combine_kernel(alpha, dest_ref, x_ref, p_ref, gate_ref, ys_hbm, ln_g_ref, ln_b_ref, gate_w_f32, ple_w_f32,
                    o_ref, yb_ref, gate_w_ref, ple_w_ref, sems):
    i = pl.program_id(0)
    n = pl.num_programs(0)
    tc = x_ref.shape[0]
    n_tok = n * tc
    slot = i % 2

    @pl.when(i == 0)
    def _():
        gate_w_ref[...] = gate_w_f32[...].astype(BF16)
        ple_w_ref[...] = ple_w_f32[...].astype(BF16)

    def issue_step(step, s):
        base = step * tc

        def issue(rb, c):
            for j in range(ISSUE_UNROLL):
                r = rb * ISSUE_UNROLL + j
                for k in range(TOP_K):
                    d = dest_ref[k * n_tok + base + r]
                    pltpu.make_async_copy(ys_hbm.at[_tile_rows(d), :], yb_ref.at[s, k, _tile_rows(r), :],
                                          sems.at[s]).start()
            return c

        lax.fori_loop(0, tc // ISSUE_UNROLL, issue, 0)

    @pl.when(i == 0)
    def _():
        issue_step(0, 0)

    @pl.when(i + 1 < n)
    def _():
        issue_step(i + 1, 1 - slot)

    for k in range(TOP_K):
        pltpu.make_async_copy(ys_hbm.at[pl.ds(0, tc * ROW_CHUNKS), :], yb_ref.at[slot, k], sems.at[slot]).wait()

    g = gate_ref[...]
    f = (g[:, 0:1] * _load_token_tiles(yb_ref.at[slot, 0], tc)
         + g[:, 1:2] * _load_token_tiles(yb_ref.at[slot, 1], tc))
    o_ref[...] = _norm_and_embed(alpha, x_ref[...], f, p_ref[...], ln_g_ref[...], ln_b_ref[...],
                                 gate_w_ref[...], ple_w_ref[...])


def _combine_layer(alpha, layer, x, p, gates, dest, ys, ln_g, ln_b, gate_w, ple_w):
    T, D = x.shape
    tc = TC_COMBINE
    consts = [_per_layer(c) for c in (ln_g, ln_b, gate_w, ple_w)]
    return pl.pallas_call(
        functools.partial(_combine_kernel, alpha),
        grid_spec=pltpu.PrefetchScalarGridSpec(
            num_scalar_prefetch=1, grid=(T // tc,),
            in_specs=[pl.BlockSpec((tc, D), lambda i, d: (i, 0)),
                      pl.BlockSpec((None, tc, D_PLE), lambda i, d: (layer, i, 0)),
                      pl.BlockSpec((tc, TOP_K), lambda i, d: (i, 0)),
                      pl.BlockSpec(memory_space=pl.ANY)]
                     + [_layer_spec(c, layer) for c in consts],
            out_specs=pl.BlockSpec((tc, D), lambda i, d: (i, 0)),
            scratch_shapes=[pltpu.VMEM((2, TOP_K, tc * ROW_CHUNKS, LANES), F32),
                            pltpu.VMEM((D, D), BF16), pltpu.VMEM((D_PLE, D), BF16),
                            pltpu.SemaphoreType.DMA((2,))]),
        out_shape=jax.ShapeDtypeStruct((T, D), F32),
        compiler_params=pltpu.CompilerParams(
            dimension_semantics=("arbitrary",), vmem_limit_bytes=VMEM_LIMIT_BYTES),
        name="combine",
    )(dest, x, p, gates, ys, *consts)


def _moe_layer(alpha, layer, x, p, router_w, w1, w3, w2, ln_g, ln_b, gate_w, ple_w):
    T, D = x.shape
    route_i, route_g, counts = _router(x, router_w)
    idx = route_i[0:TOP_K]
    rank = route_i[TOP_K:2 * TOP_K]
    gates = route_g[0:TOP_K].T

    start, te, valid, nt, fill_tiles, n_rows = _sorted_layout(counts[:, 0], TOP_K * T)
    start_of = sum(jnp.where(idx == e, start[e], 0) for e in range(N_EXPERTS))
    dest = (start_of + rank).astype(jnp.int32).reshape(-1)

    xs = _dispatch(x, dest, fill_tiles, n_rows)
    ys = _expert_ffn(xs, te, valid, nt, w1, w3, w2)
    return _combine_layer(alpha, layer, x, p, gates, dest, ys, ln_g, ln_b, gate_w, ple_w)


def kernel(x, p, w_in, pool_w, pool_scale, conv_w, conv_b, conv_ln_g, conv_ln_b, conv_pw, w_out, ln1_g, ln1_b,
           dense_w1, dense_w3, dense_w2, router_w, exp_w1, exp_w3, exp_w2, ln2_g, ln2_b, ple_gate_w, ple_w):
    depth = w_in.shape[0]
    alpha = (2.0 * depth) ** 0.25
    B, S, D = x.shape
    pt = p.reshape(depth, B * S, D_PLE)
    one_group = lambda m: m.reshape(m.shape[0], 1, *m.shape[1:])
    for i in range(depth):
        j = i // 2
        is_dense = i % 2 == 0
        if is_dense:
            side = [(one_group(dense_w1), j), (one_group(dense_w3), j), (one_group(dense_w2), j),
                    (one_group(ple_gate_w), i), (one_group(ple_w), i)]
        else:
            side = [(exp_w1, j), (exp_w3, j), (exp_w2, j)]
        x, side_bf16 = _mixer(alpha, i, x, w_in, pool_w, pool_scale, conv_w, conv_b, conv_ln_g, conv_ln_b,
                              conv_pw, w_out, ln1_g, ln1_b, side)
        xt = x.reshape(B * S, D)
        if is_dense:
            w1b, w3b, w2b, gate_b, ple_b = side_bf16
            xt = _dense_layer(alpha, i, xt, pt, w1b, w3b, w2b, ln2_g, ln2_b, gate_b, ple_b)
        else:
            xt = _moe_layer(alpha, i, xt, pt, router_w[j], *side_bf16, ln2_g, ln2_b, ple_gate_w, ple_w)
        x = xt.reshape(B, S, D)
    return x
```

```python
import functools

import jax
import jax.numpy as jnp
from jax import lax
from jax.experimental import pallas as pl
from jax.experimental.pallas import tpu as pltpu

D_MODEL = 1024
D_PLE = 256
D_POOL = 512
D_CONV = D_MODEL - D_POOL
POOL_WINDOWS = (2, 4, 8, 16)
POOL_GROUP_DIM = D_POOL // len(POOL_WINDOWS)
CONV_WIDTH = 31
D_IN_PROJ = D_POOL + 2 * D_CONV
N_EXPERTS = 8
TOP_K = 2
LN_EPS = 1e-5

F32 = jnp.float32
BF16 = jnp.bfloat16

SUBLANES = 8
LANES = 128
VMEM_LIMIT_BYTES = 56 * 1024 * 1024

HALO = 32
TS_MIX = 512
RB_CONV = 128
IN_PROJ_CHUNK = 256
MIX_ROW_GROUPS = 2
TM_DENSE = 512
FF_CHUNK_DENSE = 1024
TR_ROUTE = 512
TM_GROUP = 512
FF_CHUNK_MOE = 1792
TC_DISPATCH = 512
TC_COMBINE = 256
ROW_CHUNKS = D_MODEL // LANES
ISSUE_UNROLL = 8


def _sigmoid(z):
    return 1.0 / (1.0 + jnp.exp(-z))


def _layer_norm(h, g, b):
    mu = jnp.mean(h, axis=-1, keepdims=True)
    c = h - mu
    var = jnp.mean(c * c, axis=-1, keepdims=True)
    return c * lax.rsqrt(var + LN_EPS) * g + b


def _dot(a, b):
    return jnp.dot(a, b, preferred_element_type=F32)


def _side_cast_copies(layers, q, src_hbm, dst_hbm, in_bufs, out_bufs, in_sems, out_sems):
    loads, stores = [], []
    for a in range(len(src_hbm)):
        rows = in_bufs[a].shape[0]
        per_expert = src_hbm[a].shape[2] // rows
        e = q // per_expert
        r0 = pl.multiple_of((q % per_expert) * rows, rows)
        loads.append(pltpu.make_async_copy(src_hbm[a].at[layers[a], e, pl.ds(r0, rows), :], in_bufs[a],
                                           in_sems.at[a]))
        stores.append(pltpu.make_async_copy(out_bufs[a], dst_hbm[a].at[e, pl.ds(r0, rows), :], out_sems.at[a]))
    return loads, stores


def _mixer_kernel(alpha, tiles_per_seq, cast_layers, n_cast_chunks, x_ref, w_in_f32, pool_w_f32, pool_scale_ref,
                  conv_w_ref, conv_b_ref, cln_g_ref, cln_b_ref, conv_pw_f32, w_out_f32, ln_g_ref, ln_b_ref, *rest):
    n_side = len(cast_layers)
    side_f32, rest = rest[:n_side], rest[n_side:]
    o_ref, side_bf16, rest = rest[0], rest[1:1 + n_side], rest[1 + n_side:]
    (p_scr, v_scr, u_scr, y_scr, ypool_scr, xprev_scr, w_in_ref, pool_w_ref, conv_pw_ref, w_out_ref) = rest[:10]
    cast_scratch = rest[10:]
    ts = x_ref.shape[0]
    t = pl.program_id(0)

    if n_side:
        in_bufs, out_bufs = cast_scratch[:n_side], cast_scratch[n_side:2 * n_side]
        in_sems, out_sems = cast_scratch[2 * n_side:]

        def cast_copies(q):
            return _side_cast_copies(cast_layers, q, side_f32, side_bf16, in_bufs, out_bufs, in_sems, out_sems)

        @pl.when(t == 0)
        def _():
            for load in cast_copies(0)[0]:
                load.start()

    @pl.when(t == 0)
    def _():
        w_in_ref[...] = w_in_f32[...].astype(BF16)
        pool_w_ref[...] = pool_w_f32[...].astype(BF16)
        conv_pw_ref[...] = conv_pw_f32[...].astype(BF16)
        w_out_ref[...] = w_out_f32[...].astype(BF16)
        p_scr[...] = jnp.zeros_like(p_scr)
        v_scr[...] = jnp.zeros_like(v_scr)
        ypool_scr[...] = jnp.zeros_like(ypool_scr)
        xprev_scr[...] = jnp.zeros_like(xprev_scr)

    x = x_ref[...]
    xb = x.astype(BF16)
    n_conv_blocks = ts // RB_CONV
    in_proj_chunks = [(c0, min(c0 + IN_PROJ_CHUNK, D_IN_PROJ)) for c0 in range(0, D_IN_PROJ, IN_PROJ_CHUNK)]

    blocks_per_group = n_conv_blocks // MIX_ROW_GROUPS
    for rb in range(n_conv_blocks):
        r0 = rb * RB_CONV
        cols = []
        for lc in range(D_CONV // LANES):
            lanes = slice(lc * LANES, (lc + 1) * LANES)
            acc = jnp.zeros((RB_CONV, LANES), F32)
            for k in range(CONV_WIDTH):
                first = r0 + HALO - (CONV_WIDTH - 1 - k)
                acc = acc + v_scr[lc, first:first + RB_CONV, :] * conv_w_ref[k:k + 1, lanes]
            cols.append(acc)
        y = jnp.concatenate(cols, axis=-1) + conv_b_ref[...]
        z = _layer_norm(y, cln_g_ref[...], cln_b_ref[...])
        y_scr[r0:r0 + RB_CONV, :] = (z * _sigmoid(z)).astype(BF16)
        if rb < len(in_proj_chunks):
            c0, c1 = in_proj_chunks[rb]
            u_scr[:, c0:c1] = _dot(xb, w_in_ref[:, c0:c1])
        if (rb + 1) % blocks_per_group == 0:
            rows = slice((rb + 1 - blocks_per_group) * RB_CONV, (rb + 1) * RB_CONV)
            y_conv = _dot(y_scr[rows, :], conv_pw_ref[...])
            heads = jnp.concatenate([ypool_scr[rows, :], y_conv.astype(BF16)], axis=-1)
            mix = _dot(heads, w_out_ref[...])
            o_ref[rows, :] = _layer_norm(alpha * xprev_scr[rows, :] + mix, ln_g_ref[...], ln_b_ref[...])
    for c0, c1 in in_proj_chunks[n_conv_blocks:]:
        u_scr[:, c0:c1] = _dot(xb, w_in_ref[:, c0:c1])

    tile_in_seq = t % tiles_per_seq
    starts_seq = tile_in_seq == 0
    p_scr[:, 0:HALO, :] = jnp.where(starts_seq, 0.0, p_scr[:, ts:ts + HALO, :])
    v_scr[:, 0:HALO, :] = jnp.where(starts_seq, 0.0, v_scr[:, ts:ts + HALO, :])
    for c in range(D_CONV // LANES):
        val = u_scr[:, D_POOL + c * LANES:D_POOL + (c + 1) * LANES]
        gate = u_scr[:, D_POOL + D_CONV + c * LANES:D_POOL + D_CONV + (c + 1) * LANES]
        v_scr[c, HALO:, :] = val * _sigmoid(gate)

    pos = tile_in_seq * ts + lax.broadcasted_iota(jnp.int32, (ts, POOL_GROUP_DIM), 0)
    for g, w in enumerate(POOL_WINDOWS):
        lanes = slice(g * POOL_GROUP_DIM, (g + 1) * POOL_GROUP_DIM)
        cur = u_scr[:, lanes]
        p_scr[g, HALO:, :] = cur
        win = cur
        for back in range(1, w):
            win = win + p_scr[g, HALO - back:HALO - back + ts, :]
        cnt = jnp.minimum(pos + 1, w).astype(F32)
        d = win / cnt - cur
        yg = _dot(d.astype(BF16), pool_w_ref[g]) * pool_scale_ref[:, lanes]
        ypool_scr[:, lanes] = yg.astype(BF16)

    xprev_scr[...] = x

    if n_side:
        @pl.when(t < n_cast_chunks)
        def _():
            loads, stores = cast_copies(t)
            for load in loads:
                load.wait()

            @pl.when(t > 0)
            def _():
                for store in cast_copies(t - 1)[1]:
                    store.wait()

            for src, dst in zip(in_bufs, out_bufs):
                dst[...] = src[...].astype(BF16)
            for store in stores:
                store.start()

            @pl.when(t + 1 < n_cast_chunks)
            def _():
                for load in cast_copies(t + 1)[0]:
                    load.start()

        @pl.when(t == n_cast_chunks)
        def _():
            for store in cast_copies(t - 1)[1]:
                store.wait()


def _const_spec(shape):
    nd = len(shape)
    return pl.BlockSpec(shape, lambda *_: (0,) * nd, pipeline_mode=pl.Buffered(1))


def _per_layer(v):
    return v.reshape(v.shape[0], 1, v.shape[1]) if v.ndim == 2 else v


def _layer_spec(v, layer, **kwargs):
    nd = v.ndim
    return pl.BlockSpec((None,) + v.shape[1:], lambda *_: (layer,) + (0,) * (nd - 1),
                        pipeline_mode=pl.Buffered(1), **kwargs)


def _mixer(alpha, layer, x, w_in, pool_w, pool_scale, conv_w, conv_b, cln_g, cln_b, conv_pw, w_out, ln_g, ln_b,
           side_weights):
    B, S, D = x.shape
    ts = TS_MIX
    n_tiles = B * S // ts
    consts = [_per_layer(c) for c in (w_in, pool_w, pool_scale, conv_w, conv_b, cln_g, cln_b, conv_pw, w_out,
                                      ln_g, ln_b)]
    x_spec = pl.BlockSpec((ts, D), lambda t: (jnp.minimum(t, n_tiles - 1), 0))
    o_spec = pl.BlockSpec((ts, D), lambda t: (jnp.maximum(t - 1, 0), 0))
    o_shape = jax.ShapeDtypeStruct((B * S, D), F32)
    scratch = [pltpu.VMEM((D_POOL // LANES, HALO + ts, LANES), F32),
               pltpu.VMEM((D_CONV // LANES, HALO + ts, LANES), F32),
               pltpu.VMEM((ts, D_IN_PROJ), F32),
               pltpu.VMEM((ts, D_CONV), BF16),
               pltpu.VMEM((ts, D_POOL), BF16),
               pltpu.VMEM((ts, D), F32),
               pltpu.VMEM(w_in.shape[1:], BF16), pltpu.VMEM(pool_w.shape[1:], BF16),
               pltpu.VMEM(conv_pw.shape[1:], BF16), pltpu.VMEM(w_out.shape[1:], BF16)]
    params = pltpu.CompilerParams(dimension_semantics=("arbitrary",), vmem_limit_bytes=VMEM_LIMIT_BYTES)
    in_specs = [x_spec] + [_layer_spec(c, layer) for c in consts]

    mats = [m for m, _ in side_weights]
    cast_layers = tuple(l for _, l in side_weights)
    hbm = pl.BlockSpec(memory_space=pl.ANY)
    bf16_rows = 2 * SUBLANES
    n_chunks = max(n for n in range(1, n_tiles + 1)
                   if all((m.shape[1] * m.shape[2]) % n == 0
                          and m.shape[2] % ((m.shape[1] * m.shape[2]) // n) == 0
                          and ((m.shape[1] * m.shape[2]) // n) % bf16_rows == 0 for m in mats))
    chunk_rows = [m.shape[1] * m.shape[2] // n_chunks for m in mats]
    cast_scratch = ([pltpu.VMEM((rows, m.shape[3]), F32) for m, rows in zip(mats, chunk_rows)]
                    + [pltpu.VMEM((rows, m.shape[3]), BF16) for m, rows in zip(mats, chunk_rows)]
                    + [pltpu.SemaphoreType.DMA((len(mats),)), pltpu.SemaphoreType.DMA((len(mats),))])
    out, *mats_bf16 = pl.pallas_call(
        functools.partial(_mixer_kernel, alpha, S // ts, cast_layers, n_chunks),
        grid=(n_tiles + 1,), in_specs=in_specs + [hbm] * len(mats),
        out_specs=[o_spec] + [hbm] * len(mats),
        out_shape=[o_shape] + [jax.ShapeDtypeStruct(m.shape[1:], BF16) for m in mats],
        scratch_shapes=scratch + cast_scratch, compiler_params=params, name="mixer",
    )(x.reshape(B * S, D), *consts, *mats)
    return out.reshape(B, S, D), mats_bf16


def _norm_and_embed(alpha, x, f, p, ln_g, ln_b, gate_w, ple_w):
    h = _layer_norm(alpha * x + f, ln_g, ln_b)
    gate = _sigmoid(_dot(h.astype(BF16), gate_w))
    return h + gate * _dot(p.astype(BF16), ple_w)


def _dense_ffn_kernel(alpha, x_ref, p_ref, w1_ref, w3_ref, w2_ref, ln_g_ref, ln_b_ref, gate_w_ref, ple_w_ref,
                      o_ref, acc_ref):
    x = x_ref[...]
    xb = x.astype(BF16)
    ff = w1_ref.shape[1]
    for c0 in range(0, ff, FF_CHUNK_DENSE):
        c1 = min(c0 + FF_CHUNK_DENSE, ff)
        a = _dot(xb, w1_ref[:, c0:c1])
        b = _dot(xb, w3_ref[:, c0:c1])
        h = (a * _sigmoid(a) * b).astype(BF16)
        part = _dot(h, w2_ref[c0:c1, :])
        if c0 == 0:
            acc_ref[...] = part
        else:
            acc_ref[...] += part
    o_ref[...] = _norm_and_embed(alpha, x, acc_ref[...], p_ref[...], ln_g_ref[...], ln_b_ref[...],
                                 gate_w_ref[...], ple_w_ref[...])


def _dense_layer(alpha, layer, x, p, w1, w3, w2, ln_g, ln_b, gate_w, ple_w):
    T, D = x.shape
    tm = TM_DENSE
    ln_g, ln_b = _per_layer(ln_g), _per_layer(ln_b)
    return pl.pallas_call(
        functools.partial(_dense_ffn_kernel, alpha),
        grid=(T // tm,),
        in_specs=[pl.BlockSpec((tm, D), lambda i: (i, 0)),
                  pl.BlockSpec((None, tm, D_PLE), lambda i: (layer, i, 0)),
                  _layer_spec(w1, 0), _layer_spec(w3, 0), _layer_spec(w2, 0),
                  _layer_spec(ln_g, layer), _layer_spec(ln_b, layer),
                  _layer_spec(gate_w, 0), _layer_spec(ple_w, 0)],
        out_specs=pl.BlockSpec((tm, D), lambda i: (i, 0)),
        out_shape=jax.ShapeDtypeStruct((T, D), F32),
        scratch_shapes=[pltpu.VMEM((tm, D), F32)],
        compiler_params=pltpu.CompilerParams(
            dimension_semantics=("arbitrary",), vmem_limit_bytes=VMEM_LIMIT_BYTES),
        name="dense_ffn",
    )(x, p, w1, w3, w2, ln_g, ln_b, gate_w, ple_w)


def _router_kernel(x_ref, wt_ref, idx_ref, gate_ref, cnt_ref, carry_ref):
    i = pl.program_id(0)
    tr = x_ref.shape[0]

    @pl.when(i == 0)
    def _():
        carry_ref[...] = jnp.zeros_like(carry_ref)

    x = x_ref[...]
    wt = wt_ref[...]
    xh = x.astype(BF16)
    xl = (x - xh.astype(F32)).astype(BF16)
    wh = wt.astype(BF16)
    wl = (wt - wh.astype(F32)).astype(BF16)
    nt_dims = (((1,), (1,)), ((), ()))
    dg = lambda a, b: lax.dot_general(a, b, nt_dims, preferred_element_type=F32)
    with_xh = dg(jnp.concatenate([wh, wl], axis=0), xh)
    logits = with_xh[:N_EXPERTS] + (dg(wh, xl) + with_xh[N_EXPERTS:])

    eid = lax.broadcasted_iota(jnp.int32, logits.shape, 0)
    m1 = jnp.max(logits, axis=0, keepdims=True)
    i1 = jnp.min(jnp.where(logits == m1, eid, N_EXPERTS), axis=0, keepdims=True)
    rest = jnp.where(eid == i1, -jnp.inf, logits)
    m2 = jnp.max(rest, axis=0, keepdims=True)
    i2 = jnp.min(jnp.where(rest == m2, eid, N_EXPERTS), axis=0, keepdims=True)
    e2 = jnp.exp(m2 - m1)
    g1 = 1.0 / (1.0 + e2)
    g2 = e2 / (1.0 + e2)

    oh1 = (eid == i1).astype(F32)
    oh2 = (eid == i2).astype(F32)
    chosen = oh1 + oh2
    r_i = lax.broadcasted_iota(jnp.int32, (tr, tr), 0)
    c_i = lax.broadcasted_iota(jnp.int32, (tr, tr), 1)
    before = (r_i < c_i).astype(BF16)
    excl = _dot(chosen.astype(BF16), before) + carry_ref[:, 0:1]
    rank1 = jnp.sum(oh1 * excl, axis=0, keepdims=True)
    rank2 = jnp.sum(oh2 * excl, axis=0, keepdims=True)

    total = carry_ref[:, 0:1] + jnp.sum(chosen, axis=1, keepdims=True)
    carry_ref[...] = jnp.broadcast_to(total, carry_ref.shape)
    cnt_ref[...] = jnp.broadcast_to(total, cnt_ref.shape).astype(jnp.int32)

    zi = jnp.zeros((SUBLANES - 4, tr), jnp.int32)
    idx_ref[...] = jnp.concatenate(
        [i1, i2, rank1.astype(jnp.int32), rank2.astype(jnp.int32), zi], axis=0)
    gate_ref[...] = jnp.concatenate([g1, g2, jnp.zeros((SUBLANES - 2, tr), F32)], axis=0)


def _router(x, router_w):
    T, D = x.shape
    tr = TR_ROUTE
    return pl.pallas_call(
        _router_kernel,
        grid=(T // tr,),
        in_specs=[pl.BlockSpec((tr, D), lambda i: (i, 0)), _const_spec((N_EXPERTS, D))],
        out_specs=[pl.BlockSpec((SUBLANES, tr), lambda i: (0, i)),
                   pl.BlockSpec((SUBLANES, tr), lambda i: (0, i)),
                   pl.BlockSpec((N_EXPERTS, LANES), lambda i: (0, 0))],
        out_shape=[jax.ShapeDtypeStruct((SUBLANES, T), jnp.int32),
                   jax.ShapeDtypeStruct((SUBLANES, T), F32),
                   jax.ShapeDtypeStruct((N_EXPERTS, LANES), jnp.int32)],
        scratch_shapes=[pltpu.VMEM((N_EXPERTS, LANES), F32)],
        compiler_params=pltpu.CompilerParams(
            dimension_semantics=("arbitrary",), vmem_limit_bytes=VMEM_LIMIT_BYTES),
        name="router",
    )(x, router_w.T)


def _store_token_tiles(dst_ref, value):
    rows = value.shape[0]
    for c in range(ROW_CHUNKS):
        dst_ref[pl.ds(c, rows, stride=ROW_CHUNKS), :] = value[:, c * LANES:(c + 1) * LANES]


def _load_token_tile_chunk(src_ref, rows, c):
    return src_ref[pl.ds(c, rows, stride=ROW_CHUNKS), :]


def _load_token_tiles(src_ref, rows):
    return jnp.concatenate([_load_token_tile_chunk(src_ref, rows, c) for c in range(ROW_CHUNKS)], axis=-1)


def _tile_rows(row):
    return pl.ds(pl.multiple_of(row * ROW_CHUNKS, ROW_CHUNKS), ROW_CHUNKS)


def _dispatch_kernel(dest_ref, fill_ref, x_ref, xs_hbm, xt_ref, zero_ref, sems, fill_sem):
    i = pl.program_id(0)
    n = pl.num_programs(0)
    tc = x_ref.shape[0]
    n_tok = n * tc
    base = i * tc
    slot = i % 2
    stage = xt_ref.at[slot]

    @pl.when(i == 0)
    def _():
        zero_ref[...] = jnp.zeros_like(zero_ref)
        tile_rows = zero_ref.shape[0]

        def fill(j):
            start = pl.multiple_of(fill_ref[j] * tile_rows, tile_rows)
            return pltpu.make_async_copy(zero_ref, xs_hbm.at[pl.ds(start, tile_rows), :], fill_sem)

        for j in range(fill_ref.shape[0]):
            @pl.when(fill_ref[j] >= 0)
            def _():
                fill(j).start()

        for j in range(fill_ref.shape[0]):
            @pl.when(fill_ref[j] >= 0)
            def _():
                fill(j).wait()

    _store_token_tiles(stage, x_ref[...])

    def copy(k, r):
        d = dest_ref[k * n_tok + base + r]
        return pltpu.make_async_copy(stage.at[_tile_rows(r), :], xs_hbm.at[_tile_rows(d), :], sems.at[slot])

    def issue(rb, c):
        for j in range(ISSUE_UNROLL):
            r = rb * ISSUE_UNROLL + j
            for k in range(TOP_K):
                copy(k, r).start(priority=k)
        return c

    lax.fori_loop(0, tc // ISSUE_UNROLL, issue, 0)

    def wait_step(s):
        for _ in range(TOP_K):
            pltpu.make_async_copy(xt_ref.at[s], xs_hbm.at[pl.ds(0, tc * ROW_CHUNKS), :], sems.at[s]).wait()

    @pl.when(i > 0)
    def _():
        wait_step(1 - slot)

    @pl.when(i == n - 1)
    def _():
        wait_step(slot)


def _dispatch(x, dest, fill_tiles, n_rows):
    T, D = x.shape
    tc = TC_DISPATCH
    return pl.pallas_call(
        _dispatch_kernel,
        grid_spec=pltpu.PrefetchScalarGridSpec(
            num_scalar_prefetch=2, grid=(T // tc,),
            in_specs=[pl.BlockSpec((tc, D), lambda i, d, ft: (i, 0))],
            out_specs=pl.BlockSpec(memory_space=pl.ANY),
            scratch_shapes=[pltpu.VMEM((2, tc * ROW_CHUNKS, LANES), F32),
                            pltpu.VMEM((TM_GROUP * ROW_CHUNKS, LANES), F32),
                            pltpu.SemaphoreType.DMA((2,)), pltpu.SemaphoreType.DMA(())]),
        out_shape=jax.ShapeDtypeStruct((n_rows * ROW_CHUNKS, LANES), F32),
        compiler_params=pltpu.CompilerParams(dimension_semantics=("arbitrary",)),
        name="dispatch",
    )(dest, fill_tiles, x)


def _expert_ffn_kernel(te_ref, valid_ref, nt_ref, xs_ref, w1_ref, w3_ref, w2_ref, o_ref, xb_ref, acc_ref):
    i = pl.program_id(0)
    f = pl.program_id(1)
    nf = pl.num_programs(1)
    tm = acc_ref.shape[0]

    def run(m):
        @pl.when(f == 0)
        def _():
            for c in range(ROW_CHUNKS):
                chunk = _load_token_tile_chunk(xs_ref.at[pl.ds(0, m * ROW_CHUNKS), :], m, c)
                xb_ref[0:m, c * LANES:(c + 1) * LANES] = chunk.astype(BF16)

        xb = xb_ref[0:m, :]
        a = _dot(xb, w1_ref[...])
        b = _dot(xb, w3_ref[...])
        h = (a * _sigmoid(a) * b).astype(BF16)
        part = _dot(h, w2_ref[...])

        @pl.when(f == 0)
        def _():
            acc_ref[0:m, :] = part

        @pl.when((f > 0) & (f < nf - 1))
        def _():
            acc_ref[0:m, :] += part

        @pl.when(f == nf - 1)
        def _():
            _store_token_tiles(o_ref.at[pl.ds(0, m * ROW_CHUNKS), :], acc_ref[0:m, :] + part)
            if m < tm:
                o_ref[m * ROW_CHUNKS:, :] = jnp.zeros(((tm - m) * ROW_CHUNKS, LANES), F32)

    live = i < nt_ref[0]
    few = valid_ref[i] <= tm // 2

    @pl.when(live & jnp.logical_not(few))
    def _():
        run(tm)

    @pl.when(live & few)
    def _():
        run(tm // 2)

    @pl.when(jnp.logical_not(live) & (f == 0))
    def _():
        o_ref[...] = jnp.zeros_like(o_ref)


def _expert_ffn(xs, te, valid, nt, w1, w3, w2):
    E, D, FF = w1.shape
    tm, cf = TM_GROUP, FF_CHUNK_MOE
    nf = FF // cf
    max_tiles = xs.shape[0] // (tm * ROW_CHUNKS)

    def tile(i, nt_ref):
        return jnp.minimum(i, nt_ref[0] - 1)

    def chunk(i, f, nt_ref):
        return jnp.where(i < nt_ref[0], f, nf - 1)

    def w13_map(i, f, te, valid, nt):
        return (te[tile(i, nt)], 0, chunk(i, f, nt))

    def w2_map(i, f, te, valid, nt):
        return (te[tile(i, nt)], chunk(i, f, nt), 0)

    return pl.pallas_call(
        _expert_ffn_kernel,
        grid_spec=pltpu.PrefetchScalarGridSpec(
            num_scalar_prefetch=3, grid=(max_tiles, nf),
            in_specs=[pl.BlockSpec((tm * ROW_CHUNKS, LANES), lambda i, f, te, valid, nt: (tile(i, nt), 0)),
                      pl.BlockSpec((None, D, cf), w13_map),
                      pl.BlockSpec((None, D, cf), w13_map),
                      pl.BlockSpec((None, cf, D), w2_map)],
            out_specs=pl.BlockSpec((tm * ROW_CHUNKS, LANES), lambda i, f, te, valid, nt: (i, 0)),
            scratch_shapes=[pltpu.VMEM((tm, D), BF16), pltpu.VMEM((tm, D), F32)]),
        out_shape=jax.ShapeDtypeStruct(xs.shape, F32),
        compiler_params=pltpu.CompilerParams(
            dimension_semantics=("arbitrary", "arbitrary"), vmem_limit_bytes=VMEM_LIMIT_BYTES),
        name="expert_ffn",
    )(te, valid, nt, xs, w1, w3, w2)


def _sorted_layout(cnt, n_assign):
    tm = TM_GROUP
    max_tiles = n_assign // tm + N_EXPERTS
    tiles_e = (cnt + tm - 1) // tm
    tile_end = jnp.cumsum(tiles_e)
    row_start = (tile_end - tiles_e) * tm
    nt = tile_end[-1]
    tile_ids = jnp.minimum(jnp.arange(max_tiles, dtype=jnp.int32), nt - 1)
    te = jnp.sum((tile_end[None, :] <= tile_ids[:, None]).astype(jnp.int32), axis=1)
    te = jnp.minimum(te, N_EXPERTS - 1)
    owner = te[:, None] == jnp.arange(N_EXPERTS, dtype=jnp.int32)[None, :]
    rows_left = cnt[None, :] - (tile_ids[:, None] - (tile_end - tiles_e)[None, :]) * tm
    valid = jnp.clip(jnp.sum(jnp.where(owner, rows_left, 0), axis=1), 0, tm).astype(jnp.int32)
    last_tile = jnp.where(tiles_e > 0, tile_end - 1, -1)
    tail = nt + jnp.arange(N_EXPERTS, dtype=jnp.int32)
    tail = jnp.where(tail < max_tiles, tail, -1)
    fill_tiles = jnp.concatenate([last_tile, tail]).astype(jnp.int32)
    return row_start, te, valid, nt.reshape(1).astype(jnp.int32), fill_tiles, max_tiles * tm


def _combine_kernel(alpha, dest_ref, x_ref, p_ref, gate_ref, ys_hbm, ln_g_ref, ln_b_ref, gate_w_f32, ple_w_f32,
                    o_ref, yb_ref, gate_w_ref, ple_w_ref, sems):
    i = pl.program_id(0)
    n = pl.num_programs(0)
    tc = x_ref.shape[0]
    n_tok = n * tc
    slot = i % 2

    @pl.when(i == 0)
    def _():
        gate_w_ref[...] = gate_w_f32[...].astype(BF16)
        ple_w_ref[...] = ple_w_f32[...].astype(BF16)

    def issue_step(step, s):
        base = step * tc

        def issue(rb, c):
            for j in range(ISSUE_UNROLL):
                r = rb * ISSUE_UNROLL + j
                for k in range(TOP_K):
                    d = dest_ref[k * n_tok + base + r]
                    pltpu.make_async_copy(ys_hbm.at[_tile_rows(d), :], yb_ref.at[s, k, _tile_rows(r), :],
                                          sems.at[s]).start()
            return c

        lax.fori_loop(0, tc // ISSUE_UNROLL, issue, 0)

    @pl.when(i == 0)
    def _():
        issue_step(0, 0)

    @pl.when(i + 1 < n)
    def _():
        issue_step(i + 1, 1 - slot)

    for k in range(TOP_K):
        pltpu.make_async_copy(ys_hbm.at[pl.ds(0, tc * ROW_CHUNKS), :], yb_ref.at[slot, k], sems.at[slot]).wait()

    g = gate_ref[...]
    f = (g[:, 0:1] * _load_token_tiles(yb_ref.at[slot, 0], tc)
         + g[:, 1:2] * _load_token_tiles(yb_ref.at[slot, 1], tc))
    o_ref[...] = _norm_and_embed(alpha, x_ref[...], f, p_ref[...], ln_g_ref[...], ln_b_ref[...],
                                 gate_w_ref[...], ple_w_ref[...])


def _combine_layer(alpha, layer, x, p, gates, dest, ys, ln_g, ln_b, gate_w, ple_w):
    T, D = x.shape
    tc = TC_COMBINE
    consts = [_per_layer(c) for c in (ln_g, ln_b, gate_w, ple_w)]
    return pl.pallas_call(
        functools.partial(_combine_kernel, alpha),
        grid_spec=pltpu.PrefetchScalarGridSpec(
            num_scalar_prefetch=1, grid=(T // tc,),
            in_specs=[pl.BlockSpec((tc, D), lambda i, d: (i, 0)),
                      pl.BlockSpec((None, tc, D_PLE), lambda i, d: (layer, i, 0)),
                      pl.BlockSpec((tc, TOP_K), lambda i, d: (i, 0)),
                      pl.BlockSpec(memory_space=pl.ANY)]
                     + [_layer_spec(c, layer) for c in consts],
            out_specs=pl.BlockSpec((tc, D), lambda i, d: (i, 0)),
            scratch_shapes=[pltpu.VMEM((2, TOP_K, tc * ROW_CHUNKS, LANES), F32),
                            pltpu.VMEM((D, D), BF16), pltpu.VMEM((D_PLE, D), BF16),
                            pltpu.SemaphoreType.DMA((2,))]),
        out_shape=jax.ShapeDtypeStruct((T, D), F32),
        compiler_params=pltpu.CompilerParams(
            dimension_semantics=("arbitrary",), vmem_limit_bytes=VMEM_LIMIT_BYTES),
        name="combine",
    )(dest, x, p, gates, ys, *consts)


def _moe_layer(alpha, layer, x, p, router_w, w1, w3, w2, ln_g, ln_b, gate_w, ple_w):
    T, D = x.shape
    route_i, route_g, counts = _router(x, router_w)
    idx = route_i[0:TOP_K]
    rank = route_i[TOP_K:2 * TOP_K]
    gates = route_g[0:TOP_K].T

    start, te, valid, nt, fill_tiles, n_rows = _sorted_layout(counts[:, 0], TOP_K * T)
    start_of = sum(jnp.where(idx == e, start[e], 0) for e in range(N_EXPERTS))
    dest = (start_of + rank).astype(jnp.int32).reshape(-1)

    xs = _dispatch(x, dest, fill_tiles, n_rows)
    ys = _expert_ffn(xs, te, valid, nt, w1, w3, w2)
    return _combine_layer(alpha, layer, x, p, gates, dest, ys, ln_g, ln_b, gate_w, ple_w)


def kernel(x, p, w_in, pool_w, pool_scale, conv_w, conv_b, conv_ln_g, conv_ln_b, conv_pw, w_out, ln1_g, ln1_b,
           dense_w1, dense_w3, dense_w2, router_w, exp_w1, exp_w3, exp_w2, ln2_g, ln2_b, ple_gate_w, ple_w):
    depth = w_in.shape[0]
    alpha = (2.0 * depth) ** 0.25
    B, S, D = x.shape
    pt = p.reshape(depth, B * S, D_PLE)
    one_group = lambda m: m.reshape(m.shape[0], 1, *m.shape[1:])
    for i in range(depth):
        j = i // 2
        is_dense = i % 2 == 0
        if is_dense:
            side = [(one_group(dense_w1), j), (one_group(dense_w3), j), (one_group(dense_w2), j),
                    (one_group(ple_gate_w), i), (one_group(ple_w), i)]
        else:
            side = [(exp_w1, j), (exp_w3, j), (exp_w2, j)]
        x, side_bf16 = _mixer(alpha, i, x, w_in, pool_w, pool_scale, conv_w, conv_b, conv_ln_g, conv_ln_b,
                              conv_pw, w_out, ln1_g, ln1_b, side)
        xt = x.reshape(B * S, D)
        if is_dense:
            w1b, w3b, w2b, gate_b, ple_b = side_bf16
            xt = _dense_layer(alpha, i, xt, pt, w1b, w3b, w2b, ln2_g, ln2_b, gate_b, ple_b)
        else:
            xt = _moe_layer(alpha, i, xt, pt, router_w[j], *side_bf16, ln2_g, ln2_b, ple_gate_w, ple_w)
        x = xt.reshape(B, S, D)
    return x
```

```python
import functools

import jax
import jax.numpy as jnp
from jax import lax
from jax.experimental import pallas as pl
from jax.experimental.pallas import tpu as pltpu

D_MODEL = 1024
D_PLE = 256
D_POOL = 512
D_CONV = D_MODEL - D_POOL
POOL_WINDOWS = (2, 4, 8, 16)
POOL_GROUP_DIM = D_POOL // len(POOL_WINDOWS)
CONV_WIDTH = 31
D_IN_PROJ = D_POOL + 2 * D_CONV
N_EXPERTS = 8
TOP_K = 2
LN_EPS = 1e-5

F32 = jnp.float32
BF16 = jnp.bfloat16

SUBLANES = 8
LANES = 128
VMEM_LIMIT_BYTES = 56 * 1024 * 1024

HALO = 32
TS_MIX = 512
RB_CONV = 128
IN_PROJ_CHUNK = 256
MIX_ROW_GROUPS = 2
TM_DENSE = 512
FF_CHUNK_DENSE = 1024
TR_ROUTE = 512
TM_GROUP = 512
FF_CHUNK_MOE = 1792
TC_DISPATCH = 512
TC_COMBINE = 256
ROW_CHUNKS = D_MODEL // LANES
ISSUE_UNROLL = 8


def _sigmoid(z):
    return 1.0 / (1.0 + jnp.exp(-z))


def _layer_norm(h, g, b):
    mu = jnp.mean(h, axis=-1, keepdims=True)
    c = h - mu
    var = jnp.mean(c * c, axis=-1, keepdims=True)
    return c * lax.rsqrt(var + LN_EPS) * g + b


def _dot(a, b):
    return jnp.dot(a, b, preferred_element_type=F32)


def _side_cast_copies(layers, q, src_hbm, dst_hbm, in_bufs, out_bufs, in_sems, out_sems):
    loads, stores = [], []
    for a in range(len(src_hbm)):
        rows = in_bufs[a].shape[0]
        per_expert = src_hbm[a].shape[2] // rows
        e = q // per_expert
        r0 = pl.multiple_of((q % per_expert) * rows, rows)
        loads.append(pltpu.make_async_copy(src_hbm[a].at[layers[a], e, pl.ds(r0, rows), :], in_bufs[a],
                                           in_sems.at[a]))
        stores.append(pltpu.make_async_copy(out_bufs[a], dst_hbm[a].at[e, pl.ds(r0, rows), :], out_sems.at[a]))
    return loads, stores


def _mixer_kernel(alpha, tiles_per_seq, cast_layers, n_cast_chunks, x_ref, w_in_f32, pool_w_f32, pool_scale_ref,
                  conv_w_ref, conv_b_ref, cln_g_ref, cln_b_ref, conv_pw_f32, w_out_f32, ln_g_ref, ln_b_ref, *rest):
    n_side = len(cast_layers)
    side_f32, rest = rest[:n_side], rest[n_side:]
    o_ref, side_bf16, rest = rest[0], rest[1:1 + n_side], rest[1 + n_side:]
    (p_scr, v_scr, u_scr, y_scr, ypool_scr, xprev_scr, w_in_ref, pool_w_ref, conv_pw_ref, w_out_ref) = rest[:10]
    cast_scratch = rest[10:]
    ts = x_ref.shape[0]
    t = pl.program_id(0)

    if n_side:
        in_bufs, out_bufs = cast_scratch[:n_side], cast_scratch[n_side:2 * n_side]
        in_sems, out_sems = cast_scratch[2 * n_side:]

        def cast_copies(q):
            return _side_cast_copies(cast_layers, q, side_f32, side_bf16, in_bufs, out_bufs, in_sems, out_sems)

        @pl.when(t == 0)
        def _():
            for load in cast_copies(0)[0]:
                load.start()

    @pl.when(t == 0)
    def _():
        w_in_ref[...] = w_in_f32[...].astype(BF16)
        pool_w_ref[...] = pool_w_f32[...].astype(BF16)
        conv_pw_ref[...] = conv_pw_f32[...].astype(BF16)
        w_out_ref[...] = w_out_f32[...].astype(BF16)
        p_scr[...] = jnp.zeros_like(p_scr)
        v_scr[...] = jnp.zeros_like(v_scr)
        ypool_scr[...] = jnp.zeros_like(ypool_scr)
        xprev_scr[...] = jnp.zeros_like(xprev_scr)

    x = x_ref[...]
    xb = x.astype(BF16)
    n_conv_blocks = ts // RB_CONV
    in_proj_chunks = [(c0, min(c0 + IN_PROJ_CHUNK, D_IN_PROJ)) for c0 in range(0, D_IN_PROJ, IN_PROJ_CHUNK)]

    blocks_per_group = n_conv_blocks // MIX_ROW_GROUPS
    for rb in range(n_conv_blocks):
        r0 = rb * RB_CONV
        cols = []
        for lc in range(D_CONV // LANES):
            lanes = slice(lc * LANES, (lc + 1) * LANES)
            acc = jnp.zeros((RB_CONV, LANES), F32)
            for k in range(CONV_WIDTH):
                first = r0 + HALO - (CONV_WIDTH - 1 - k)
                acc = acc + v_scr[lc, first:first + RB_CONV, :] * conv_w_ref[k:k + 1, lanes]
            cols.append(acc)
        y = jnp.concatenate(cols, axis=-1) + conv_b_ref[...]
        z = _layer_norm(y, cln_g_ref[...], cln_b_ref[...])
        y_scr[r0:r0 + RB_CONV, :] = (z * _sigmoid(z)).astype(BF16)
        if rb < len(in_proj_chunks):
            c0, c1 = in_proj_chunks[rb]
            u_scr[:, c0:c1] = _dot(xb, w_in_ref[:, c0:c1])
        if (rb + 1) % blocks_per_group == 0:
            rows = slice((rb + 1 - blocks_per_group) * RB_CONV, (rb + 1) * RB_CONV)
            y_conv = _dot(y_scr[rows, :], conv_pw_ref[...])
            heads = jnp.concatenate([ypool_scr[rows, :], y_conv.astype(BF16)], axis=-1)
            mix = _dot(heads, w_out_ref[...])
            o_ref[rows, :] = _layer_norm(alpha * xprev_scr[rows, :] + mix, ln_g_ref[...], ln_b_ref[...])
    for c0, c1 in in_proj_chunks[n_conv_blocks:]:
        u_scr[:, c0:c1] = _dot(xb, w_in_ref[:, c0:c1])

    tile_in_seq = t % tiles_per_seq
    starts_seq = tile_in_seq == 0
    p_scr[:, 0:HALO, :] = jnp.where(starts_seq, 0.0, p_scr[:, ts:ts + HALO, :])
    v_scr[:, 0:HALO, :] = jnp.where(starts_seq, 0.0, v_scr[:, ts:ts + HALO, :])
    for c in range(D_CONV // LANES):
        val = u_scr[:, D_POOL + c * LANES:D_POOL + (c + 1) * LANES]
        gate = u_scr[:, D_POOL + D_CONV + c * LANES:D_POOL + D_CONV + (c + 1) * LANES]
        v_scr[c, HALO:, :] = val * _sigmoid(gate)

    pos = tile_in_seq * ts + lax.broadcasted_iota(jnp.int32, (ts, POOL_GROUP_DIM), 0)
    for g, w in enumerate(POOL_WINDOWS):
        lanes = slice(g * POOL_GROUP_DIM, (g + 1) * POOL_GROUP_DIM)
        cur = u_scr[:, lanes]
        p_scr[g, HALO:, :] = cur
        win = cur
        for back in range(1, w):
            win = win + p_scr[g, HALO - back:HALO - back + ts, :]
        cnt = jnp.minimum(pos + 1, w).astype(F32)
        d = win / cnt - cur
        yg = _dot(d.astype(BF16), pool_w_ref[g]) * pool_scale_ref[:, lanes]
        ypool_scr[:, lanes] = yg.astype(BF16)

    xprev_scr[...] = x

    if n_side:
        @pl.when(t < n_cast_chunks)
        def _():
            loads, stores = cast_copies(t)
            for load in loads:
                load.wait()

            @pl.when(t > 0)
            def _():
                for store in cast_copies(t - 1)[1]:
                    store.wait()

            for src, dst in zip(in_bufs, out_bufs):
                dst[...] = src[...].astype(BF16)
            for store in stores:
                store.start()

            @pl.when(t + 1 < n_cast_chunks)
            def _():
                for load in cast_copies(t + 1)[0]:
                    load.start()

        @pl.when(t == n_cast_chunks)
        def _():
            for store in cast_copies(t - 1)[1]:
                store.wait()


def _const_spec(shape):
    nd = len(shape)
    return pl.BlockSpec(shape, lambda *_: (0,) * nd, pipeline_mode=pl.Buffered(1))


def _per_layer(v):
    return v.reshape(v.shape[0], 1, v.shape[1]) if v.ndim == 2 else v


def _layer_spec(v, layer, **kwargs):
    nd = v.ndim
    return pl.BlockSpec((None,) + v.shape[1:], lambda *_: (layer,) + (0,) * (nd - 1),
                        pipeline_mode=pl.Buffered(1), **kwargs)


def _mixer(alpha, layer, x, w_in, pool_w, pool_scale, conv_w, conv_b, cln_g, cln_b, conv_pw, w_out, ln_g, ln_b,
           side_weights):
    B, S, D = x.shape
    ts = TS_MIX
    n_tiles = B * S // ts
    consts = [_per_layer(c) for c in (w_in, pool_w, pool_scale, conv_w, conv_b, cln_g, cln_b, conv_pw, w_out,
                                      ln_g, ln_b)]
    x_spec = pl.BlockSpec((ts, D), lambda t: (jnp.minimum(t, n_tiles - 1), 0))
    o_spec = pl.BlockSpec((ts, D), lambda t: (jnp.maximum(t - 1, 0), 0))
    o_shape = jax.ShapeDtypeStruct((B * S, D), F32)
    scratch = [pltpu.VMEM((D_POOL // LANES, HALO + ts, LANES), F32),
               pltpu.VMEM((D_CONV // LANES, HALO + ts, LANES), F32),
               pltpu.VMEM((ts, D_IN_PROJ), F32),
               pltpu.VMEM((ts, D_CONV), BF16),
               pltpu.VMEM((ts, D_POOL), BF16),
               pltpu.VMEM((ts, D), F32),
               pltpu.VMEM(w_in.shape[1:], BF16), pltpu.VMEM(pool_w.shape[1:], BF16),
               pltpu.VMEM(conv_pw.shape[1:], BF16), pltpu.VMEM(w_out.shape[1:], BF16)]
    params = pltpu.CompilerParams(dimension_semantics=("arbitrary",), vmem_limit_bytes=VMEM_LIMIT_BYTES)
    in_specs = [x_spec] + [_layer_spec(c, layer) for c in consts]

    mats = [m for m, _ in side_weights]
    cast_layers = tuple(l for _, l in side_weights)
    hbm = pl.BlockSpec(memory_space=pl.ANY)
    bf16_rows = 2 * SUBLANES
    n_chunks = max(n for n in range(1, n_tiles + 1)
                   if all((m.shape[1] * m.shape[2]) % n == 0
                          and m.shape[2] % ((m.shape[1] * m.shape[2]) // n) == 0
                          and ((m.shape[1] * m.shape[2]) // n) % bf16_rows == 0 for m in mats))
    chunk_rows = [m.shape[1] * m.shape[2] // n_chunks for m in mats]
    cast_scratch = ([pltpu.VMEM((rows, m.shape[3]), F32) for m, rows in zip(mats, chunk_rows)]
                    + [pltpu.VMEM((rows, m.shape[3]), BF16) for m, rows in zip(mats, chunk_rows)]
                    + [pltpu.SemaphoreType.DMA((len(mats),)), pltpu.SemaphoreType.DMA((len(mats),))])
    out, *mats_bf16 = pl.pallas_call(
        functools.partial(_mixer_kernel, alpha, S // ts, cast_layers, n_chunks),
        grid=(n_tiles + 1,), in_specs=in_specs + [hbm] * len(mats),
        out_specs=[o_spec] + [hbm] * len(mats),
        out_shape=[o_shape] + [jax.ShapeDtypeStruct(m.shape[1:], BF16) for m in mats],
        scratch_shapes=scratch + cast_scratch, compiler_params=params, name="mixer",
    )(x.reshape(B * S, D), *consts, *mats)
    return out.reshape(B, S, D), mats_bf16


def _norm_and_embed(alpha, x, f, p, ln_g, ln_b, gate_w, ple_w):
    h = _layer_norm(alpha * x + f, ln_g, ln_b)
    gate = _sigmoid(_dot(h.astype(BF16), gate_w))
    return h + gate * _dot(p.astype(BF16), ple_w)


def _dense_ffn_kernel(alpha, x_ref, p_ref, w1_ref, w3_ref, w2_ref, ln_g_ref, ln_b_ref, gate_w_ref, ple_w_ref,
                      o_ref, acc_ref):
    x = x_ref[...]
    xb = x.astype(BF16)
    ff = w1_ref.shape[1]
    for c0 in range(0, ff, FF_CHUNK_DENSE):
        c1 = min(c0 + FF_CHUNK_DENSE, ff)
        a = _dot(xb, w1_ref[:, c0:c1])
        b = _dot(xb, w3_ref[:, c0:c1])
        h = (a * _sigmoid(a) * b).astype(BF16)
        part = _dot(h, w2_ref[c0:c1, :])
        if c0 == 0:
            acc_ref[...] = part
        else:
            acc_ref[...] += part
    o_ref[...] = _norm_and_embed(alpha, x, acc_ref[...], p_ref[...], ln_g_ref[...], ln_b_ref[...],
                                 gate_w_ref[...], ple_w_ref[...])


def _dense_layer(alpha, layer, x, p, w1, w3, w2, ln_g, ln_b, gate_w, ple_w):
    T, D = x.shape
    tm = TM_DENSE
    ln_g, ln_b = _per_layer(ln_g), _per_layer(ln_b)
    return pl.pallas_call(
        functools.partial(_dense_ffn_kernel, alpha),
        grid=(T // tm,),
        in_specs=[pl.BlockSpec((tm, D), lambda i: (i, 0)),
                  pl.BlockSpec((None, tm, D_PLE), lambda i: (layer, i, 0)),
                  _layer_spec(w1, 0), _layer_spec(w3, 0), _layer_spec(w2, 0),
                  _layer_spec(ln_g, layer), _layer_spec(ln_b, layer),
                  _layer_spec(gate_w, 0), _layer_spec(ple_w, 0)],
        out_specs=pl.BlockSpec((tm, D), lambda i: (i, 0)),
        out_shape=jax.ShapeDtypeStruct((T, D), F32),
        scratch_shapes=[pltpu.VMEM((tm, D), F32)],
        compiler_params=pltpu.CompilerParams(
            dimension_semantics=("arbitrary",), vmem_limit_bytes=VMEM_LIMIT_BYTES),
        name="dense_ffn",
    )(x, p, w1, w3, w2, ln_g, ln_b, gate_w, ple_w)


def _router_kernel(x_ref, wt_ref, idx_ref, gate_ref, cnt_ref, carry_ref):
    i = pl.program_id(0)
    tr = x_ref.shape[0]

    @pl.when(i == 0)
    def _():
        carry_ref[...] = jnp.zeros_like(carry_ref)

    x = x_ref[...]
    wt = wt_ref[...]
    xh = x.astype(BF16)
    xl = (x - xh.astype(F32)).astype(BF16)
    wh = wt.astype(BF16)
    wl = (wt - wh.astype(F32)).astype(BF16)
    nt_dims = (((1,), (1,)), ((), ()))
    dg = lambda a, b: lax.dot_general(a, b, nt_dims, preferred_element_type=F32)
    with_xh = dg(jnp.concatenate([wh, wl], axis=0), xh)
    logits = with_xh[:N_EXPERTS] + (dg(wh, xl) + with_xh[N_EXPERTS:])

    eid = lax.broadcasted_iota(jnp.int32, logits.shape, 0)
    m1 = jnp.max(logits, axis=0, keepdims=True)
    i1 = jnp.min(jnp.where(logits == m1, eid, N_EXPERTS), axis=0, keepdims=True)
    rest = jnp.where(eid == i1, -jnp.inf, logits)
    m2 = jnp.max(rest, axis=0, keepdims=True)
    i2 = jnp.min(jnp.where(rest == m2, eid, N_EXPERTS), axis=0, keepdims=True)
    e2 = jnp.exp(m2 - m1)
    g1 = 1.0 / (1.0 + e2)
    g2 = e2 / (1.0 + e2)

    oh1 = (eid == i1).astype(F32)
    oh2 = (eid == i2).astype(F32)
    chosen = oh1 + oh2
    r_i = lax.broadcasted_iota(jnp.int32, (tr, tr), 0)
    c_i = lax.broadcasted_iota(jnp.int32, (tr, tr), 1)
    before = (r_i < c_i).astype(BF16)
    excl = _dot(chosen.astype(BF16), before) + carry_ref[:, 0:1]
    rank1 = jnp.sum(oh1 * excl, axis=0, keepdims=True)
    rank2 = jnp.sum(oh2 * excl, axis=0, keepdims=True)

    total = carry_ref[:, 0:1] + jnp.sum(chosen, axis=1, keepdims=True)
    carry_ref[...] = jnp.broadcast_to(total, carry_ref.shape)
    cnt_ref[...] = jnp.broadcast_to(total, cnt_ref.shape).astype(jnp.int32)

    zi = jnp.zeros((SUBLANES - 4, tr), jnp.int32)
    idx_ref[...] = jnp.concatenate(
        [i1, i2, rank1.astype(jnp.int32), rank2.astype(jnp.int32), zi], axis=0)
    gate_ref[...] = jnp.concatenate([g1, g2, jnp.zeros((SUBLANES - 2, tr), F32)], axis=0)


def _router(x, router_w):
    T, D = x.shape
    tr = TR_ROUTE
    return pl.pallas_call(
        _router_kernel,
        grid=(T // tr,),
        in_specs=[pl.BlockSpec((tr, D), lambda i: (i, 0)), _const_spec((N_EXPERTS, D))],
        out_specs=[pl.BlockSpec((SUBLANES, tr), lambda i: (0, i)),
                   pl.BlockSpec((SUBLANES, tr), lambda i: (0, i)),
                   pl.BlockSpec((N_EXPERTS, LANES), lambda i: (0, 0))],
        out_shape=[jax.ShapeDtypeStruct((SUBLANES, T), jnp.int32),
                   jax.ShapeDtypeStruct((SUBLANES, T), F32),
                   jax.ShapeDtypeStruct((N_EXPERTS, LANES), jnp.int32)],
        scratch_shapes=[pltpu.VMEM((N_EXPERTS, LANES), F32)],
        compiler_params=pltpu.CompilerParams(
            dimension_semantics=("arbitrary",), vmem_limit_bytes=VMEM_LIMIT_BYTES),
        name="router",
    )(x, router_w.T)


def _store_token_tiles(dst_ref, value):
    rows = value.shape[0]
    for c in range(ROW_CHUNKS):
        dst_ref[pl.ds(c, rows, stride=ROW_CHUNKS), :] = value[:, c * LANES:(c + 1) * LANES]


def _load_token_tile_chunk(src_ref, rows, c):
    return src_ref[pl.ds(c, rows, stride=ROW_CHUNKS), :]


def _load_token_tiles(src_ref, rows):
    return jnp.concatenate([_load_token_tile_chunk(src_ref, rows, c) for c in range(ROW_CHUNKS)], axis=-1)


def _tile_rows(row):
    return pl.ds(pl.multiple_of(row * ROW_CHUNKS, ROW_CHUNKS), ROW_CHUNKS)


def _dispatch_kernel(dest_ref, fill_ref, x_ref, xs_hbm, xt_ref, zero_ref, sems, fill_sem):
    i = pl.program_id(0)
    n = pl.num_programs(0)
    tc = x_ref.shape[0]
    n_tok = n * tc
    base = i * tc
    slot = i % 2
    stage = xt_ref.at[slot]

    @pl.when(i == 0)
    def _():
        zero_ref[...] = jnp.zeros_like(zero_ref)
        tile_rows = zero_ref.shape[0]

        def fill(j):
            start = pl.multiple_of(fill_ref[j] * tile_rows, tile_rows)
            return pltpu.make_async_copy(zero_ref, xs_hbm.at[pl.ds(start, tile_rows), :], fill_sem)

        for j in range(fill_ref.shape[0]):
            @pl.when(fill_ref[j] >= 0)
            def _():
                fill(j).start()

        for j in range(fill_ref.shape[0]):
            @pl.when(fill_ref[j] >= 0)
            def _():
                fill(j).wait()

    _store_token_tiles(stage, x_ref[...])

    def copy(k, r):
        d = dest_ref[k * n_tok + base + r]
        return pltpu.make_async_copy(stage.at[_tile_rows(r), :], xs_hbm.at[_tile_rows(d), :], sems.at[slot])

    def issue(rb, c):
        for j in range(ISSUE_UNROLL):
            r = rb * ISSUE_UNROLL + j
            for k in range(TOP_K):
                copy(k, r).start(priority=k)
        return c

    lax.fori_loop(0, tc // ISSUE_UNROLL, issue, 0)

    def wait_step(s):
        for _ in range(TOP_K):
            pltpu.make_async_copy(xt_ref.at[s], xs_hbm.at[pl.ds(0, tc * ROW_CHUNKS), :], sems.at[s]).wait()

    @pl.when(i > 0)
    def _():
        wait_step(1 - slot)

    @pl.when(i == n - 1)
    def _():
        wait_step(slot)


def _dispatch(x, dest, fill_tiles, n_rows):
    T, D = x.shape
    tc = TC_DISPATCH
    return pl.pallas_call(
        _dispatch_kernel,
        grid_spec=pltpu.PrefetchScalarGridSpec(
            num_scalar_prefetch=2, grid=(T // tc,),
            in_specs=[pl.BlockSpec((tc, D), lambda i, d, ft: (i, 0))],
            out_specs=pl.BlockSpec(memory_space=pl.ANY),
            scratch_shapes=[pltpu.VMEM((2, tc * ROW_CHUNKS, LANES), F32),
                            pltpu.VMEM((TM_GROUP * ROW_CHUNKS, LANES), F32),
                            pltpu.SemaphoreType.DMA((2,)), pltpu.SemaphoreType.DMA(())]),
        out_shape=jax.ShapeDtypeStruct((n_rows * ROW_CHUNKS, LANES), F32),
        compiler_params=pltpu.CompilerParams(dimension_semantics=("arbitrary",)),
        name="dispatch",
    )(dest, fill_tiles, x)


def _expert_ffn_kernel(nf, te_ref, valid_ref, nt_ref, xs_ref, w1_ref, w3_ref, w2_ref, o_ref, xb_ref, acc_ref):
    i = pl.program_id(0)
    f = pl.program_id(1)
    tm = acc_ref.shape[0]

    def run(m, first, last):
        if first:
            for c in range(ROW_CHUNKS):
                chunk = _load_token_tile_chunk(xs_ref.at[pl.ds(0, m * ROW_CHUNKS), :], m, c)
                xb_ref[0:m, c * LANES:(c + 1) * LANES] = chunk.astype(BF16)

        xb = xb_ref[0:m, :]
        a = _dot(xb, w1_ref[...])
        b = _dot(xb, w3_ref[...])
        h = (a * _sigmoid(a) * b).astype(BF16)
        part = _dot(h, w2_ref[...])

        if last:
            total = part if first else acc_ref[0:m, :] + part
            _store_token_tiles(o_ref.at[pl.ds(0, m * ROW_CHUNKS), :], total)
            if m < tm:
                o_ref[m * ROW_CHUNKS:, :] = jnp.zeros(((tm - m) * ROW_CHUNKS, LANES), F32)
        elif first:
            acc_ref[0:m, :] = part
        else:
            acc_ref[0:m, :] += part

    live = i < nt_ref[0]
    few = valid_ref[i] <= tm // 2
    for m, size_matches in ((tm, jnp.logical_not(few)), (tm // 2, few)):
        for chunk in range(nf):
            @pl.when(live & size_matches & (f == chunk))
            def _():
                run(m, chunk == 0, chunk == nf - 1)

    @pl.when(jnp.logical_not(live) & (f == 0))
    def _():
        o_ref[...] = jnp.zeros_like(o_ref)


def _expert_ffn(xs, te, valid, nt, w1, w3, w2):
    E, D, FF = w1.shape
    tm, cf = TM_GROUP, FF_CHUNK_MOE
    nf = FF // cf
    max_tiles = xs.shape[0] // (tm * ROW_CHUNKS)

    def tile(i, nt_ref):
        return jnp.minimum(i, nt_ref[0] - 1)

    def chunk(i, f, nt_ref):
        return jnp.where(i < nt_ref[0], f, nf - 1)

    def w13_map(i, f, te, valid, nt):
        return (te[tile(i, nt)], 0, chunk(i, f, nt))

    def w2_map(i, f, te, valid, nt):
        return (te[tile(i, nt)], chunk(i, f, nt), 0)

    return pl.pallas_call(
        functools.partial(_expert_ffn_kernel, nf),
        grid_spec=pltpu.PrefetchScalarGridSpec(
            num_scalar_prefetch=3, grid=(max_tiles, nf),
            in_specs=[pl.BlockSpec((tm * ROW_CHUNKS, LANES), lambda i, f, te, valid, nt: (tile(i, nt), 0)),
                      pl.BlockSpec((None, D, cf), w13_map),
                      pl.BlockSpec((None, D, cf), w13_map),
                      pl.BlockSpec((None, cf, D), w2_map)],
            out_specs=pl.BlockSpec((tm * ROW_CHUNKS, LANES), lambda i, f, te, valid, nt: (i, 0)),
            scratch_shapes=[pltpu.VMEM((tm, D), BF16), pltpu.VMEM((tm, D), F32)]),
        out_shape=jax.ShapeDtypeStruct(xs.shape, F32),
        compiler_params=pltpu.CompilerParams(
            dimension_semantics=("arbitrary", "arbitrary"), vmem_limit_bytes=VMEM_LIMIT_BYTES),
        name="expert_ffn",
    )(te, valid, nt, xs, w1, w3, w2)


def _sorted_layout(cnt, n_assign):
    tm = TM_GROUP
    max_tiles = n_assign // tm + N_EXPERTS
    tiles_e = (cnt + tm - 1) // tm
    tile_end = jnp.cumsum(tiles_e)
    row_start = (tile_end - tiles_e) * tm
    nt = tile_end[-1]
    tile_ids = jnp.minimum(jnp.arange(max_tiles, dtype=jnp.int32), nt - 1)
    te = jnp.sum((tile_end[None, :] <= tile_ids[:, None]).astype(jnp.int32), axis=1)
    te = jnp.minimum(te, N_EXPERTS - 1)
    owner = te[:, None] == jnp.arange(N_EXPERTS, dtype=jnp.int32)[None, :]
    rows_left = cnt[None, :] - (tile_ids[:, None] - (tile_end - tiles_e)[None, :]) * tm
    valid = jnp.clip(jnp.sum(jnp.where(owner, rows_left, 0), axis=1), 0, tm).astype(jnp.int32)
    last_tile = jnp.where(tiles_e > 0, tile_end - 1, -1)
    tail = nt + jnp.arange(N_EXPERTS, dtype=jnp.int32)
    tail = jnp.where(tail < max_tiles, tail, -1)
    fill_tiles = jnp.concatenate([last_tile, tail]).astype(jnp.int32)
    return row_start, te, valid, nt.reshape(1).astype(jnp.int32), fill_tiles, max_tiles * tm


def _combine_kernel(alpha, dest_ref, x_ref, p_ref, gate_ref, ys_hbm, ln_g_ref, ln_b_ref, gate_w_f32, ple_w_f32,
                    o_ref, yb_ref, gate_w_ref, ple_w_ref, sems):
    i = pl.program_id(0)
    n = pl.num_programs(0)
    tc = x_ref.shape[0]
    n_tok = n * tc
    slot = i % 2

    @pl.when(i == 0)
    def _():
        gate_w_ref[...] = gate_w_f32[...].astype(BF16)
        ple_w_ref[...] = ple_w_f32[...].astype(BF16)

    def issue_step(step, s):
        base = step * tc

        def issue(rb, c):
            for j in range(ISSUE_UNROLL):
                r = rb * ISSUE_UNROLL + j
                for k in range(TOP_K):
                    d = dest_ref[k * n_tok + base + r]
                    pltpu.make_async_copy(ys_hbm.at[_tile_rows(d), :], yb_ref.at[s, k, _tile_rows(r), :],
                                          sems.at[s]).start()
            return c

        lax.fori_loop(0, tc // ISSUE_UNROLL, issue, 0)

    @pl.when(i == 0)
    def _():
        issue_step(0, 0)

    @pl.when(i + 1 < n)
    def _():
        issue_step(i + 1, 1 - slot)

    for k in range(TOP_K):
        pltpu.make_async_copy(ys_hbm.at[pl.ds(0, tc * ROW_CHUNKS), :], yb_ref.at[slot, k], sems.at[slot]).wait()

    g = gate_ref[...]
    f = (g[:, 0:1] * _load_token_tiles(yb_ref.at[slot, 0], tc)
         + g[:, 1:2] * _load_token_tiles(yb_ref.at[slot, 1], tc))
    o_ref[...] = _norm_and_embed(alpha, x_ref[...], f, p_ref[...], ln_g_ref[...], ln_b_ref[...],
                                 gate_w_ref[...], ple_w_ref[...])


def _combine_layer(alpha, layer, x, p, gates, dest, ys, ln_g, ln_b, gate_w, ple_w):
    T, D = x.shape
    tc = TC_COMBINE
    consts = [_per_layer(c) for c in (ln_g, ln_b, gate_w, ple_w)]
    return pl.pallas_call(
        functools.partial(_combine_kernel, alpha),
        grid_spec=pltpu.PrefetchScalarGridSpec(
            num_scalar_prefetch=1, grid=(T // tc,),
            in_specs=[pl.BlockSpec((tc, D), lambda i, d: (i, 0)),
                      pl.BlockSpec((None, tc, D_PLE), lambda i, d: (layer, i, 0)),
                      pl.BlockSpec((tc, TOP_K), lambda i, d: (i, 0)),
                      pl.BlockSpec(memory_space=pl.ANY)]
                     + [_layer_spec(c, layer) for c in consts],
            out_specs=pl.BlockSpec((tc, D), lambda i, d: (i, 0)),
            scratch_shapes=[pltpu.VMEM((2, TOP_K, tc * ROW_CHUNKS, LANES), F32),
                            pltpu.VMEM((D, D), BF16), pltpu.VMEM((D_PLE, D), BF16),
                            pltpu.SemaphoreType.DMA((2,))]),
        out_shape=jax.ShapeDtypeStruct((T, D), F32),
        compiler_params=pltpu.CompilerParams(
            dimension_semantics=("arbitrary",), vmem_limit_bytes=VMEM_LIMIT_BYTES),
        name="combine",
    )(dest, x, p, gates, ys, *consts)


def _moe_layer(alpha, layer, x, p, router_w, w1, w3, w2, ln_g, ln_b, gate_w, ple_w):
    T, D = x.shape
    route_i, route_g, counts = _router(x, router_w)
    idx = route_i[0:TOP_K]
    rank = route_i[TOP_K:2 * TOP_K]
    gates = route_g[0:TOP_K].T

    start, te, valid, nt, fill_tiles, n_rows = _sorted_layout(counts[:, 0], TOP_K * T)
    start_of = sum(jnp.where(idx == e, start[e], 0) for e in range(N_EXPERTS))
    dest = (start_of + rank).astype(jnp.int32).reshape(-1)

    xs = _dispatch(x, dest, fill_tiles, n_rows)
    ys = _expert_ffn(xs, te, valid, nt, w1, w3, w2)
    return _combine_layer(alpha, layer, x, p, gates, dest, ys, ln_g, ln_b, gate_w, ple_w)


def kernel(x, p, w_in, pool_w, pool_scale, conv_w, conv_b, conv_ln_g, conv_ln_b, conv_pw, w_out, ln1_g, ln1_b,
           dense_w1, dense_w3, dense_w2, router_w, exp_w1, exp_w3, exp_w2, ln2_g, ln2_b, ple_gate_w, ple_w):
    depth = w_in.shape[0]
    alpha = (2.0 * depth) ** 0.25
    B, S, D = x.shape
    pt = p.reshape(depth, B * S, D_PLE)
    one_group = lambda m: m.reshape(m.shape[0], 1, *m.shape[1:])
    for i in range(depth):
        j = i // 2
        is_dense = i % 2 == 0
        if is_dense:
            side = [(one_group(dense_w1), j), (one_group(dense_w3), j), (one_group(dense_w2), j),
                    (one_group(ple_gate_w), i), (one_group(ple_w), i)]
        else:
            side = [(exp_w1, j), (exp_w3, j), (exp_w2, j)]
        x, side_bf16 = _mixer(alpha, i, x, w_in, pool_w, pool_scale, conv_w, conv_b, conv_ln_g, conv_ln_b,
                              conv_pw, w_out, ln1_g, ln1_b, side)
        xt = x.reshape(B * S, D)
        if is_dense:
            w1b, w3b, w2b, gate_b, ple_b = side_bf16
            xt = _dense_layer(alpha, i, xt, pt, w1b, w3b, w2b, ln2_g, ln2_b, gate_b, ple_b)
        else:
            xt = _moe_layer(alpha, i, xt, pt, router_w[j], *side_bf16, ln2_g, ln2_b, ple_gate_w, ple_w)
        x = xt.reshape(B, S, D)
    return x
```

```python
import functools

import jax
import jax.numpy as jnp
from jax import lax
from jax.experimental import pallas as pl
from jax.experimental.pallas import tpu as pltpu

D_MODEL = 1024
D_PLE = 256
D_POOL = 512
D_CONV = D_MODEL - D_POOL
POOL_WINDOWS = (2, 4, 8, 16)
POOL_GROUP_DIM = D_POOL // len(POOL_WINDOWS)
CONV_WIDTH = 31
D_IN_PROJ = D_POOL + 2 * D_CONV
N_EXPERTS = 8
TOP_K = 2
LN_EPS = 1e-5

F32 = jnp.float32
BF16 = jnp.bfloat16

SUBLANES = 8
LANES = 128
VMEM_LIMIT_BYTES = 56 * 1024 * 1024

HALO = 32
TS_MIX = 512
RB_CONV = 128
IN_PROJ_CHUNK = 256
MIX_ROW_GROUPS = 2
TM_DENSE = 512
FF_CHUNK_DENSE = 1024
TR_ROUTE = 512
TM_GROUP = 512
FF_CHUNK_MOE = 1792
TC_DISPATCH = 512
TC_COMBINE = 256
ROW_CHUNKS = D_MODEL // LANES
ISSUE_UNROLL = 8


def _sigmoid(z):
    return 1.0 / (1.0 + jnp.exp(-z))


def _layer_norm(h, g, b):
    mu = jnp.mean(h, axis=-1, keepdims=True)
    c = h - mu
    var = jnp.mean(c * c, axis=-1, keepdims=True)
    return c * lax.rsqrt(var + LN_EPS) * g + b


def _dot(a, b):
    return jnp.dot(a, b, preferred_element_type=F32)


def _side_cast_copies(layers, q, src_hbm, dst_hbm, in_bufs, out_bufs, in_sems, out_sems):
    loads, stores = [], []
    for a in range(len(src_hbm)):
        rows = in_bufs[a].shape[0]
        per_expert = src_hbm[a].shape[2] // rows
        e = q // per_expert
        r0 = pl.multiple_of((q % per_expert) * rows, rows)
        loads.append(pltpu.make_async_copy(src_hbm[a].at[layers[a], e, pl.ds(r0, rows), :], in_bufs[a],
                                           in_sems.at[a]))
        stores.append(pltpu.make_async_copy(out_bufs[a], dst_hbm[a].at[e, pl.ds(r0, rows), :], out_sems.at[a]))
    return loads, stores


def _mixer_kernel(alpha, tiles_per_seq, cast_layers, n_cast_chunks, x_ref, w_in_f32, pool_w_f32, pool_scale_ref,
                  conv_w_ref, conv_b_ref, cln_g_ref, cln_b_ref, conv_pw_f32, w_out_f32, ln_g_ref, ln_b_ref, *rest):
    n_side = len(cast_layers)
    side_f32, rest = rest[:n_side], rest[n_side:]
    o_ref, side_bf16, rest = rest[0], rest[1:1 + n_side], rest[1 + n_side:]
    (p_scr, v_scr, u_scr, y_scr, ypool_scr, xprev_scr, w_in_ref, pool_w_ref, conv_pw_ref, w_out_ref) = rest[:10]
    cast_scratch = rest[10:]
    ts = x_ref.shape[0]
    t = pl.program_id(0)

    if n_side:
        in_bufs, out_bufs = cast_scratch[:n_side], cast_scratch[n_side:2 * n_side]
        in_sems, out_sems = cast_scratch[2 * n_side:]

        def cast_copies(q):
            return _side_cast_copies(cast_layers, q, side_f32, side_bf16, in_bufs, out_bufs, in_sems, out_sems)

        @pl.when(t == 0)
        def _():
            for load in cast_copies(0)[0]:
                load.start()

    @pl.when(t == 0)
    def _():
        w_in_ref[...] = w_in_f32[...].astype(BF16)
        pool_w_ref[...] = pool_w_f32[...].astype(BF16)
        conv_pw_ref[...] = conv_pw_f32[...].astype(BF16)
        w_out_ref[...] = w_out_f32[...].astype(BF16)
        p_scr[...] = jnp.zeros_like(p_scr)
        v_scr[...] = jnp.zeros_like(v_scr)
        ypool_scr[...] = jnp.zeros_like(ypool_scr)
        xprev_scr[...] = jnp.zeros_like(xprev_scr)

    x = x_ref[...]
    xb = x.astype(BF16)
    n_conv_blocks = ts // RB_CONV
    in_proj_chunks = [(c0, min(c0 + IN_PROJ_CHUNK, D_IN_PROJ)) for c0 in range(0, D_IN_PROJ, IN_PROJ_CHUNK)]

    blocks_per_group = n_conv_blocks // MIX_ROW_GROUPS
    for rb in range(n_conv_blocks):
        r0 = rb * RB_CONV
        cols = []
        for lc in range(D_CONV // LANES):
            lanes = slice(lc * LANES, (lc + 1) * LANES)
            acc = jnp.zeros((RB_CONV, LANES), F32)
            for k in range(CONV_WIDTH):
                first = r0 + HALO - (CONV_WIDTH - 1 - k)
                acc = acc + v_scr[lc, first:first + RB_CONV, :] * conv_w_ref[k:k + 1, lanes]
            cols.append(acc)
        y = jnp.concatenate(cols, axis=-1) + conv_b_ref[...]
        z = _layer_norm(y, cln_g_ref[...], cln_b_ref[...])
        y_scr[r0:r0 + RB_CONV, :] = (z * _sigmoid(z)).astype(BF16)
        if rb < len(in_proj_chunks):
            c0, c1 = in_proj_chunks[rb]
            u_scr[:, c0:c1] = _dot(xb, w_in_ref[:, c0:c1])
        if (rb + 1) % blocks_per_group == 0:
            rows = slice((rb + 1 - blocks_per_group) * RB_CONV, (rb + 1) * RB_CONV)
            y_conv = _dot(y_scr[rows, :], conv_pw_ref[...])
            heads = jnp.concatenate([ypool_scr[rows, :], y_conv.astype(BF16)], axis=-1)
            mix = _dot(heads, w_out_ref[...])
            o_ref[rows, :] = _layer_norm(alpha * xprev_scr[rows, :] + mix, ln_g_ref[...], ln_b_ref[...])
    for c0, c1 in in_proj_chunks[n_conv_blocks:]:
        u_scr[:, c0:c1] = _dot(xb, w_in_ref[:, c0:c1])

    tile_in_seq = t % tiles_per_seq
    starts_seq = tile_in_seq == 0
    p_scr[:, 0:HALO, :] = jnp.where(starts_seq, 0.0, p_scr[:, ts:ts + HALO, :])
    v_scr[:, 0:HALO, :] = jnp.where(starts_seq, 0.0, v_scr[:, ts:ts + HALO, :])
    for c in range(D_CONV // LANES):
        val = u_scr[:, D_POOL + c * LANES:D_POOL + (c + 1) * LANES]
        gate = u_scr[:, D_POOL + D_CONV + c * LANES:D_POOL + D_CONV + (c + 1) * LANES]
        v_scr[c, HALO:, :] = val * _sigmoid(gate)

    pos = tile_in_seq * ts + lax.broadcasted_iota(jnp.int32, (ts, POOL_GROUP_DIM), 0)
    for g, w in enumerate(POOL_WINDOWS):
        lanes = slice(g * POOL_GROUP_DIM, (g + 1) * POOL_GROUP_DIM)
        cur = u_scr[:, lanes]
        p_scr[g, HALO:, :] = cur
        win = cur
        for back in range(1, w):
            win = win + p_scr[g, HALO - back:HALO - back + ts, :]
        cnt = jnp.minimum(pos + 1, w).astype(F32)
        d = win / cnt - cur
        yg = _dot(d.astype(BF16), pool_w_ref[g]) * pool_scale_ref[:, lanes]
        ypool_scr[:, lanes] = yg.astype(BF16)

    xprev_scr[...] = x

    if n_side:
        @pl.when(t < n_cast_chunks)
        def _():
            loads, stores = cast_copies(t)
            for load in loads:
                load.wait()

            @pl.when(t > 0)
            def _():
                for store in cast_copies(t - 1)[1]:
                    store.wait()

            for src, dst in zip(in_bufs, out_bufs):
                dst[...] = src[...].astype(BF16)
            for store in stores:
                store.start()

            @pl.when(t + 1 < n_cast_chunks)
            def _():
                for load in cast_copies(t + 1)[0]:
                    load.start()

        @pl.when(t == n_cast_chunks)
        def _():
            for store in cast_copies(t - 1)[1]:
                store.wait()


def _const_spec(shape):
    nd = len(shape)
    return pl.BlockSpec(shape, lambda *_: (0,) * nd, pipeline_mode=pl.Buffered(1))


def _per_layer(v):
    return v.reshape(v.shape[0], 1, v.shape[1]) if v.ndim == 2 else v


def _layer_spec(v, layer, **kwargs):
    nd = v.ndim
    return pl.BlockSpec((None,) + v.shape[1:], lambda *_: (layer,) + (0,) * (nd - 1),
                        pipeline_mode=pl.Buffered(1), **kwargs)


def _mixer(alpha, layer, x, w_in, pool_w, pool_scale, conv_w, conv_b, cln_g, cln_b, conv_pw, w_out, ln_g, ln_b,
           side_weights):
    B, S, D = x.shape
    ts = TS_MIX
    n_tiles = B * S // ts
    consts = [_per_layer(c) for c in (w_in, pool_w, pool_scale, conv_w, conv_b, cln_g, cln_b, conv_pw, w_out,
                                      ln_g, ln_b)]
    x_spec = pl.BlockSpec((ts, D), lambda t: (jnp.minimum(t, n_tiles - 1), 0))
    o_spec = pl.BlockSpec((ts, D), lambda t: (jnp.maximum(t - 1, 0), 0))
    o_shape = jax.ShapeDtypeStruct((B * S, D), F32)
    scratch = [pltpu.VMEM((D_POOL // LANES, HALO + ts, LANES), F32),
               pltpu.VMEM((D_CONV // LANES, HALO + ts, LANES), F32),
               pltpu.VMEM((ts, D_IN_PROJ), F32),
               pltpu.VMEM((ts, D_CONV), BF16),
               pltpu.VMEM((ts, D_POOL), BF16),
               pltpu.VMEM((ts, D), F32),
               pltpu.VMEM(w_in.shape[1:], BF16), pltpu.VMEM(pool_w.shape[1:], BF16),
               pltpu.VMEM(conv_pw.shape[1:], BF16), pltpu.VMEM(w_out.shape[1:], BF16)]
    params = pltpu.CompilerParams(dimension_semantics=("arbitrary",), vmem_limit_bytes=VMEM_LIMIT_BYTES)
    in_specs = [x_spec] + [_layer_spec(c, layer) for c in consts]

    mats = [m for m, _ in side_weights]
    cast_layers = tuple(l for _, l in side_weights)
    hbm = pl.BlockSpec(memory_space=pl.ANY)
    bf16_rows = 2 * SUBLANES
    n_chunks = max(n for n in range(1, n_tiles + 1)
                   if all((m.shape[1] * m.shape[2]) % n == 0
                          and m.shape[2] % ((m.shape[1] * m.shape[2]) // n) == 0
                          and ((m.shape[1] * m.shape[2]) // n) % bf16_rows == 0 for m in mats))
    chunk_rows = [m.shape[1] * m.shape[2] // n_chunks for m in mats]
    cast_scratch = ([pltpu.VMEM((rows, m.shape[3]), F32) for m, rows in zip(mats, chunk_rows)]
                    + [pltpu.VMEM((rows, m.shape[3]), BF16) for m, rows in zip(mats, chunk_rows)]
                    + [pltpu.SemaphoreType.DMA((len(mats),)), pltpu.SemaphoreType.DMA((len(mats),))])
    out, *mats_bf16 = pl.pallas_call(
        functools.partial(_mixer_kernel, alpha, S // ts, cast_layers, n_chunks),
        grid=(n_tiles + 1,), in_specs=in_specs + [hbm] * len(mats),
        out_specs=[o_spec] + [hbm] * len(mats),
        out_shape=[o_shape] + [jax.ShapeDtypeStruct(m.shape[1:], BF16) for m in mats],
        scratch_shapes=scratch + cast_scratch, compiler_params=params, name="mixer",
    )(x.reshape(B * S, D), *consts, *mats)
    return out.reshape(B, S, D), mats_bf16


def _norm_and_embed(alpha, x, f, p, ln_g, ln_b, gate_w, ple_w):
    h = _layer_norm(alpha * x + f, ln_g, ln_b)
    gate = _sigmoid(_dot(h.astype(BF16), gate_w))
    return h + gate * _dot(p.astype(BF16), ple_w)


def _dense_ffn_kernel(alpha, x_ref, p_ref, w1_ref, w3_ref, w2_ref, ln_g_ref, ln_b_ref, gate_w_ref, ple_w_ref,
                      o_ref, acc_ref):
    x = x_ref[...]
    xb = x.astype(BF16)
    ff = w1_ref.shape[1]
    for c0 in range(0, ff, FF_CHUNK_DENSE):
        c1 = min(c0 + FF_CHUNK_DENSE, ff)
        a = _dot(xb, w1_ref[:, c0:c1])
        b = _dot(xb, w3_ref[:, c0:c1])
        h = (a * _sigmoid(a) * b).astype(BF16)
        part = _dot(h, w2_ref[c0:c1, :])
        if c0 == 0:
            acc_ref[...] = part
        else:
            acc_ref[...] += part
    o_ref[...] = _norm_and_embed(alpha, x, acc_ref[...], p_ref[...], ln_g_ref[...], ln_b_ref[...],
                                 gate_w_ref[...], ple_w_ref[...])


def _dense_layer(alpha, layer, x, p, w1, w3, w2, ln_g, ln_b, gate_w, ple_w):
    T, D = x.shape
    tm = TM_DENSE
    ln_g, ln_b = _per_layer(ln_g), _per_layer(ln_b)
    return pl.pallas_call(
        functools.partial(_dense_ffn_kernel, alpha),
        grid=(T // tm,),
        in_specs=[pl.BlockSpec((tm, D), lambda i: (i, 0)),
                  pl.BlockSpec((None, tm, D_PLE), lambda i: (layer, i, 0)),
                  _layer_spec(w1, 0), _layer_spec(w3, 0), _layer_spec(w2, 0),
                  _layer_spec(ln_g, layer), _layer_spec(ln_b, layer),
                  _layer_spec(gate_w, 0), _layer_spec(ple_w, 0)],
        out_specs=pl.BlockSpec((tm, D), lambda i: (i, 0)),
        out_shape=jax.ShapeDtypeStruct((T, D), F32),
        scratch_shapes=[pltpu.VMEM((tm, D), F32)],
        compiler_params=pltpu.CompilerParams(
            dimension_semantics=("arbitrary",), vmem_limit_bytes=VMEM_LIMIT_BYTES),
        name="dense_ffn",
    )(x, p, w1, w3, w2, ln_g, ln_b, gate_w, ple_w)


def _router_kernel(x_ref, wt_ref, idx_ref, gate_ref, cnt_ref, carry_ref):
    i = pl.program_id(0)
    tr = x_ref.shape[0]

    @pl.when(i == 0)
    def _():
        carry_ref[...] = jnp.zeros_like(carry_ref)

    x = x_ref[...]
    wt = wt_ref[...]
    xh = x.astype(BF16)
    xl = (x - xh.astype(F32)).astype(BF16)
    wh = wt.astype(BF16)
    wl = (wt - wh.astype(F32)).astype(BF16)
    nt_dims = (((1,), (1,)), ((), ()))
    dg = lambda a, b: lax.dot_general(a, b, nt_dims, preferred_element_type=F32)
    with_xh = dg(jnp.concatenate([wh, wl], axis=0), xh)
    logits = with_xh[:N_EXPERTS] + (dg(wh, xl) + with_xh[N_EXPERTS:])

    eid = lax.broadcasted_iota(jnp.int32, logits.shape, 0)
    m1 = jnp.max(logits, axis=0, keepdims=True)
    i1 = jnp.min(jnp.where(logits == m1, eid, N_EXPERTS), axis=0, keepdims=True)
    rest = jnp.where(eid == i1, -jnp.inf, logits)
    m2 = jnp.max(rest, axis=0, keepdims=True)
    i2 = jnp.min(jnp.where(rest == m2, eid, N_EXPERTS), axis=0, keepdims=True)
    e2 = jnp.exp(m2 - m1)
    g1 = 1.0 / (1.0 + e2)
    g2 = e2 / (1.0 + e2)

    oh1 = (eid == i1).astype(F32)
    oh2 = (eid == i2).astype(F32)
    chosen = oh1 + oh2
    r_i = lax.broadcasted_iota(jnp.int32, (tr, tr), 0)
    c_i = lax.broadcasted_iota(jnp.int32, (tr, tr), 1)
    before = (r_i < c_i).astype(BF16)
    excl = _dot(chosen.astype(BF16), before) + carry_ref[:, 0:1]
    rank1 = jnp.sum(oh1 * excl, axis=0, keepdims=True)
    rank2 = jnp.sum(oh2 * excl, axis=0, keepdims=True)

    total = carry_ref[:, 0:1] + jnp.sum(chosen, axis=1, keepdims=True)
    carry_ref[...] = jnp.broadcast_to(total, carry_ref.shape)
    cnt_ref[...] = jnp.broadcast_to(total, cnt_ref.shape).astype(jnp.int32)

    zi = jnp.zeros((SUBLANES - 4, tr), jnp.int32)
    idx_ref[...] = jnp.concatenate(
        [i1, i2, rank1.astype(jnp.int32), rank2.astype(jnp.int32), zi], axis=0)
    gate_ref[...] = jnp.concatenate([g1, g2, jnp.zeros((SUBLANES - 2, tr), F32)], axis=0)


def _router(x, router_w):
    T, D = x.shape
    tr = TR_ROUTE
    return pl.pallas_call(
        _router_kernel,
        grid=(T // tr,),
        in_specs=[pl.BlockSpec((tr, D), lambda i: (i, 0)), _const_spec((N_EXPERTS, D))],
        out_specs=[pl.BlockSpec((SUBLANES, tr), lambda i: (0, i)),
                   pl.BlockSpec((SUBLANES, tr), lambda i: (0, i)),
                   pl.BlockSpec((N_EXPERTS, LANES), lambda i: (0, 0))],
        out_shape=[jax.ShapeDtypeStruct((SUBLANES, T), jnp.int32),
                   jax.ShapeDtypeStruct((SUBLANES, T), F32),
                   jax.ShapeDtypeStruct((N_EXPERTS, LANES), jnp.int32)],
        scratch_shapes=[pltpu.VMEM((N_EXPERTS, LANES), F32)],
        compiler_params=pltpu.CompilerParams(
            dimension_semantics=("arbitrary",), vmem_limit_bytes=VMEM_LIMIT_BYTES),
        name="router",
    )(x, router_w.T)


def _store_token_tiles(dst_ref, value):
    rows = value.shape[0]
    for c in range(ROW_CHUNKS):
        dst_ref[pl.ds(c, rows, stride=ROW_CHUNKS), :] = value[:, c * LANES:(c + 1) * LANES]


def _load_token_tile_chunk(src_ref, rows, c):
    return src_ref[pl.ds(c, rows, stride=ROW_CHUNKS), :]


def _load_token_tiles(src_ref, rows):
    return jnp.concatenate([_load_token_tile_chunk(src_ref, rows, c) for c in range(ROW_CHUNKS)], axis=-1)


def _tile_rows(row):
    return pl.ds(pl.multiple_of(row * ROW_CHUNKS, ROW_CHUNKS), ROW_CHUNKS)


def _dispatch_kernel(dest_ref, fill_ref, x_ref, xs_hbm, xt_ref, zero_ref, sems, fill_sem):
    i = pl.program_id(0)
    n = pl.num_programs(0)
    tc = x_ref.shape[0]
    n_tok = n * tc
    base = i * tc
    slot = i % 2
    stage = xt_ref.at[slot]

    @pl.when(i == 0)
    def _():
        zero_ref[...] = jnp.zeros_like(zero_ref)
        tile_rows = zero_ref.shape[0]

        def fill(j):
            start = pl.multiple_of(fill_ref[j] * tile_rows, tile_rows)
            return pltpu.make_async_copy(zero_ref, xs_hbm.at[pl.ds(start, tile_rows), :], fill_sem)

        for j in range(fill_ref.shape[0]):
            @pl.when(fill_ref[j] >= 0)
            def _():
                fill(j).start()

        for j in range(fill_ref.shape[0]):
            @pl.when(fill_ref[j] >= 0)
            def _():
                fill(j).wait()

    _store_token_tiles(stage, x_ref[...])

    def copy(k, r):
        d = dest_ref[k * n_tok + base + r]
        return pltpu.make_async_copy(stage.at[_tile_rows(r), :], xs_hbm.at[_tile_rows(d), :], sems.at[slot])

    def issue(rb, c):
        for j in range(ISSUE_UNROLL):
            r = rb * ISSUE_UNROLL + j
            for k in range(TOP_K):
                copy(k, r).start(priority=k)
        return c

    lax.fori_loop(0, tc // ISSUE_UNROLL, issue, 0)

    def wait_step(s):
        for _ in range(TOP_K):
            pltpu.make_async_copy(xt_ref.at[s], xs_hbm.at[pl.ds(0, tc * ROW_CHUNKS), :], sems.at[s]).wait()

    @pl.when(i > 0)
    def _():
        wait_step(1 - slot)

    @pl.when(i == n - 1)
    def _():
        wait_step(slot)


def _dispatch(x, dest, fill_tiles, n_rows):
    T, D = x.shape
    tc = TC_DISPATCH
    return pl.pallas_call(
        _dispatch_kernel,
        grid_spec=pltpu.PrefetchScalarGridSpec(
            num_scalar_prefetch=2, grid=(T // tc,),
            in_specs=[pl.BlockSpec((tc, D), lambda i, d, ft: (i, 0))],
            out_specs=pl.BlockSpec(memory_space=pl.ANY),
            scratch_shapes=[pltpu.VMEM((2, tc * ROW_CHUNKS, LANES), F32),
                            pltpu.VMEM((TM_GROUP * ROW_CHUNKS, LANES), F32),
                            pltpu.SemaphoreType.DMA((2,)), pltpu.SemaphoreType.DMA(())]),
        out_shape=jax.ShapeDtypeStruct((n_rows * ROW_CHUNKS, LANES), F32),
        compiler_params=pltpu.CompilerParams(dimension_semantics=("arbitrary",)),
        name="dispatch",
    )(dest, fill_tiles, x)


def _expert_ffn_kernel(nf, te_ref, valid_ref, nt_ref, xs_ref, w1_ref, w3_ref, w2_ref, o_ref, xb_ref, acc_ref):
    i = pl.program_id(0)
    f = pl.program_id(1)
    tm = acc_ref.shape[0]

    def run(m, first, last):
        if first:
            for c in range(ROW_CHUNKS):
                chunk = _load_token_tile_chunk(xs_ref.at[pl.ds(0, m * ROW_CHUNKS), :], m, c)
                xb_ref[0:m, c * LANES:(c + 1) * LANES] = chunk.astype(BF16)

        xb = xb_ref[0:m, :]
        a = _dot(xb, w1_ref[...])
        b = _dot(xb, w3_ref[...])
        h = (a * _sigmoid(a) * b).astype(BF16)
        part = _dot(h, w2_ref[...])

        if last:
            total = part if first else acc_ref[0:m, :] + part
            _store_token_tiles(o_ref.at[pl.ds(0, m * ROW_CHUNKS), :], total)
            if m < tm:
                o_ref[m * ROW_CHUNKS:, :] = jnp.zeros(((tm - m) * ROW_CHUNKS, LANES), F32)
        elif first:
            acc_ref[0:m, :] = part
        else:
            acc_ref[0:m, :] += part

    live = i < nt_ref[0]
    few = valid_ref[i] <= tm // 2
    for m, size_matches in ((tm, jnp.logical_not(few)), (tm // 2, few)):
        for chunk in range(nf):
            @pl.when(live & size_matches & (f == chunk))
            def _():
                run(m, chunk == 0, chunk == nf - 1)

    @pl.when(jnp.logical_not(live) & (f == 0))
    def _():
        o_ref[...] = jnp.zeros_like(o_ref)


def _expert_ffn(xs, te, valid, nt, w1, w3, w2):
    E, D, FF = w1.shape
    tm, cf = TM_GROUP, FF_CHUNK_MOE
    nf = FF // cf
    max_tiles = xs.shape[0] // (tm * ROW_CHUNKS)

    def tile(i, nt_ref):
        return jnp.minimum(i, nt_ref[0] - 1)

    def chunk(i, f, nt_ref):
        return jnp.where(i < nt_ref[0], f, nf - 1)

    def w13_map(i, f, te, valid, nt):
        return (te[tile(i, nt)], 0, chunk(i, f, nt))

    def w2_map(i, f, te, valid, nt):
        return (te[tile(i, nt)], chunk(i, f, nt), 0)

    return pl.pallas_call(
        functools.partial(_expert_ffn_kernel, nf),
        grid_spec=pltpu.PrefetchScalarGridSpec(
            num_scalar_prefetch=3, grid=(max_tiles, nf),
            in_specs=[pl.BlockSpec((tm * ROW_CHUNKS, LANES), lambda i, f, te, valid, nt: (tile(i, nt), 0)),
                      pl.BlockSpec((None, D, cf), w13_map),
                      pl.BlockSpec((None, D, cf), w13_map),
                      pl.BlockSpec((None, cf, D), w2_map)],
            out_specs=pl.BlockSpec((tm * ROW_CHUNKS, LANES), lambda i, f, te, valid, nt: (i, 0)),
            scratch_shapes=[pltpu.VMEM((tm, D), BF16), pltpu.VMEM((tm, D), F32)]),
        out_shape=jax.ShapeDtypeStruct(xs.shape, F32),
        compiler_params=pltpu.CompilerParams(
            dimension_semantics=("arbitrary", "arbitrary"), vmem_limit_bytes=VMEM_LIMIT_BYTES),
        name="expert_ffn",
    )(te, valid, nt, xs, w1, w3, w2)


def _sorted_layout(cnt, n_assign):
    tm = TM_GROUP
    max_tiles = n_assign // tm + N_EXPERTS
    tiles_e = (cnt + tm - 1) // tm
    tile_end = jnp.cumsum(tiles_e)
    row_start = (tile_end - tiles_e) * tm
    nt = tile_end[-1]
    tile_ids = jnp.minimum(jnp.arange(max_tiles, dtype=jnp.int32), nt - 1)
    te = jnp.sum((tile_end[None, :] <= tile_ids[:, None]).astype(jnp.int32), axis=1)
    te = jnp.minimum(te, N_EXPERTS - 1)
    owner = te[:, None] == jnp.arange(N_EXPERTS, dtype=jnp.int32)[None, :]
    rows_left = cnt[None, :] - (tile_ids[:, None] - (tile_end - tiles_e)[None, :]) * tm
    valid = jnp.clip(jnp.sum(jnp.where(owner, rows_left, 0), axis=1), 0, tm).astype(jnp.int32)
    last_tile = jnp.where(tiles_e > 0, tile_end - 1, -1)
    tail = nt + jnp.arange(N_EXPERTS, dtype=jnp.int32)
    tail = jnp.where(tail < max_tiles, tail, -1)
    fill_tiles = jnp.concatenate([last_tile, tail]).astype(jnp.int32)
    return row_start, te, valid, nt.reshape(1).astype(jnp.int32), fill_tiles, max_tiles * tm


def _combine_kernel(alpha, dest_ref, x_ref, p_ref, gate_ref, ys_hbm, ln_g_ref, ln_b_ref, gate_w_f32, ple_w_f32,
                    o_ref, yb_ref, gate_w_ref, ple_w_ref, sems):
    i = pl.program_id(0)
    n = pl.num_programs(0)
    tc = x_ref.shape[0]
    n_tok = n * tc
    slot = i % 2

    @pl.when(i == 0)
    def _():
        gate_w_ref[...] = gate_w_f32[...].astype(BF16)
        ple_w_ref[...] = ple_w_f32[...].astype(BF16)

    def row_copy(step, s, k, r):
        d = dest_ref[k * n_tok + step * tc + r]
        return pltpu.make_async_copy(ys_hbm.at[_tile_rows(d), :], yb_ref.at[s, k, _tile_rows(r), :], sems.at[s])

    def wait_rows(s):
        for k in range(TOP_K):
            pltpu.make_async_copy(ys_hbm.at[pl.ds(0, tc * ROW_CHUNKS), :], yb_ref.at[s, k], sems.at[s]).wait()

    @pl.when(i == 0)
    def _():
        def issue(rb, c):
            for j in range(ISSUE_UNROLL):
                for k in range(TOP_K):
                    row_copy(0, 0, k, rb * ISSUE_UNROLL + j).start()
            return c

        lax.fori_loop(0, tc // ISSUE_UNROLL, issue, 0)

    wait_rows(slot)

    nxt = jnp.minimum(i + 1, n - 1)
    for r in range(tc):
        for k in range(TOP_K):
            row_copy(nxt, 1 - slot, k, r).start()

    g = gate_ref[...]
    f = (g[:, 0:1] * _load_token_tiles(yb_ref.at[slot, 0], tc)
         + g[:, 1:2] * _load_token_tiles(yb_ref.at[slot, 1], tc))
    o_ref[...] = _norm_and_embed(alpha, x_ref[...], f, p_ref[...], ln_g_ref[...], ln_b_ref[...],
                                 gate_w_ref[...], ple_w_ref[...])

    @pl.when(i == n - 1)
    def _():
        wait_rows(1 - slot)


def _combine_layer(alpha, layer, x, p, gates, dest, ys, ln_g, ln_b, gate_w, ple_w):
    T, D = x.shape
    tc = TC_COMBINE
    consts = [_per_layer(c) for c in (ln_g, ln_b, gate_w, ple_w)]
    return pl.pallas_call(
        functools.partial(_combine_kernel, alpha),
        grid_spec=pltpu.PrefetchScalarGridSpec(
            num_scalar_prefetch=1, grid=(T // tc,),
            in_specs=[pl.BlockSpec((tc, D), lambda i, d: (i, 0)),
                      pl.BlockSpec((None, tc, D_PLE), lambda i, d: (layer, i, 0)),
                      pl.BlockSpec((tc, TOP_K), lambda i, d: (i, 0)),
                      pl.BlockSpec(memory_space=pl.ANY)]
                     + [_layer_spec(c, layer) for c in consts],
            out_specs=pl.BlockSpec((tc, D), lambda i, d: (i, 0)),
            scratch_shapes=[pltpu.VMEM((2, TOP_K, tc * ROW_CHUNKS, LANES), F32),
                            pltpu.VMEM((D, D), BF16), pltpu.VMEM((D_PLE, D), BF16),
                            pltpu.SemaphoreType.DMA((2,))]),
        out_shape=jax.ShapeDtypeStruct((T, D), F32),
        compiler_params=pltpu.CompilerParams(
            dimension_semantics=("arbitrary",), vmem_limit_bytes=VMEM_LIMIT_BYTES),
        name="combine",
    )(dest, x, p, gates, ys, *consts)


def _moe_layer(alpha, layer, x, p, router_w, w1, w3, w2, ln_g, ln_b, gate_w, ple_w):
    T, D = x.shape
    route_i, route_g, counts = _router(x, router_w)
    idx = route_i[0:TOP_K]
    rank = route_i[TOP_K:2 * TOP_K]
    gates = route_g[0:TOP_K].T

    start, te, valid, nt, fill_tiles, n_rows = _sorted_layout(counts[:, 0], TOP_K * T)
    start_of = sum(jnp.where(idx == e, start[e], 0) for e in range(N_EXPERTS))
    dest = (start_of + rank).astype(jnp.int32).reshape(-1)

    xs = _dispatch(x, dest, fill_tiles, n_rows)
    ys = _expert_ffn(xs, te, valid, nt, w1, w3, w2)
    return _combine_layer(alpha, layer, x, p, gates, dest, ys, ln_g, ln_b, gate_w, ple_w)


def kernel(x, p, w_in, pool_w, pool_scale, conv_w, conv_b, conv_ln_g, conv_ln_b, conv_pw, w_out, ln1_g, ln1_b,
           dense_w1, dense_w3, dense_w2, router_w, exp_w1, exp_w3, exp_w2, ln2_g, ln2_b, ple_gate_w, ple_w):
    depth = w_in.shape[0]
    alpha = (2.0 * depth) ** 0.25
    B, S, D = x.shape
    pt = p.reshape(depth, B * S, D_PLE)
    one_group = lambda m: m.reshape(m.shape[0], 1, *m.shape[1:])
    for i in range(depth):
        j = i // 2
        is_dense = i % 2 == 0
        if is_dense:
            side = [(one_group(dense_w1), j), (one_group(dense_w3), j), (one_group(dense_w2), j),
                    (one_group(ple_gate_w), i), (one_group(ple_w), i)]
        else:
            side = [(exp_w1, j), (exp_w3, j), (exp_w2, j)]
        x, side_bf16 = _mixer(alpha, i, x, w_in, pool_w, pool_scale, conv_w, conv_b, conv_ln_g, conv_ln_b,
                              conv_pw, w_out, ln1_g, ln1_b, side)
        xt = x.reshape(B * S, D)
        if is_dense:
            w1b, w3b, w2b, gate_b, ple_b = side_bf16
            xt = _dense_layer(alpha, i, xt, pt, w1b, w3b, w2b, ln2_g, ln2_b, gate_b, ple_b)
        else:
            xt = _moe_layer(alpha, i, xt, pt, router_w[j], *side_bf16, ln2_g, ln2_b, ple_gate_w, ple_w)
        x = xt.reshape(B, S, D)
    return x
```

```python
import functools

import jax
import jax.numpy as jnp
from jax import lax
from jax.experimental import pallas as pl
from jax.experimental.pallas import tpu as pltpu

D_MODEL = 1024
D_PLE = 256
D_POOL = 512
D_CONV = D_MODEL - D_POOL
POOL_WINDOWS = (2, 4, 8, 16)
POOL_GROUP_DIM = D_POOL // len(POOL_WINDOWS)
CONV_WIDTH = 31
D_IN_PROJ = D_POOL + 2 * D_CONV
N_EXPERTS = 8
TOP_K = 2
LN_EPS = 1e-5

F32 = jnp.float32
BF16 = jnp.bfloat16

SUBLANES = 8
LANES = 128
VMEM_LIMIT_BYTES = 56 * 1024 * 1024

HALO = 32
TS_MIX = 512
RB_CONV = 128
IN_PROJ_CHUNK = 256
MIX_ROW_GROUPS = 2
TM_DENSE = 512
FF_CHUNK_DENSE = 1024
TR_ROUTE = 512
TM_GROUP = 512
FF_CHUNK_MOE = 1792
TC_DISPATCH = 512
TC_COMBINE = 256
ROW_CHUNKS = D_MODEL // LANES
ISSUE_UNROLL = 8


def _sigmoid(z):
    return 1.0 / (1.0 + jnp.exp(-z))


def _layer_norm(h, g, b):
    mu = jnp.mean(h, axis=-1, keepdims=True)
    c = h - mu
    var = jnp.mean(c * c, axis=-1, keepdims=True)
    return c * lax.rsqrt(var + LN_EPS) * g + b


def _dot(a, b):
    return jnp.dot(a, b, preferred_element_type=F32)


def _side_cast_copies(layers, q, src_hbm, dst_hbm, in_bufs, out_bufs, in_sems, out_sems):
    loads, stores = [], []
    for a in range(len(src_hbm)):
        rows = in_bufs[a].shape[0]
        per_expert = src_hbm[a].shape[2] // rows
        e = q // per_expert
        r0 = pl.multiple_of((q % per_expert) * rows, rows)
        loads.append(pltpu.make_async_copy(src_hbm[a].at[layers[a], e, pl.ds(r0, rows), :], in_bufs[a],
                                           in_sems.at[a]))
        stores.append(pltpu.make_async_copy(out_bufs[a], dst_hbm[a].at[e, pl.ds(r0, rows), :], out_sems.at[a]))
    return loads, stores


def _mixer_kernel(alpha, tiles_per_seq, cast_layers, n_cast_chunks, x_ref, w_in_f32, pool_w_f32, pool_scale_ref,
                  conv_w_ref, conv_b_ref, cln_g_ref, cln_b_ref, conv_pw_f32, w_out_f32, ln_g_ref, ln_b_ref, *rest):
    n_side = len(cast_layers)
    side_f32, rest = rest[:n_side], rest[n_side:]
    o_ref, side_bf16, rest = rest[0], rest[1:1 + n_side], rest[1 + n_side:]
    (p_scr, v_scr, u_scr, y_scr, ypool_scr, xprev_scr, w_in_ref, pool_w_ref, conv_pw_ref, w_out_ref) = rest[:10]
    cast_scratch = rest[10:]
    ts = x_ref.shape[0]
    t = pl.program_id(0)

    if n_side:
        in_bufs, out_bufs = cast_scratch[:n_side], cast_scratch[n_side:2 * n_side]
        in_sems, out_sems = cast_scratch[2 * n_side:]

        def cast_copies(q):
            return _side_cast_copies(cast_layers, q, side_f32, side_bf16, in_bufs, out_bufs, in_sems, out_sems)

        @pl.when(t == 0)
        def _():
            for load in cast_copies(0)[0]:
                load.start()

    @pl.when(t == 0)
    def _():
        w_in_ref[...] = w_in_f32[...].astype(BF16)
        pool_w_ref[...] = pool_w_f32[...].astype(BF16)
        conv_pw_ref[...] = conv_pw_f32[...].astype(BF16)
        w_out_ref[...] = w_out_f32[...].astype(BF16)
        p_scr[...] = jnp.zeros_like(p_scr)
        v_scr[...] = jnp.zeros_like(v_scr)
        ypool_scr[...] = jnp.zeros_like(ypool_scr)
        xprev_scr[...] = jnp.zeros_like(xprev_scr)

    x = x_ref[...]
    xb = x.astype(BF16)
    n_conv_blocks = ts // RB_CONV
    in_proj_chunks = [(c0, min(c0 + IN_PROJ_CHUNK, D_IN_PROJ)) for c0 in range(0, D_IN_PROJ, IN_PROJ_CHUNK)]

    blocks_per_group = n_conv_blocks // MIX_ROW_GROUPS
    for rb in range(n_conv_blocks):
        r0 = rb * RB_CONV
        cols = []
        for lc in range(D_CONV // LANES):
            lanes = slice(lc * LANES, (lc + 1) * LANES)
            acc = jnp.zeros((RB_CONV, LANES), F32)
            for k in range(CONV_WIDTH):
                first = r0 + HALO - (CONV_WIDTH - 1 - k)
                acc = acc + v_scr[lc, first:first + RB_CONV, :] * conv_w_ref[k:k + 1, lanes]
            cols.append(acc)
        y = jnp.concatenate(cols, axis=-1) + conv_b_ref[...]
        z = _layer_norm(y, cln_g_ref[...], cln_b_ref[...])
        y_scr[r0:r0 + RB_CONV, :] = (z * _sigmoid(z)).astype(BF16)
        if rb < len(in_proj_chunks):
            c0, c1 = in_proj_chunks[rb]
            u_scr[:, c0:c1] = _dot(xb, w_in_ref[:, c0:c1])
        if (rb + 1) % blocks_per_group == 0:
            rows = slice((rb + 1 - blocks_per_group) * RB_CONV, (rb + 1) * RB_CONV)
            y_conv = _dot(y_scr[rows, :], conv_pw_ref[...])
            heads = jnp.concatenate([ypool_scr[rows, :], y_conv.astype(BF16)], axis=-1)
            mix = _dot(heads, w_out_ref[...])
            o_ref[rows, :] = _layer_norm(alpha * xprev_scr[rows, :] + mix, ln_g_ref[...], ln_b_ref[...])
    for c0, c1 in in_proj_chunks[n_conv_blocks:]:
        u_scr[:, c0:c1] = _dot(xb, w_in_ref[:, c0:c1])

    tile_in_seq = t % tiles_per_seq
    starts_seq = tile_in_seq == 0
    p_scr[:, 0:HALO, :] = jnp.where(starts_seq, 0.0, p_scr[:, ts:ts + HALO, :])
    v_scr[:, 0:HALO, :] = jnp.where(starts_seq, 0.0, v_scr[:, ts:ts + HALO, :])
    for c in range(D_CONV // LANES):
        val = u_scr[:, D_POOL + c * LANES:D_POOL + (c + 1) * LANES]
        gate = u_scr[:, D_POOL + D_CONV + c * LANES:D_POOL + D_CONV + (c + 1) * LANES]
        v_scr[c, HALO:, :] = val * _sigmoid(gate)

    pos = tile_in_seq * ts + lax.broadcasted_iota(jnp.int32, (ts, POOL_GROUP_DIM), 0)
    for g, w in enumerate(POOL_WINDOWS):
        lanes = slice(g * POOL_GROUP_DIM, (g + 1) * POOL_GROUP_DIM)
        cur = u_scr[:, lanes]
        p_scr[g, HALO:, :] = cur
        win = cur
        for back in range(1, w):
            win = win + p_scr[g, HALO - back:HALO - back + ts, :]
        cnt = jnp.minimum(pos + 1, w).astype(F32)
        d = win / cnt - cur
        yg = _dot(d.astype(BF16), pool_w_ref[g]) * pool_scale_ref[:, lanes]
        ypool_scr[:, lanes] = yg.astype(BF16)

    xprev_scr[...] = x

    if n_side:
        @pl.when(t < n_cast_chunks)
        def _():
            loads, stores = cast_copies(t)
            for load in loads:
                load.wait()

            @pl.when(t > 0)
            def _():
                for store in cast_copies(t - 1)[1]:
                    store.wait()

            for src, dst in zip(in_bufs, out_bufs):
                dst[...] = src[...].astype(BF16)
            for store in stores:
                store.start()

            @pl.when(t + 1 < n_cast_chunks)
            def _():
                for load in cast_copies(t + 1)[0]:
                    load.start()

        @pl.when(t == n_cast_chunks)
        def _():
            for store in cast_copies(t - 1)[1]:
                store.wait()


def _const_spec(shape):
    nd = len(shape)
    return pl.BlockSpec(shape, lambda *_: (0,) * nd, pipeline_mode=pl.Buffered(1))


def _per_layer(v):
    return v.reshape(v.shape[0], 1, v.shape[1]) if v.ndim == 2 else v


def _layer_spec(v, layer, **kwargs):
    nd = v.ndim
    return pl.BlockSpec((None,) + v.shape[1:], lambda *_: (layer,) + (0,) * (nd - 1),
                        pipeline_mode=pl.Buffered(1), **kwargs)


def _mixer(alpha, layer, x, w_in, pool_w, pool_scale, conv_w, conv_b, cln_g, cln_b, conv_pw, w_out, ln_g, ln_b,
           side_weights):
    B, S, D = x.shape
    ts = TS_MIX
    n_tiles = B * S // ts
    consts = [_per_layer(c) for c in (w_in, pool_w, pool_scale, conv_w, conv_b, cln_g, cln_b, conv_pw, w_out,
                                      ln_g, ln_b)]
    x_spec = pl.BlockSpec((ts, D), lambda t: (jnp.minimum(t, n_tiles - 1), 0))
    o_spec = pl.BlockSpec((ts, D), lambda t: (jnp.maximum(t - 1, 0), 0))
    o_shape = jax.ShapeDtypeStruct((B * S, D), F32)
    scratch = [pltpu.VMEM((D_POOL // LANES, HALO + ts, LANES), F32),
               pltpu.VMEM((D_CONV // LANES, HALO + ts, LANES), F32),
               pltpu.VMEM((ts, D_IN_PROJ), F32),
               pltpu.VMEM((ts, D_CONV), BF16),
               pltpu.VMEM((ts, D_POOL), BF16),
               pltpu.VMEM((ts, D), F32),
               pltpu.VMEM(w_in.shape[1:], BF16), pltpu.VMEM(pool_w.shape[1:], BF16),
               pltpu.VMEM(conv_pw.shape[1:], BF16), pltpu.VMEM(w_out.shape[1:], BF16)]
    params = pltpu.CompilerParams(dimension_semantics=("arbitrary",), vmem_limit_bytes=VMEM_LIMIT_BYTES)
    in_specs = [x_spec] + [_layer_spec(c, layer) for c in consts]

    mats = [m for m, _ in side_weights]
    cast_layers = tuple(l for _, l in side_weights)
    hbm = pl.BlockSpec(memory_space=pl.ANY)
    bf16_rows = 2 * SUBLANES
    n_chunks = max(n for n in range(1, n_tiles + 1)
                   if all((m.shape[1] * m.shape[2]) % n == 0
                          and m.shape[2] % ((m.shape[1] * m.shape[2]) // n) == 0
                          and ((m.shape[1] * m.shape[2]) // n) % bf16_rows == 0 for m in mats))
    chunk_rows = [m.shape[1] * m.shape[2] // n_chunks for m in mats]
    cast_scratch = ([pltpu.VMEM((rows, m.shape[3]), F32) for m, rows in zip(mats, chunk_rows)]
                    + [pltpu.VMEM((rows, m.shape[3]), BF16) for m, rows in zip(mats, chunk_rows)]
                    + [pltpu.SemaphoreType.DMA((len(mats),)), pltpu.SemaphoreType.DMA((len(mats),))])
    out, *mats_bf16 = pl.pallas_call(
        functools.partial(_mixer_kernel, alpha, S // ts, cast_layers, n_chunks),
        grid=(n_tiles + 1,), in_specs=in_specs + [hbm] * len(mats),
        out_specs=[o_spec] + [hbm] * len(mats),
        out_shape=[o_shape] + [jax.ShapeDtypeStruct(m.shape[1:], BF16) for m in mats],
        scratch_shapes=scratch + cast_scratch, compiler_params=params, name="mixer",
    )(x.reshape(B * S, D), *consts, *mats)
    return out.reshape(B, S, D), mats_bf16


def _norm_and_embed(alpha, x, f, p, ln_g, ln_b, gate_w, ple_w):
    h = _layer_norm(alpha * x + f, ln_g, ln_b)
    gate = _sigmoid(_dot(h.astype(BF16), gate_w))
    return h + gate * _dot(p.astype(BF16), ple_w)


def _dense_ffn_kernel(alpha, x_ref, p_ref, w1_ref, w3_ref, w2_ref, ln_g_ref, ln_b_ref, gate_w_ref, ple_w_ref,
                      o_ref, acc_ref):
    x = x_ref[...]
    xb = x.astype(BF16)
    ff = w1_ref.shape[1]
    for c0 in range(0, ff, FF_CHUNK_DENSE):
        c1 = min(c0 + FF_CHUNK_DENSE, ff)
        a = _dot(xb, w1_ref[:, c0:c1])
        b = _dot(xb, w3_ref[:, c0:c1])
        h = (a * _sigmoid(a) * b).astype(BF16)
        part = _dot(h, w2_ref[c0:c1, :])
        if c0 == 0:
            acc_ref[...] = part
        else:
            acc_ref[...] += part
    o_ref[...] = _norm_and_embed(alpha, x, acc_ref[...], p_ref[...], ln_g_ref[...], ln_b_ref[...],
                                 gate_w_ref[...], ple_w_ref[...])


def _dense_layer(alpha, layer, x, p, w1, w3, w2, ln_g, ln_b, gate_w, ple_w):
    T, D = x.shape
    tm = TM_DENSE
    ln_g, ln_b = _per_layer(ln_g), _per_layer(ln_b)
    return pl.pallas_call(
        functools.partial(_dense_ffn_kernel, alpha),
        grid=(T // tm,),
        in_specs=[pl.BlockSpec((tm, D), lambda i: (i, 0)),
                  pl.BlockSpec((None, tm, D_PLE), lambda i: (layer, i, 0)),
                  _layer_spec(w1, 0), _layer_spec(w3, 0), _layer_spec(w2, 0),
                  _layer_spec(ln_g, layer), _layer_spec(ln_b, layer),
                  _layer_spec(gate_w, 0), _layer_spec(ple_w, 0)],
        out_specs=pl.BlockSpec((tm, D), lambda i: (i, 0)),
        out_shape=jax.ShapeDtypeStruct((T, D), F32),
        scratch_shapes=[pltpu.VMEM((tm, D), F32)],
        compiler_params=pltpu.CompilerParams(
            dimension_semantics=("arbitrary",), vmem_limit_bytes=VMEM_LIMIT_BYTES),
        name="dense_ffn",
    )(x, p, w1, w3, w2, ln_g, ln_b, gate_w, ple_w)


def _router_kernel(x_ref, wt_ref, idx_ref, gate_ref, cnt_ref, carry_ref):
    i = pl.program_id(0)
    tr = x_ref.shape[0]

    @pl.when(i == 0)
    def _():
        carry_ref[...] = jnp.zeros_like(carry_ref)

    x = x_ref[...]
    wt = wt_ref[...]
    xh = x.astype(BF16)
    xl = (x - xh.astype(F32)).astype(BF16)
    wh = wt.astype(BF16)
    wl = (wt - wh.astype(F32)).astype(BF16)
    nt_dims = (((1,), (1,)), ((), ()))
    dg = lambda a, b: lax.dot_general(a, b, nt_dims, preferred_element_type=F32)
    with_xh = dg(jnp.concatenate([wh, wl], axis=0), xh)
    logits = with_xh[:N_EXPERTS] + (dg(wh, xl) + with_xh[N_EXPERTS:])

    eid = lax.broadcasted_iota(jnp.int32, logits.shape, 0)
    m1 = jnp.max(logits, axis=0, keepdims=True)
    i1 = jnp.min(jnp.where(logits == m1, eid, N_EXPERTS), axis=0, keepdims=True)
    rest = jnp.where(eid == i1, -jnp.inf, logits)
    m2 = jnp.max(rest, axis=0, keepdims=True)
    i2 = jnp.min(jnp.where(rest == m2, eid, N_EXPERTS), axis=0, keepdims=True)
    e2 = jnp.exp(m2 - m1)
    g1 = 1.0 / (1.0 + e2)
    g2 = e2 / (1.0 + e2)

    oh1 = (eid == i1).astype(F32)
    oh2 = (eid == i2).astype(F32)
    chosen = oh1 + oh2
    r_i = lax.broadcasted_iota(jnp.int32, (tr, tr), 0)
    c_i = lax.broadcasted_iota(jnp.int32, (tr, tr), 1)
    before = (r_i < c_i).astype(BF16)
    excl = _dot(chosen.astype(BF16), before) + carry_ref[:, 0:1]
    rank1 = jnp.sum(oh1 * excl, axis=0, keepdims=True)
    rank2 = jnp.sum(oh2 * excl, axis=0, keepdims=True)

    total = carry_ref[:, 0:1] + jnp.sum(chosen, axis=1, keepdims=True)
    carry_ref[...] = jnp.broadcast_to(total, carry_ref.shape)
    cnt_ref[...] = jnp.broadcast_to(total, cnt_ref.shape).astype(jnp.int32)

    zi = jnp.zeros((SUBLANES - 4, tr), jnp.int32)
    idx_ref[...] = jnp.concatenate(
        [i1, i2, rank1.astype(jnp.int32), rank2.astype(jnp.int32), zi], axis=0)
    gate_ref[...] = jnp.concatenate([g1, g2, jnp.zeros((SUBLANES - 2, tr), F32)], axis=0)


def _router(x, router_w):
    T, D = x.shape
    tr = TR_ROUTE
    return pl.pallas_call(
        _router_kernel,
        grid=(T // tr,),
        in_specs=[pl.BlockSpec((tr, D), lambda i: (i, 0)), _const_spec((N_EXPERTS, D))],
        out_specs=[pl.BlockSpec((SUBLANES, tr), lambda i: (0, i)),
                   pl.BlockSpec((SUBLANES, tr), lambda i: (0, i)),
                   pl.BlockSpec((N_EXPERTS, LANES), lambda i: (0, 0))],
        out_shape=[jax.ShapeDtypeStruct((SUBLANES, T), jnp.int32),
                   jax.ShapeDtypeStruct((SUBLANES, T), F32),
                   jax.ShapeDtypeStruct((N_EXPERTS, LANES), jnp.int32)],
        scratch_shapes=[pltpu.VMEM((N_EXPERTS, LANES), F32)],
        compiler_params=pltpu.CompilerParams(
            dimension_semantics=("arbitrary",), vmem_limit_bytes=VMEM_LIMIT_BYTES),
        name="router",
    )(x, router_w.T)


def _store_token_tiles(dst_ref, value):
    rows = value.shape[0]
    for c in range(ROW_CHUNKS):
        dst_ref[pl.ds(c, rows, stride=ROW_CHUNKS), :] = value[:, c * LANES:(c + 1) * LANES]


def _load_token_tile_chunk(src_ref, rows, c):
    return src_ref[pl.ds(c, rows, stride=ROW_CHUNKS), :]


def _load_token_tiles(src_ref, rows):
    return jnp.concatenate([_load_token_tile_chunk(src_ref, rows, c) for c in range(ROW_CHUNKS)], axis=-1)


def _tile_rows(row):
    return pl.ds(pl.multiple_of(row * ROW_CHUNKS, ROW_CHUNKS), ROW_CHUNKS)


def _dispatch_kernel(dest_ref, fill_ref, x_ref, xs_hbm, xt_ref, zero_ref, sems, fill_sem):
    i = pl.program_id(0)
    n = pl.num_programs(0)
    tc = x_ref.shape[0]
    n_tok = n * tc
    base = i * tc
    slot = i % 2
    stage = xt_ref.at[slot]

    @pl.when(i == 0)
    def _():
        zero_ref[...] = jnp.zeros_like(zero_ref)
        tile_rows = zero_ref.shape[0]

        def fill(j):
            start = pl.multiple_of(fill_ref[j] * tile_rows, tile_rows)
            return pltpu.make_async_copy(zero_ref, xs_hbm.at[pl.ds(start, tile_rows), :], fill_sem)

        for j in range(fill_ref.shape[0]):
            @pl.when(fill_ref[j] >= 0)
            def _():
                fill(j).start()

        for j in range(fill_ref.shape[0]):
            @pl.when(fill_ref[j] >= 0)
            def _():
                fill(j).wait()

    _store_token_tiles(stage, x_ref[...])

    def copy(k, r):
        d = dest_ref[k * n_tok + base + r]
        return pltpu.make_async_copy(stage.at[_tile_rows(r), :], xs_hbm.at[_tile_rows(d), :], sems.at[slot])

    def issue(rb, c):
        for j in range(ISSUE_UNROLL):
            r = rb * ISSUE_UNROLL + j
            for k in range(TOP_K):
                copy(k, r).start(priority=k)
        return c

    lax.fori_loop(0, tc // ISSUE_UNROLL, issue, 0)

    def wait_step(s):
        for _ in range(TOP_K):
            pltpu.make_async_copy(xt_ref.at[s], xs_hbm.at[pl.ds(0, tc * ROW_CHUNKS), :], sems.at[s]).wait()

    @pl.when(i > 0)
    def _():
        wait_step(1 - slot)

    @pl.when(i == n - 1)
    def _():
        wait_step(slot)


def _dispatch(x, dest, fill_tiles, n_rows):
    T, D = x.shape
    tc = TC_DISPATCH
    return pl.pallas_call(
        _dispatch_kernel,
        grid_spec=pltpu.PrefetchScalarGridSpec(
            num_scalar_prefetch=2, grid=(T // tc,),
            in_specs=[pl.BlockSpec((tc, D), lambda i, d, ft: (i, 0))],
            out_specs=pl.BlockSpec(memory_space=pl.ANY),
            scratch_shapes=[pltpu.VMEM((2, tc * ROW_CHUNKS, LANES), F32),
                            pltpu.VMEM((TM_GROUP * ROW_CHUNKS, LANES), F32),
                            pltpu.SemaphoreType.DMA((2,)), pltpu.SemaphoreType.DMA(())]),
        out_shape=jax.ShapeDtypeStruct((n_rows * ROW_CHUNKS, LANES), F32),
        compiler_params=pltpu.CompilerParams(dimension_semantics=("arbitrary",)),
        name="dispatch",
    )(dest, fill_tiles, x)


def _expert_ffn_kernel(nf, te_ref, valid_ref, nt_ref, xs_ref, w1_ref, w3_ref, w2_ref, o_ref, xb_ref, acc_ref):
    i = pl.program_id(0)
    f = pl.program_id(1)
    tm = acc_ref.shape[0]

    def run(m, first, last):
        if first:
            for c in range(ROW_CHUNKS):
                chunk = _load_token_tile_chunk(xs_ref.at[pl.ds(0, m * ROW_CHUNKS), :], m, c)
                xb_ref[0:m, c * LANES:(c + 1) * LANES] = chunk.astype(BF16)

        xb = xb_ref[0:m, :]
        a = _dot(xb, w1_ref[...])
        b = _dot(xb, w3_ref[...])
        h = (a * _sigmoid(a) * b).astype(BF16)
        part = _dot(h, w2_ref[...])

        if last:
            total = part if first else acc_ref[0:m, :] + part
            _store_token_tiles(o_ref.at[pl.ds(0, m * ROW_CHUNKS), :], total)
            if m < tm:
                o_ref[m * ROW_CHUNKS:, :] = jnp.zeros(((tm - m) * ROW_CHUNKS, LANES), F32)
        elif first:
            acc_ref[0:m, :] = part
        else:
            acc_ref[0:m, :] += part

    live = i < nt_ref[0]
    few = valid_ref[i] <= tm // 2
    for m, size_matches in ((tm, jnp.logical_not(few)), (tm // 2, few)):
        for chunk in range(nf):
            @pl.when(live & size_matches & (f == chunk))
            def _():
                run(m, chunk == 0, chunk == nf - 1)

    @pl.when(jnp.logical_not(live) & (f == 0))
    def _():
        o_ref[...] = jnp.zeros_like(o_ref)


def _expert_ffn(xs, te, valid, nt, w1, w3, w2):
    E, D, FF = w1.shape
    tm, cf = TM_GROUP, FF_CHUNK_MOE
    nf = FF // cf
    max_tiles = xs.shape[0] // (tm * ROW_CHUNKS)

    def tile(i, nt_ref):
        return jnp.minimum(i, nt_ref[0] - 1)

    def chunk(i, f, nt_ref):
        return jnp.where(i < nt_ref[0], f, nf - 1)

    def w13_map(i, f, te, valid, nt):
        return (te[tile(i, nt)], 0, chunk(i, f, nt))

    def w2_map(i, f, te, valid, nt):
        return (te[tile(i, nt)], chunk(i, f, nt), 0)

    return pl.pallas_call(
        functools.partial(_expert_ffn_kernel, nf),
        grid_spec=pltpu.PrefetchScalarGridSpec(
            num_scalar_prefetch=3, grid=(max_tiles, nf),
            in_specs=[pl.BlockSpec((tm * ROW_CHUNKS, LANES), lambda i, f, te, valid, nt: (tile(i, nt), 0)),
                      pl.BlockSpec((None, D, cf), w13_map),
                      pl.BlockSpec((None, D, cf), w13_map),
                      pl.BlockSpec((None, cf, D), w2_map)],
            out_specs=pl.BlockSpec((tm * ROW_CHUNKS, LANES), lambda i, f, te, valid, nt: (i, 0)),
            scratch_shapes=[pltpu.VMEM((tm, D), BF16), pltpu.VMEM((tm, D), F32)]),
        out_shape=jax.ShapeDtypeStruct(xs.shape, F32),
        compiler_params=pltpu.CompilerParams(
            dimension_semantics=("arbitrary", "arbitrary"), vmem_limit_bytes=VMEM_LIMIT_BYTES),
        name="expert_ffn",
    )(te, valid, nt, xs, w1, w3, w2)


def _sorted_layout(cnt, n_assign):
    tm = TM_GROUP
    max_tiles = n_assign // tm + N_EXPERTS
    tiles_e = (cnt + tm - 1) // tm
    tile_end = jnp.cumsum(tiles_e)
    row_start = (tile_end - tiles_e) * tm
    nt = tile_end[-1]
    tile_ids = jnp.minimum(jnp.arange(max_tiles, dtype=jnp.int32), nt - 1)
    te = jnp.sum((tile_end[None, :] <= tile_ids[:, None]).astype(jnp.int32), axis=1)
    te = jnp.minimum(te, N_EXPERTS - 1)
    owner = te[:, None] == jnp.arange(N_EXPERTS, dtype=jnp.int32)[None, :]
    rows_left = cnt[None, :] - (tile_ids[:, None] - (tile_end - tiles_e)[None, :]) * tm
    valid = jnp.clip(jnp.sum(jnp.where(owner, rows_left, 0), axis=1), 0, tm).astype(jnp.int32)
    last_tile = jnp.where(tiles_e > 0, tile_end - 1, -1)
    tail = nt + jnp.arange(N_EXPERTS, dtype=jnp.int32)
    tail = jnp.where(tail < max_tiles, tail, -1)
    fill_tiles = jnp.concatenate([last_tile, tail]).astype(jnp.int32)
    return row_start, te, valid, nt.reshape(1).astype(jnp.int32), fill_tiles, max_tiles * tm


def _combine_kernel(alpha, dest_ref, x_ref, p_ref, gate_ref, ys_hbm, ln_g_ref, ln_b_ref, gate_w_f32, ple_w_f32,
                    o_ref, yb_ref, gate_w_ref, ple_w_ref, sems):
    i = pl.program_id(0)
    n = pl.num_programs(0)
    tc = x_ref.shape[0]
    n_tok = n * tc
    slot = i % 2

    @pl.when(i == 0)
    def _():
        gate_w_ref[...] = gate_w_f32[...].astype(BF16)
        ple_w_ref[...] = ple_w_f32[...].astype(BF16)

    def row_copy(step, s, k, r):
        d = dest_ref[k * n_tok + step * tc + r]
        return pltpu.make_async_copy(ys_hbm.at[_tile_rows(d), :], yb_ref.at[s, k, _tile_rows(r), :], sems.at[s])

    def wait_rows(s):
        for k in range(TOP_K):
            pltpu.make_async_copy(ys_hbm.at[pl.ds(0, tc * ROW_CHUNKS), :], yb_ref.at[s, k], sems.at[s]).wait()

    @pl.when(i == 0)
    def _():
        def issue(rb, c):
            for j in range(ISSUE_UNROLL):
                for k in range(TOP_K):
                    row_copy(0, 0, k, rb * ISSUE_UNROLL + j).start(priority=k)
            return c

        lax.fori_loop(0, tc // ISSUE_UNROLL, issue, 0)

    wait_rows(slot)

    nxt = jnp.minimum(i + 1, n - 1)
    for r in range(tc):
        for k in range(TOP_K):
            row_copy(nxt, 1 - slot, k, r).start(priority=k)

    g = gate_ref[...]
    f = (g[:, 0:1] * _load_token_tiles(yb_ref.at[slot, 0], tc)
         + g[:, 1:2] * _load_token_tiles(yb_ref.at[slot, 1], tc))
    o_ref[...] = _norm_and_embed(alpha, x_ref[...], f, p_ref[...], ln_g_ref[...], ln_b_ref[...],
                                 gate_w_ref[...], ple_w_ref[...])

    @pl.when(i == n - 1)
    def _():
        wait_rows(1 - slot)


def _combine_layer(alpha, layer, x, p, gates, dest, ys, ln_g, ln_b, gate_w, ple_w):
    T, D = x.shape
    tc = TC_COMBINE
    consts = [_per_layer(c) for c in (ln_g, ln_b, gate_w, ple_w)]
    return pl.pallas_call(
        functools.partial(_combine_kernel, alpha),
        grid_spec=pltpu.PrefetchScalarGridSpec(
            num_scalar_prefetch=1, grid=(T // tc,),
            in_specs=[pl.BlockSpec((tc, D), lambda i, d: (i, 0)),
                      pl.BlockSpec((None, tc, D_PLE), lambda i, d: (layer, i, 0)),
                      pl.BlockSpec((tc, TOP_K), lambda i, d: (i, 0)),
                      pl.BlockSpec(memory_space=pl.ANY)]
                     + [_layer_spec(c, layer) for c in consts],
            out_specs=pl.BlockSpec((tc, D), lambda i, d: (i, 0)),
            scratch_shapes=[pltpu.VMEM((2, TOP_K, tc * ROW_CHUNKS, LANES), F32),
                            pltpu.VMEM((D, D), BF16), pltpu.VMEM((D_PLE, D), BF16),
                            pltpu.SemaphoreType.DMA((2,))]),
        out_shape=jax.ShapeDtypeStruct((T, D), F32),
        compiler_params=pltpu.CompilerParams(
            dimension_semantics=("arbitrary",), vmem_limit_bytes=VMEM_LIMIT_BYTES),
        name="combine",
    )(dest, x, p, gates, ys, *consts)


def _moe_layer(alpha, layer, x, p, router_w, w1, w3, w2, ln_g, ln_b, gate_w, ple_w):
    T, D = x.shape
    route_i, route_g, counts = _router(x, router_w)
    idx = route_i[0:TOP_K]
    rank = route_i[TOP_K:2 * TOP_K]
    gates = route_g[0:TOP_K].T

    start, te, valid, nt, fill_tiles, n_rows = _sorted_layout(counts[:, 0], TOP_K * T)
    start_of = sum(jnp.where(idx == e, start[e], 0) for e in range(N_EXPERTS))
    dest = (start_of + rank).astype(jnp.int32).reshape(-1)

    xs = _dispatch(x, dest, fill_tiles, n_rows)
    ys = _expert_ffn(xs, te, valid, nt, w1, w3, w2)
    return _combine_layer(alpha, layer, x, p, gates, dest, ys, ln_g, ln_b, gate_w, ple_w)


def kernel(x, p, w_in, pool_w, pool_scale, conv_w, conv_b, conv_ln_g, conv_ln_b, conv_pw, w_out, ln1_g, ln1_b,
           dense_w1, dense_w3, dense_w2, router_w, exp_w1, exp_w3, exp_w2, ln2_g, ln2_b, ple_gate_w, ple_w):
    depth = w_in.shape[0]
    alpha = (2.0 * depth) ** 0.25
    B, S, D = x.shape
    pt = p.reshape(depth, B * S, D_PLE)
    one_group = lambda m: m.reshape(m.shape[0], 1, *m.shape[1:])
    for i in range(depth):
        j = i // 2
        is_dense = i % 2 == 0
        if is_dense:
            side = [(one_group(dense_w1), j), (one_group(dense_w3), j), (one_group(dense_w2), j),
                    (one_group(ple_gate_w), i), (one_group(ple_w), i)]
        else:
            side = [(exp_w1, j), (exp_w3, j), (exp_w2, j)]
        x, side_bf16 = _mixer(alpha, i, x, w_in, pool_w, pool_scale, conv_w, conv_b, conv_ln_g, conv_ln_b,
                              conv_pw, w_out, ln1_g, ln1_b, side)
        xt = x.reshape(B * S, D)
        if is_dense:
            w1b, w3b, w2b, gate_b, ple_b = side_bf16
            xt = _dense_layer(alpha, i, xt, pt, w1b, w3b, w2b, ln2_g, ln2_b, gate_b, ple_b)
        else:
            xt = _moe_layer(alpha, i, xt, pt, router_w[j], *side_bf16, ln2_g, ln2_b, ple_gate_w, ple_w)
        x = xt.reshape(B, S, D)
    return x
```

```python
import functools

import jax
import jax.numpy as jnp
from jax import lax
from jax.experimental import pallas as pl
from jax.experimental.pallas import tpu as pltpu

D_MODEL = 1024
D_PLE = 256
D_POOL = 512
D_CONV = D_MODEL - D_POOL
POOL_WINDOWS = (2, 4, 8, 16)
POOL_GROUP_DIM = D_POOL // len(POOL_WINDOWS)
CONV_WIDTH = 31
D_IN_PROJ = D_POOL + 2 * D_CONV
N_EXPERTS = 8
TOP_K = 2
LN_EPS = 1e-5

F32 = jnp.float32
BF16 = jnp.bfloat16

SUBLANES = 8
LANES = 128
VMEM_LIMIT_BYTES = 56 * 1024 * 1024

HALO = 32
TS_MIX = 512
RB_CONV = 128
IN_PROJ_CHUNK = 256
MIX_ROW_GROUPS = 2
TM_DENSE = 512
FF_CHUNK_DENSE = 1024
TR_ROUTE = 512
TM_GROUP = 512
FF_CHUNK_MOE = 1792
TC_DISPATCH = 512
TC_COMBINE = 256
ROW_CHUNKS = D_MODEL // LANES
ISSUE_UNROLL = 8


def _sigmoid(z):
    return 1.0 / (1.0 + jnp.exp(-z))


def _layer_norm(h, g, b):
    mu = jnp.mean(h, axis=-1, keepdims=True)
    c = h - mu
    var = jnp.mean(c * c, axis=-1, keepdims=True)
    return c * lax.rsqrt(var + LN_EPS) * g + b


def _dot(a, b):
    return jnp.dot(a, b, preferred_element_type=F32)


def _side_cast_copies(load, layers, q, hbm, bufs, sems):
    copies = []
    for a in range(len(hbm)):
        rows = bufs[a].shape[0]
        per_group = hbm[a].shape[-2] // rows
        g = q // per_group
        r0 = pl.multiple_of((q % per_group) * rows, rows)
        if load:
            copies.append(pltpu.make_async_copy(hbm[a].at[layers[a], g, pl.ds(r0, rows), :], bufs[a], sems.at[a]))
        else:
            copies.append(pltpu.make_async_copy(bufs[a], hbm[a].at[g, pl.ds(r0, rows), :], sems.at[a]))
    return copies


def _mixer_kernel(alpha, tiles_per_seq, cast_layers, n_cast_chunks, x_ref, w_in_f32, pool_w_f32, pool_scale_ref,
                  conv_w_ref, conv_b_ref, cln_g_ref, cln_b_ref, conv_pw_f32, w_out_f32, ln_g_ref, ln_b_ref, *rest):
    n_side = len(cast_layers)
    side_f32, rest = rest[:n_side], rest[n_side:]
    o_ref, side_bf16, rest = rest[0], rest[1:1 + n_side], rest[1 + n_side:]
    (p_scr, v_scr, u_scr, y_scr, ypool_scr, xprev_scr, w_in_ref, pool_w_ref, conv_pw_ref, w_out_ref) = rest[:10]
    cast_scratch = rest[10:]
    ts = x_ref.shape[0]
    t = pl.program_id(0)

    if n_side:
        in_bufs, out_bufs = cast_scratch[:n_side], cast_scratch[n_side:2 * n_side]
        in_sems, out_sems = cast_scratch[2 * n_side:]

        def chunk_loads(q):
            return _side_cast_copies(True, cast_layers, q, side_f32, in_bufs, in_sems)

        def chunk_stores(q):
            return _side_cast_copies(False, cast_layers, q, side_bf16, out_bufs, out_sems)

        @pl.when(t == 0)
        def _():
            for load in chunk_loads(0):
                load.start()

    @pl.when(t == 0)
    def _():
        w_in_ref[...] = w_in_f32[...].astype(BF16)
        pool_w_ref[...] = pool_w_f32[...].astype(BF16)
        conv_pw_ref[...] = conv_pw_f32[...].astype(BF16)
        w_out_ref[...] = w_out_f32[...].astype(BF16)
        p_scr[...] = jnp.zeros_like(p_scr)
        v_scr[...] = jnp.zeros_like(v_scr)
        ypool_scr[...] = jnp.zeros_like(ypool_scr)
        xprev_scr[...] = jnp.zeros_like(xprev_scr)

    x = x_ref[...]
    xb = x.astype(BF16)
    n_conv_blocks = ts // RB_CONV
    in_proj_chunks = [(c0, min(c0 + IN_PROJ_CHUNK, D_IN_PROJ)) for c0 in range(0, D_IN_PROJ, IN_PROJ_CHUNK)]

    blocks_per_group = n_conv_blocks // MIX_ROW_GROUPS
    for rb in range(n_conv_blocks):
        r0 = rb * RB_CONV
        cols = []
        for lc in range(D_CONV // LANES):
            lanes = slice(lc * LANES, (lc + 1) * LANES)
            acc = jnp.zeros((RB_CONV, LANES), F32)
            for k in range(CONV_WIDTH):
                first = r0 + HALO - (CONV_WIDTH - 1 - k)
                acc = acc + v_scr[lc, first:first + RB_CONV, :] * conv_w_ref[k:k + 1, lanes]
            cols.append(acc)
        y = jnp.concatenate(cols, axis=-1) + conv_b_ref[...]
        z = _layer_norm(y, cln_g_ref[...], cln_b_ref[...])
        y_scr[r0:r0 + RB_CONV, :] = (z * _sigmoid(z)).astype(BF16)
        if rb < len(in_proj_chunks):
            c0, c1 = in_proj_chunks[rb]
            u_scr[:, c0:c1] = _dot(xb, w_in_ref[:, c0:c1])
        if (rb + 1) % blocks_per_group == 0:
            rows = slice((rb + 1 - blocks_per_group) * RB_CONV, (rb + 1) * RB_CONV)
            y_conv = _dot(y_scr[rows, :], conv_pw_ref[...])
            heads = jnp.concatenate([ypool_scr[rows, :], y_conv.astype(BF16)], axis=-1)
            mix = _dot(heads, w_out_ref[...])
            o_ref[rows, :] = _layer_norm(alpha * xprev_scr[rows, :] + mix, ln_g_ref[...], ln_b_ref[...])
    for c0, c1 in in_proj_chunks[n_conv_blocks:]:
        u_scr[:, c0:c1] = _dot(xb, w_in_ref[:, c0:c1])

    tile_in_seq = t % tiles_per_seq
    starts_seq = tile_in_seq == 0
    p_scr[:, 0:HALO, :] = jnp.where(starts_seq, 0.0, p_scr[:, ts:ts + HALO, :])
    v_scr[:, 0:HALO, :] = jnp.where(starts_seq, 0.0, v_scr[:, ts:ts + HALO, :])
    for c in range(D_CONV // LANES):
        val = u_scr[:, D_POOL + c * LANES:D_POOL + (c + 1) * LANES]
        gate = u_scr[:, D_POOL + D_CONV + c * LANES:D_POOL + D_CONV + (c + 1) * LANES]
        v_scr[c, HALO:, :] = val * _sigmoid(gate)

    pos = tile_in_seq * ts + lax.broadcasted_iota(jnp.int32, (ts, POOL_GROUP_DIM), 0)
    for g, w in enumerate(POOL_WINDOWS):
        lanes = slice(g * POOL_GROUP_DIM, (g + 1) * POOL_GROUP_DIM)
        cur = u_scr[:, lanes]
        p_scr[g, HALO:, :] = cur
        win = cur
        for back in range(1, w):
            win = win + p_scr[g, HALO - back:HALO - back + ts, :]
        cnt = jnp.minimum(pos + 1, w).astype(F32)
        d = win / cnt - cur
        yg = _dot(d.astype(BF16), pool_w_ref[g]) * pool_scale_ref[:, lanes]
        ypool_scr[:, lanes] = yg.astype(BF16)

    xprev_scr[...] = x

    if n_side:
        @pl.when(t < n_cast_chunks)
        def _():
            for load in chunk_loads(t):
                load.wait()

            @pl.when(t > 0)
            def _():
                for store in chunk_stores(t - 1):
                    store.wait()

            for src, dst in zip(in_bufs, out_bufs):
                dst[...] = src[...].astype(BF16)
            for store in chunk_stores(t):
                store.start()

            @pl.when(t + 1 < n_cast_chunks)
            def _():
                for load in chunk_loads(t + 1):
                    load.start()

        @pl.when(t == n_cast_chunks)
        def _():
            for store in chunk_stores(t - 1):
                store.wait()


def _const_spec(shape):
    nd = len(shape)
    return pl.BlockSpec(shape, lambda *_: (0,) * nd, pipeline_mode=pl.Buffered(1))


def _per_layer(v):
    return v.reshape(v.shape[0], 1, v.shape[1]) if v.ndim == 2 else v


def _layer_spec(v, layer, **kwargs):
    nd = v.ndim
    return pl.BlockSpec((None,) + v.shape[1:], lambda *_: (layer,) + (0,) * (nd - 1),
                        pipeline_mode=pl.Buffered(1), **kwargs)


def _mixer(alpha, layer, x, w_in, pool_w, pool_scale, conv_w, conv_b, cln_g, cln_b, conv_pw, w_out, ln_g, ln_b,
           side_weights):
    B, S, D = x.shape
    ts = TS_MIX
    n_tiles = B * S // ts
    assert D == D_MODEL and S % ts == 0 and POOL_GROUP_DIM == LANES and D_CONV % LANES == 0
    assert ts % (RB_CONV * MIX_ROW_GROUPS) == 0 and HALO >= max(CONV_WIDTH, *POOL_WINDOWS) - 1
    consts = [_per_layer(c) for c in (w_in, pool_w, pool_scale, conv_w, conv_b, cln_g, cln_b, conv_pw, w_out,
                                      ln_g, ln_b)]
    x_spec = pl.BlockSpec((ts, D), lambda t: (jnp.minimum(t, n_tiles - 1), 0))
    o_spec = pl.BlockSpec((ts, D), lambda t: (jnp.maximum(t - 1, 0), 0))
    o_shape = jax.ShapeDtypeStruct((B * S, D), F32)
    scratch = [pltpu.VMEM((D_POOL // LANES, HALO + ts, LANES), F32),
               pltpu.VMEM((D_CONV // LANES, HALO + ts, LANES), F32),
               pltpu.VMEM((ts, D_IN_PROJ), F32),
               pltpu.VMEM((ts, D_CONV), BF16),
               pltpu.VMEM((ts, D_POOL), BF16),
               pltpu.VMEM((ts, D), F32),
               pltpu.VMEM(w_in.shape[1:], BF16), pltpu.VMEM(pool_w.shape[1:], BF16),
               pltpu.VMEM(conv_pw.shape[1:], BF16), pltpu.VMEM(w_out.shape[1:], BF16)]
    params = pltpu.CompilerParams(dimension_semantics=("arbitrary",), vmem_limit_bytes=VMEM_LIMIT_BYTES)
    in_specs = [x_spec] + [_layer_spec(c, layer) for c in consts]

    mats = [m for m, _ in side_weights]
    cast_layers = tuple(l for _, l in side_weights)
    hbm = pl.BlockSpec(memory_space=pl.ANY)
    bf16_rows = 2 * SUBLANES
    n_chunks = max(n for n in range(1, n_tiles + 1)
                   if all((m.shape[1] * m.shape[2]) % n == 0
                          and m.shape[2] % ((m.shape[1] * m.shape[2]) // n) == 0
                          and ((m.shape[1] * m.shape[2]) // n) % bf16_rows == 0 for m in mats))
    chunk_rows = [m.shape[1] * m.shape[2] // n_chunks for m in mats]
    cast_scratch = ([pltpu.VMEM((rows, m.shape[3]), F32) for m, rows in zip(mats, chunk_rows)]
                    + [pltpu.VMEM((rows, m.shape[3]), BF16) for m, rows in zip(mats, chunk_rows)]
                    + [pltpu.SemaphoreType.DMA((len(mats),)), pltpu.SemaphoreType.DMA((len(mats),))])
    out, *mats_bf16 = pl.pallas_call(
        functools.partial(_mixer_kernel, alpha, S // ts, cast_layers, n_chunks),
        grid=(n_tiles + 1,), in_specs=in_specs + [hbm] * len(mats),
        out_specs=[o_spec] + [hbm] * len(mats),
        out_shape=[o_shape] + [jax.ShapeDtypeStruct(m.shape[1:], BF16) for m in mats],
        scratch_shapes=scratch + cast_scratch, compiler_params=params, name="mixer",
    )(x.reshape(B * S, D), *consts, *mats)
    return out.reshape(B, S, D), mats_bf16


def _norm_and_embed(alpha, x, f, p, ln_g, ln_b, gate_w, ple_w):
    h = _layer_norm(alpha * x + f, ln_g, ln_b)
    gate = _sigmoid(_dot(h.astype(BF16), gate_w))
    return h + gate * _dot(p.astype(BF16), ple_w)


def _dense_ffn_kernel(alpha, x_ref, p_ref, w1_ref, w3_ref, w2_ref, ln_g_ref, ln_b_ref, gate_w_ref, ple_w_ref,
                      o_ref, acc_ref):
    x = x_ref[...]
    xb = x.astype(BF16)
    ff = w1_ref.shape[1]
    for c0 in range(0, ff, FF_CHUNK_DENSE):
        c1 = min(c0 + FF_CHUNK_DENSE, ff)
        a = _dot(xb, w1_ref[:, c0:c1])
        b = _dot(xb, w3_ref[:, c0:c1])
        h = (a * _sigmoid(a) * b).astype(BF16)
        part = _dot(h, w2_ref[c0:c1, :])
        if c0 == 0:
            acc_ref[...] = part
        else:
            acc_ref[...] += part
    o_ref[...] = _norm_and_embed(alpha, x, acc_ref[...], p_ref[...], ln_g_ref[...], ln_b_ref[...],
                                 gate_w_ref[...], ple_w_ref[...])


def _dense_layer(alpha, layer, x, p, w1, w3, w2, ln_g, ln_b, gate_w, ple_w):
    T, D = x.shape
    tm = TM_DENSE
    assert T % tm == 0
    ln_g, ln_b = _per_layer(ln_g), _per_layer(ln_b)
    return pl.pallas_call(
        functools.partial(_dense_ffn_kernel, alpha),
        grid=(T // tm,),
        in_specs=[pl.BlockSpec((tm, D), lambda i: (i, 0)),
                  pl.BlockSpec((None, tm, D_PLE), lambda i: (layer, i, 0)),
                  _layer_spec(w1, 0), _layer_spec(w3, 0), _layer_spec(w2, 0),
                  _layer_spec(ln_g, layer), _layer_spec(ln_b, layer),
                  _layer_spec(gate_w, 0), _layer_spec(ple_w, 0)],
        out_specs=pl.BlockSpec((tm, D), lambda i: (i, 0)),
        out_shape=jax.ShapeDtypeStruct((T, D), F32),
        scratch_shapes=[pltpu.VMEM((tm, D), F32)],
        compiler_params=pltpu.CompilerParams(
            dimension_semantics=("arbitrary",), vmem_limit_bytes=VMEM_LIMIT_BYTES),
        name="dense_ffn",
    )(x, p, w1, w3, w2, ln_g, ln_b, gate_w, ple_w)


def _router_kernel(x_ref, wt_ref, idx_ref, gate_ref, cnt_ref, carry_ref):
    i = pl.program_id(0)
    tr = x_ref.shape[0]

    @pl.when(i == 0)
    def _():
        carry_ref[...] = jnp.zeros_like(carry_ref)

    x = x_ref[...]
    wt = wt_ref[...]
    xh = x.astype(BF16)
    xl = (x - xh.astype(F32)).astype(BF16)
    wh = wt.astype(BF16)
    wl = (wt - wh.astype(F32)).astype(BF16)
    nt_dims = (((1,), (1,)), ((), ()))
    dg = lambda a, b: lax.dot_general(a, b, nt_dims, preferred_element_type=F32)
    with_xh = dg(jnp.concatenate([wh, wl], axis=0), xh)
    logits = with_xh[:N_EXPERTS] + (dg(wh, xl) + with_xh[N_EXPERTS:])

    eid = lax.broadcasted_iota(jnp.int32, logits.shape, 0)
    m1 = jnp.max(logits, axis=0, keepdims=True)
    i1 = jnp.min(jnp.where(logits == m1, eid, N_EXPERTS), axis=0, keepdims=True)
    rest = jnp.where(eid == i1, -jnp.inf, logits)
    m2 = jnp.max(rest, axis=0, keepdims=True)
    i2 = jnp.min(jnp.where(rest == m2, eid, N_EXPERTS), axis=0, keepdims=True)
    e2 = jnp.exp(m2 - m1)
    g1 = 1.0 / (1.0 + e2)
    g2 = e2 / (1.0 + e2)

    oh1 = (eid == i1).astype(F32)
    oh2 = (eid == i2).astype(F32)
    chosen = oh1 + oh2
    r_i = lax.broadcasted_iota(jnp.int32, (tr, tr), 0)
    c_i = lax.broadcasted_iota(jnp.int32, (tr, tr), 1)
    before = (r_i < c_i).astype(BF16)
    excl = _dot(chosen.astype(BF16), before) + carry_ref[:, 0:1]
    rank1 = jnp.sum(oh1 * excl, axis=0, keepdims=True)
    rank2 = jnp.sum(oh2 * excl, axis=0, keepdims=True)

    total = carry_ref[:, 0:1] + jnp.sum(chosen, axis=1, keepdims=True)
    carry_ref[...] = jnp.broadcast_to(total, carry_ref.shape)
    cnt_ref[...] = jnp.broadcast_to(total, cnt_ref.shape).astype(jnp.int32)

    zi = jnp.zeros((SUBLANES - 4, tr), jnp.int32)
    idx_ref[...] = jnp.concatenate(
        [i1, i2, rank1.astype(jnp.int32), rank2.astype(jnp.int32), zi], axis=0)
    gate_ref[...] = jnp.concatenate([g1, g2, jnp.zeros((SUBLANES - 2, tr), F32)], axis=0)


def _router(x, router_w):
    T, D = x.shape
    tr = TR_ROUTE
    return pl.pallas_call(
        _router_kernel,
        grid=(T // tr,),
        in_specs=[pl.BlockSpec((tr, D), lambda i: (i, 0)), _const_spec((N_EXPERTS, D))],
        out_specs=[pl.BlockSpec((SUBLANES, tr), lambda i: (0, i)),
                   pl.BlockSpec((SUBLANES, tr), lambda i: (0, i)),
                   pl.BlockSpec((N_EXPERTS, LANES), lambda i: (0, 0))],
        out_shape=[jax.ShapeDtypeStruct((SUBLANES, T), jnp.int32),
                   jax.ShapeDtypeStruct((SUBLANES, T), F32),
                   jax.ShapeDtypeStruct((N_EXPERTS, LANES), jnp.int32)],
        scratch_shapes=[pltpu.VMEM((N_EXPERTS, LANES), F32)],
        compiler_params=pltpu.CompilerParams(
            dimension_semantics=("arbitrary",), vmem_limit_bytes=VMEM_LIMIT_BYTES),
        name="router",
    )(x, router_w.T)


def _store_token_tiles(dst_ref, value):
    rows = value.shape[0]
    for c in range(ROW_CHUNKS):
        dst_ref[pl.ds(c, rows, stride=ROW_CHUNKS), :] = value[:, c * LANES:(c + 1) * LANES]


def _load_token_tile_chunk(src_ref, rows, c):
    return src_ref[pl.ds(c, rows, stride=ROW_CHUNKS), :]


def _load_token_tiles(src_ref, rows):
    return jnp.concatenate([_load_token_tile_chunk(src_ref, rows, c) for c in range(ROW_CHUNKS)], axis=-1)


def _tile_rows(row):
    return pl.ds(pl.multiple_of(row * ROW_CHUNKS, ROW_CHUNKS), ROW_CHUNKS)


def _dispatch_kernel(dest_ref, fill_ref, x_ref, xs_hbm, xt_ref, zero_ref, sems, fill_sem):
    i = pl.program_id(0)
    n = pl.num_programs(0)
    tc = x_ref.shape[0]
    n_tok = n * tc
    base = i * tc
    slot = i % 2
    stage = xt_ref.at[slot]

    @pl.when(i == 0)
    def _():
        zero_ref[...] = jnp.zeros_like(zero_ref)
        tile_rows = zero_ref.shape[0]

        def fill(j):
            start = pl.multiple_of(fill_ref[j] * tile_rows, tile_rows)
            return pltpu.make_async_copy(zero_ref, xs_hbm.at[pl.ds(start, tile_rows), :], fill_sem)

        for j in range(fill_ref.shape[0]):
            @pl.when(fill_ref[j] >= 0)
            def _():
                fill(j).start()

        for j in range(fill_ref.shape[0]):
            @pl.when(fill_ref[j] >= 0)
            def _():
                fill(j).wait()

    _store_token_tiles(stage, x_ref[...])

    def copy(k, r):
        d = dest_ref[k * n_tok + base + r]
        return pltpu.make_async_copy(stage.at[_tile_rows(r), :], xs_hbm.at[_tile_rows(d), :], sems.at[slot])

    def issue(rb, c):
        for j in range(ISSUE_UNROLL):
            r = rb * ISSUE_UNROLL + j
            for k in range(TOP_K):
                copy(k, r).start(priority=k)
        return c

    lax.fori_loop(0, tc // ISSUE_UNROLL, issue, 0)

    def wait_step(s):
        for _ in range(TOP_K):
            pltpu.make_async_copy(xt_ref.at[s], xs_hbm.at[pl.ds(0, tc * ROW_CHUNKS), :], sems.at[s]).wait()

    @pl.when(i > 0)
    def _():
        wait_step(1 - slot)

    @pl.when(i == n - 1)
    def _():
        wait_step(slot)


def _dispatch(x, dest, fill_tiles, n_rows):
    T, D = x.shape
    tc = TC_DISPATCH
    assert T % tc == 0 and tc % ISSUE_UNROLL == 0 and D == ROW_CHUNKS * LANES and ROW_CHUNKS == SUBLANES
    return pl.pallas_call(
        _dispatch_kernel,
        grid_spec=pltpu.PrefetchScalarGridSpec(
            num_scalar_prefetch=2, grid=(T // tc,),
            in_specs=[pl.BlockSpec((tc, D), lambda i, d, ft: (i, 0))],
            out_specs=pl.BlockSpec(memory_space=pl.ANY),
            scratch_shapes=[pltpu.VMEM((2, tc * ROW_CHUNKS, LANES), F32),
                            pltpu.VMEM((TM_GROUP * ROW_CHUNKS, LANES), F32),
                            pltpu.SemaphoreType.DMA((2,)), pltpu.SemaphoreType.DMA(())]),
        out_shape=jax.ShapeDtypeStruct((n_rows * ROW_CHUNKS, LANES), F32),
        compiler_params=pltpu.CompilerParams(dimension_semantics=("arbitrary",)),
        name="dispatch",
    )(dest, fill_tiles, x)


def _expert_ffn_kernel(nf, te_ref, valid_ref, nt_ref, xs_ref, w1_ref, w3_ref, w2_ref, o_ref, xb_ref, acc_ref):
    i = pl.program_id(0)
    f = pl.program_id(1)
    tm = acc_ref.shape[0]

    def run(m, first, last):
        if first:
            for c in range(ROW_CHUNKS):
                chunk = _load_token_tile_chunk(xs_ref.at[pl.ds(0, m * ROW_CHUNKS), :], m, c)
                xb_ref[0:m, c * LANES:(c + 1) * LANES] = chunk.astype(BF16)

        xb = xb_ref[0:m, :]
        a = _dot(xb, w1_ref[...])
        b = _dot(xb, w3_ref[...])
        h = (a * _sigmoid(a) * b).astype(BF16)
        part = _dot(h, w2_ref[...])

        if last:
            total = part if first else acc_ref[0:m, :] + part
            _store_token_tiles(o_ref.at[pl.ds(0, m * ROW_CHUNKS), :], total)
            if m < tm:
                o_ref[m * ROW_CHUNKS:, :] = jnp.zeros(((tm - m) * ROW_CHUNKS, LANES), F32)
        elif first:
            acc_ref[0:m, :] = part
        else:
            acc_ref[0:m, :] += part

    live = i < nt_ref[0]
    few = valid_ref[i] <= tm // 2
    for m, size_matches in ((tm, jnp.logical_not(few)), (tm // 2, few)):
        for chunk in range(nf):
            @pl.when(live & size_matches & (f == chunk))
            def _():
                run(m, chunk == 0, chunk == nf - 1)

    @pl.when(jnp.logical_not(live) & (f == 0))
    def _():
        o_ref[...] = jnp.zeros_like(o_ref)


def _expert_ffn(xs, te, valid, nt, w1, w3, w2):
    E, D, FF = w1.shape
    tm, cf = TM_GROUP, FF_CHUNK_MOE
    nf = FF // cf
    assert E == N_EXPERTS and FF == nf * cf and tm % (4 * SUBLANES) == 0
    max_tiles = xs.shape[0] // (tm * ROW_CHUNKS)

    def tile(i, nt_ref):
        return jnp.minimum(i, nt_ref[0] - 1)

    def chunk(i, f, nt_ref):
        return jnp.where(i < nt_ref[0], f, nf - 1)

    def w13_map(i, f, te, valid, nt):
        return (te[tile(i, nt)], 0, chunk(i, f, nt))

    def w2_map(i, f, te, valid, nt):
        return (te[tile(i, nt)], chunk(i, f, nt), 0)

    return pl.pallas_call(
        functools.partial(_expert_ffn_kernel, nf),
        grid_spec=pltpu.PrefetchScalarGridSpec(
            num_scalar_prefetch=3, grid=(max_tiles, nf),
            in_specs=[pl.BlockSpec((tm * ROW_CHUNKS, LANES), lambda i, f, te, valid, nt: (tile(i, nt), 0)),
                      pl.BlockSpec((None, D, cf), w13_map),
                      pl.BlockSpec((None, D, cf), w13_map),
                      pl.BlockSpec((None, cf, D), w2_map)],
            out_specs=pl.BlockSpec((tm * ROW_CHUNKS, LANES), lambda i, f, te, valid, nt: (i, 0)),
            scratch_shapes=[pltpu.VMEM((tm, D), BF16), pltpu.VMEM((tm, D), F32)]),
        out_shape=jax.ShapeDtypeStruct(xs.shape, F32),
        compiler_params=pltpu.CompilerParams(
            dimension_semantics=("arbitrary", "arbitrary"), vmem_limit_bytes=VMEM_LIMIT_BYTES),
        name="expert_ffn",
    )(te, valid, nt, xs, w1, w3, w2)


def _sorted_layout(cnt, n_assign):
    tm = TM_GROUP
    max_tiles = n_assign // tm + N_EXPERTS
    tiles_e = (cnt + tm - 1) // tm
    tile_end = jnp.cumsum(tiles_e)
    row_start = (tile_end - tiles_e) * tm
    nt = tile_end[-1]
    tile_ids = jnp.minimum(jnp.arange(max_tiles, dtype=jnp.int32), nt - 1)
    te = jnp.sum((tile_end[None, :] <= tile_ids[:, None]).astype(jnp.int32), axis=1)
    te = jnp.minimum(te, N_EXPERTS - 1)
    owner = te[:, None] == jnp.arange(N_EXPERTS, dtype=jnp.int32)[None, :]
    rows_left = cnt[None, :] - (tile_ids[:, None] - (tile_end - tiles_e)[None, :]) * tm
    valid = jnp.clip(jnp.sum(jnp.where(owner, rows_left, 0), axis=1), 0, tm).astype(jnp.int32)
    last_tile = jnp.where(tiles_e > 0, tile_end - 1, -1)
    tail = nt + jnp.arange(N_EXPERTS, dtype=jnp.int32)
    tail = jnp.where(tail < max_tiles, tail, -1)
    fill_tiles = jnp.concatenate([last_tile, tail]).astype(jnp.int32)
    return row_start, te, valid, nt.reshape(1).astype(jnp.int32), fill_tiles, max_tiles * tm


def _combine_kernel(alpha, dest_ref, x_ref, p_ref, gate_ref, ys_hbm, ln_g_ref, ln_b_ref, gate_w_f32, ple_w_f32,
                    o_ref, yb_ref, gate_w_ref, ple_w_ref, sems):
    i = pl.program_id(0)
    n = pl.num_programs(0)
    tc = x_ref.shape[0]
    n_tok = n * tc
    slot = i % 2

    @pl.when(i == 0)
    def _():
        gate_w_ref[...] = gate_w_f32[...].astype(BF16)
        ple_w_ref[...] = ple_w_f32[...].astype(BF16)

    def row_copy(step, s, k, r):
        d = dest_ref[k * n_tok + step * tc + r]
        return pltpu.make_async_copy(ys_hbm.at[_tile_rows(d), :], yb_ref.at[s, k, _tile_rows(r), :], sems.at[s])

    def wait_rows(s):
        for k in range(TOP_K):
            pltpu.make_async_copy(ys_hbm.at[pl.ds(0, tc * ROW_CHUNKS), :], yb_ref.at[s, k], sems.at[s]).wait()

    @pl.when(i == 0)
    def _():
        def issue(rb, c):
            for j in range(ISSUE_UNROLL):
                for k in range(TOP_K):
                    row_copy(0, 0, k, rb * ISSUE_UNROLL + j).start(priority=k)
            return c

        lax.fori_loop(0, tc // ISSUE_UNROLL, issue, 0)

    wait_rows(slot)

    nxt = jnp.minimum(i + 1, n - 1)
    for r in range(tc):
        for k in range(TOP_K):
            row_copy(nxt, 1 - slot, k, r).start(priority=k)

    g = gate_ref[...]
    f = (g[:, 0:1] * _load_token_tiles(yb_ref.at[slot, 0], tc)
         + g[:, 1:2] * _load_token_tiles(yb_ref.at[slot, 1], tc))
    o_ref[...] = _norm_and_embed(alpha, x_ref[...], f, p_ref[...], ln_g_ref[...], ln_b_ref[...],
                                 gate_w_ref[...], ple_w_ref[...])

    @pl.when(i == n - 1)
    def _():
        wait_rows(1 - slot)


def _combine_layer(alpha, layer, x, p, gates, dest, ys, ln_g, ln_b, gate_w, ple_w):
    T, D = x.shape
    tc = TC_COMBINE
    assert T % tc == 0 and tc % ISSUE_UNROLL == 0
    consts = [_per_layer(c) for c in (ln_g, ln_b, gate_w, ple_w)]
    return pl.pallas_call(
        functools.partial(_combine_kernel, alpha),
        grid_spec=pltpu.PrefetchScalarGridSpec(
            num_scalar_prefetch=1, grid=(T // tc,),
            in_specs=[pl.BlockSpec((tc, D), lambda i, d: (i, 0)),
                      pl.BlockSpec((None, tc, D_PLE), lambda i, d: (layer, i, 0)),
                      pl.BlockSpec((tc, TOP_K), lambda i, d: (i, 0)),
                      pl.BlockSpec(memory_space=pl.ANY)]
                     + [_layer_spec(c, layer) for c in consts],
            out_specs=pl.BlockSpec((tc, D), lambda i, d: (i, 0)),
            scratch_shapes=[pltpu.VMEM((2, TOP_K, tc * ROW_CHUNKS, LANES), F32),
                            pltpu.VMEM((D, D), BF16), pltpu.VMEM((D_PLE, D), BF16),
                            pltpu.SemaphoreType.DMA((2,))]),
        out_shape=jax.ShapeDtypeStruct((T, D), F32),
        compiler_params=pltpu.CompilerParams(
            dimension_semantics=("arbitrary",), vmem_limit_bytes=VMEM_LIMIT_BYTES),
        name="combine",
    )(dest, x, p, gates, ys, *consts)


def _moe_layer(alpha, layer, x, p, router_w, w1, w3, w2, ln_g, ln_b, gate_w, ple_w):
    T, D = x.shape
    route_i, route_g, counts = _router(x, router_w)
    idx = route_i[0:TOP_K]
    rank = route_i[TOP_K:2 * TOP_K]
    gates = route_g[0:TOP_K].T

    start, te, valid, nt, fill_tiles, n_rows = _sorted_layout(counts[:, 0], TOP_K * T)
    start_of = sum(jnp.where(idx == e, start[e], 0) for e in range(N_EXPERTS))
    dest = (start_of + rank).astype(jnp.int32).reshape(-1)

    xs = _dispatch(x, dest, fill_tiles, n_rows)
    ys = _expert_ffn(xs, te, valid, nt, w1, w3, w2)
    return _combine_layer(alpha, layer, x, p, gates, dest, ys, ln_g, ln_b, gate_w, ple_w)


def kernel(x, p, w_in, pool_w, pool_scale, conv_w, conv_b, conv_ln_g, conv_ln_b, conv_pw, w_out, ln1_g, ln1_b,
           dense_w1, dense_w3, dense_w2, router_w, exp_w1, exp_w3, exp_w2, ln2_g, ln2_b, ple_gate_w, ple_w):
    depth = w_in.shape[0]
    alpha = (2.0 * depth) ** 0.25
    B, S, D = x.shape
    pt = p.reshape(depth, B * S, D_PLE)
    one_group = lambda m: m.reshape(m.shape[0], 1, *m.shape[1:])
    for i in range(depth):
        j = i // 2
        is_dense = i % 2 == 0
        if is_dense:
            side = [(one_group(dense_w1), j), (one_group(dense_w3), j), (one_group(dense_w2), j),
                    (one_group(ple_gate_w), i), (one_group(ple_w), i)]
        else:
            side = [(exp_w1, j), (exp_w3, j), (exp_w2, j)]
        x, side_bf16 = _mixer(alpha, i, x, w_in, pool_w, pool_scale, conv_w, conv_b, conv_ln_g, conv_ln_b,
                              conv_pw, w_out, ln1_g, ln1_b, side)
        xt = x.reshape(B * S, D)
        if is_dense:
            w1b, w3b, w2b, gate_b, ple_b = side_bf16
            xt = _dense_layer(alpha, i, xt, pt, w1b, w3b, w2b, ln2_g, ln2_b, gate_b, ple_b)
        else:
            xt = _moe_layer(alpha, i, xt, pt, router_w[j], *side_bf16, ln2_g, ln2_b, ple_gate_w, ple_w)
        x = xt.reshape(B, S, D)
    return x
```

```python
import functools

import jax
import jax.numpy as jnp
from jax import lax
from jax.experimental import pallas as pl
from jax.experimental.pallas import tpu as pltpu

D_MODEL = 1024
D_PLE = 256
D_POOL = 512
D_CONV = D_MODEL - D_POOL
POOL_WINDOWS = (2, 4, 8, 16)
POOL_GROUP_DIM = D_POOL // len(POOL_WINDOWS)
CONV_WIDTH = 31
D_IN_PROJ = D_POOL + 2 * D_CONV
N_EXPERTS = 8
TOP_K = 2
LN_EPS = 1e-5

F32 = jnp.float32
BF16 = jnp.bfloat16

SUBLANES = 8
LANES = 128
VMEM_LIMIT_BYTES = 56 * 1024 * 1024

HALO = 32
TS_MIX = 512
RB_CONV = 128
IN_PROJ_CHUNK = 256
MIX_ROW_GROUPS = 2
TM_DENSE = 512
FF_CHUNK_DENSE = 1024
TR_ROUTE = 512
TM_GROUP = 512
FF_CHUNK_MOE = 1792
TC_DISPATCH = 512
TC_COMBINE = 256
ROW_CHUNKS = D_MODEL // LANES
ISSUE_UNROLL = 8


def _sigmoid(z):
    return 1.0 / (1.0 + jnp.exp(-z))


def _layer_norm(h, g, b):
    mu = jnp.mean(h, axis=-1, keepdims=True)
    c = h - mu
    var = jnp.mean(c * c, axis=-1, keepdims=True)
    return c * lax.rsqrt(var + LN_EPS) * g + b


def _dot(a, b):
    return jnp.dot(a, b, preferred_element_type=F32)


def _side_cast_copies(load, layers, q, hbm, bufs, sems):
    copies = []
    for a in range(len(hbm)):
        rows = bufs[a].shape[0]
        per_group = hbm[a].shape[-2] // rows
        g = q // per_group
        r0 = pl.multiple_of((q % per_group) * rows, rows)
        if load:
            copies.append(pltpu.make_async_copy(hbm[a].at[layers[a], g, pl.ds(r0, rows), :], bufs[a], sems.at[a]))
        else:
            copies.append(pltpu.make_async_copy(bufs[a], hbm[a].at[g, pl.ds(r0, rows), :], sems.at[a]))
    return copies


def _mixer_kernel(alpha, tiles_per_seq, cast_layers, n_cast_chunks, x_ref, w_in_f32, pool_w_f32, pool_scale_ref,
                  conv_w_ref, conv_b_ref, cln_g_ref, cln_b_ref, conv_pw_f32, w_out_f32, ln_g_ref, ln_b_ref, *rest):
    n_side = len(cast_layers)
    side_f32, rest = rest[:n_side], rest[n_side:]
    o_ref, side_bf16, rest = rest[0], rest[1:1 + n_side], rest[1 + n_side:]
    (p_scr, v_scr, u_scr, y_scr, ypool_scr, xprev_scr, w_in_ref, pool_w_ref, conv_pw_ref, w_out_ref) = rest[:10]
    cast_scratch = rest[10:]
    ts = x_ref.shape[0]
    t = pl.program_id(0)

    if n_side:
        in_bufs, out_bufs = cast_scratch[:n_side], cast_scratch[n_side:2 * n_side]
        in_sems, out_sems = cast_scratch[2 * n_side:]

        def chunk_loads(q):
            return _side_cast_copies(True, cast_layers, q, side_f32, in_bufs, in_sems)

        def chunk_stores(q):
            return _side_cast_copies(False, cast_layers, q, side_bf16, out_bufs, out_sems)

        @pl.when(t == 0)
        def _():
            for load in chunk_loads(0):
                load.start()

    @pl.when(t == 0)
    def _():
        w_in_ref[...] = w_in_f32[...].astype(BF16)
        pool_w_ref[...] = pool_w_f32[...].astype(BF16)
        conv_pw_ref[...] = conv_pw_f32[...].astype(BF16)
        w_out_ref[...] = w_out_f32[...].astype(BF16)
        p_scr[...] = jnp.zeros_like(p_scr)
        v_scr[...] = jnp.zeros_like(v_scr)

    def step_body(finish_prev, begin_this):
        n_conv_blocks = ts // RB_CONV
        in_proj_chunks = [(c0, min(c0 + IN_PROJ_CHUNK, D_IN_PROJ)) for c0 in range(0, D_IN_PROJ, IN_PROJ_CHUNK)]
        if begin_this:
            x = x_ref[...]
            xb = x.astype(BF16)

        blocks_per_group = n_conv_blocks // MIX_ROW_GROUPS
        for rb in range(n_conv_blocks):
            if finish_prev:
                r0 = rb * RB_CONV
                cols = []
                for lc in range(D_CONV // LANES):
                    lanes = slice(lc * LANES, (lc + 1) * LANES)
                    acc = jnp.zeros((RB_CONV, LANES), F32)
                    for k in range(CONV_WIDTH):
                        first = r0 + HALO - (CONV_WIDTH - 1 - k)
                        acc = acc + v_scr[lc, first:first + RB_CONV, :] * conv_w_ref[k:k + 1, lanes]
                    cols.append(acc)
                y = jnp.concatenate(cols, axis=-1) + conv_b_ref[...]
                z = _layer_norm(y, cln_g_ref[...], cln_b_ref[...])
                y_scr[r0:r0 + RB_CONV, :] = (z * _sigmoid(z)).astype(BF16)
            if begin_this and rb < len(in_proj_chunks):
                c0, c1 = in_proj_chunks[rb]
                u_scr[:, c0:c1] = _dot(xb, w_in_ref[:, c0:c1])
            if finish_prev and (rb + 1) % blocks_per_group == 0:
                rows = slice((rb + 1 - blocks_per_group) * RB_CONV, (rb + 1) * RB_CONV)
                y_conv = _dot(y_scr[rows, :], conv_pw_ref[...])
                heads = jnp.concatenate([ypool_scr[rows, :], y_conv.astype(BF16)], axis=-1)
                mix = _dot(heads, w_out_ref[...])
                o_ref[rows, :] = _layer_norm(alpha * xprev_scr[rows, :] + mix, ln_g_ref[...], ln_b_ref[...])
        if not begin_this:
            return
        for c0, c1 in in_proj_chunks[n_conv_blocks:]:
            u_scr[:, c0:c1] = _dot(xb, w_in_ref[:, c0:c1])

        tile_in_seq = t % tiles_per_seq
        starts_seq = tile_in_seq == 0
        p_scr[:, 0:HALO, :] = jnp.where(starts_seq, 0.0, p_scr[:, ts:ts + HALO, :])
        v_scr[:, 0:HALO, :] = jnp.where(starts_seq, 0.0, v_scr[:, ts:ts + HALO, :])
        for c in range(D_CONV // LANES):
            val = u_scr[:, D_POOL + c * LANES:D_POOL + (c + 1) * LANES]
            gate = u_scr[:, D_POOL + D_CONV + c * LANES:D_POOL + D_CONV + (c + 1) * LANES]
            v_scr[c, HALO:, :] = val * _sigmoid(gate)

        pos = tile_in_seq * ts + lax.broadcasted_iota(jnp.int32, (ts, POOL_GROUP_DIM), 0)
        for g, w in enumerate(POOL_WINDOWS):
            lanes = slice(g * POOL_GROUP_DIM, (g + 1) * POOL_GROUP_DIM)
            cur = u_scr[:, lanes]
            p_scr[g, HALO:, :] = cur
            win = cur
            for back in range(1, w):
                win = win + p_scr[g, HALO - back:HALO - back + ts, :]
            cnt = jnp.minimum(pos + 1, w).astype(F32)
            d = win / cnt - cur
            yg = _dot(d.astype(BF16), pool_w_ref[g]) * pool_scale_ref[:, lanes]
            ypool_scr[:, lanes] = yg.astype(BF16)

        xprev_scr[...] = x

    last = pl.num_programs(0) - 1
    for finish_prev, begin_this, cond in ((False, True, t == 0), (True, True, (t > 0) & (t < last)),
                                          (True, False, t == last)):
        @pl.when(cond)
        def _():
            step_body(finish_prev, begin_this)

    if n_side:
        @pl.when(t < n_cast_chunks)
        def _():
            for load in chunk_loads(t):
                load.wait()

            @pl.when(t > 0)
            def _():
                for store in chunk_stores(t - 1):
                    store.wait()

            for src, dst in zip(in_bufs, out_bufs):
                dst[...] = src[...].astype(BF16)
            for store in chunk_stores(t):
                store.start()

            @pl.when(t + 1 < n_cast_chunks)
            def _():
                for load in chunk_loads(t + 1):
                    load.start()

        @pl.when(t == n_cast_chunks)
        def _():
            for store in chunk_stores(t - 1):
                store.wait()


def _const_spec(shape):
    nd = len(shape)
    return pl.BlockSpec(shape, lambda *_: (0,) * nd, pipeline_mode=pl.Buffered(1))


def _per_layer(v):
    return v.reshape(v.shape[0], 1, v.shape[1]) if v.ndim == 2 else v


def _layer_spec(v, layer, **kwargs):
    nd = v.ndim
    return pl.BlockSpec((None,) + v.shape[1:], lambda *_: (layer,) + (0,) * (nd - 1),
                        pipeline_mode=pl.Buffered(1), **kwargs)


def _mixer(alpha, layer, x, w_in, pool_w, pool_scale, conv_w, conv_b, cln_g, cln_b, conv_pw, w_out, ln_g, ln_b,
           side_weights):
    B, S, D = x.shape
    ts = TS_MIX
    n_tiles = B * S // ts
    assert D == D_MODEL and S % ts == 0 and POOL_GROUP_DIM == LANES and D_CONV % LANES == 0
    assert ts % (RB_CONV * MIX_ROW_GROUPS) == 0 and HALO >= max(CONV_WIDTH, *POOL_WINDOWS) - 1
    consts = [_per_layer(c) for c in (w_in, pool_w, pool_scale, conv_w, conv_b, cln_g, cln_b, conv_pw, w_out,
                                      ln_g, ln_b)]
    x_spec = pl.BlockSpec((ts, D), lambda t: (jnp.minimum(t, n_tiles - 1), 0))
    o_spec = pl.BlockSpec((ts, D), lambda t: (jnp.maximum(t - 1, 0), 0))
    o_shape = jax.ShapeDtypeStruct((B * S, D), F32)
    scratch = [pltpu.VMEM((D_POOL // LANES, HALO + ts, LANES), F32),
               pltpu.VMEM((D_CONV // LANES, HALO + ts, LANES), F32),
               pltpu.VMEM((ts, D_IN_PROJ), F32),
               pltpu.VMEM((ts, D_CONV), BF16),
               pltpu.VMEM((ts, D_POOL), BF16),
               pltpu.VMEM((ts, D), F32),
               pltpu.VMEM(w_in.shape[1:], BF16), pltpu.VMEM(pool_w.shape[1:], BF16),
               pltpu.VMEM(conv_pw.shape[1:], BF16), pltpu.VMEM(w_out.shape[1:], BF16)]
    params = pltpu.CompilerParams(dimension_semantics=("arbitrary",), vmem_limit_bytes=VMEM_LIMIT_BYTES)
    in_specs = [x_spec] + [_layer_spec(c, layer) for c in consts]

    mats = [m for m, _ in side_weights]
    cast_layers = tuple(l for _, l in side_weights)
    hbm = pl.BlockSpec(memory_space=pl.ANY)
    bf16_rows = 2 * SUBLANES
    n_chunks = max(n for n in range(1, n_tiles + 1)
                   if all((m.shape[1] * m.shape[2]) % n == 0
                          and m.shape[2] % ((m.shape[1] * m.shape[2]) // n) == 0
                          and ((m.shape[1] * m.shape[2]) // n) % bf16_rows == 0 for m in mats))
    chunk_rows = [m.shape[1] * m.shape[2] // n_chunks for m in mats]
    cast_scratch = ([pltpu.VMEM((rows, m.shape[3]), F32) for m, rows in zip(mats, chunk_rows)]
                    + [pltpu.VMEM((rows, m.shape[3]), BF16) for m, rows in zip(mats, chunk_rows)]
                    + [pltpu.SemaphoreType.DMA((len(mats),)), pltpu.SemaphoreType.DMA((len(mats),))])
    out, *mats_bf16 = pl.pallas_call(
        functools.partial(_mixer_kernel, alpha, S // ts, cast_layers, n_chunks),
        grid=(n_tiles + 1,), in_specs=in_specs + [hbm] * len(mats),
        out_specs=[o_spec] + [hbm] * len(mats),
        out_shape=[o_shape] + [jax.ShapeDtypeStruct(m.shape[1:], BF16) for m in mats],
        scratch_shapes=scratch + cast_scratch, compiler_params=params, name="mixer",
    )(x.reshape(B * S, D), *consts, *mats)
    return out.reshape(B, S, D), mats_bf16


def _norm_and_embed(alpha, x, f, p, ln_g, ln_b, gate_w, ple_w):
    h = _layer_norm(alpha * x + f, ln_g, ln_b)
    gate = _sigmoid(_dot(h.astype(BF16), gate_w))
    return h + gate * _dot(p.astype(BF16), ple_w)


def _dense_ffn_kernel(alpha, x_ref, p_ref, w1_ref, w3_ref, w2_ref, ln_g_ref, ln_b_ref, gate_w_ref, ple_w_ref,
                      o_ref, acc_ref):
    x = x_ref[...]
    xb = x.astype(BF16)
    ff = w1_ref.shape[1]
    for c0 in range(0, ff, FF_CHUNK_DENSE):
        c1 = min(c0 + FF_CHUNK_DENSE, ff)
        a = _dot(xb, w1_ref[:, c0:c1])
        b = _dot(xb, w3_ref[:, c0:c1])
        h = (a * _sigmoid(a) * b).astype(BF16)
        part = _dot(h, w2_ref[c0:c1, :])
        if c0 == 0:
            acc_ref[...] = part
        else:
            acc_ref[...] += part
    o_ref[...] = _norm_and_embed(alpha, x, acc_ref[...], p_ref[...], ln_g_ref[...], ln_b_ref[...],
                                 gate_w_ref[...], ple_w_ref[...])


def _dense_layer(alpha, layer, x, p, w1, w3, w2, ln_g, ln_b, gate_w, ple_w):
    T, D = x.shape
    tm = TM_DENSE
    assert T % tm == 0
    ln_g, ln_b = _per_layer(ln_g), _per_layer(ln_b)
    return pl.pallas_call(
        functools.partial(_dense_ffn_kernel, alpha),
        grid=(T // tm,),
        in_specs=[pl.BlockSpec((tm, D), lambda i: (i, 0)),
                  pl.BlockSpec((None, tm, D_PLE), lambda i: (layer, i, 0)),
                  _layer_spec(w1, 0), _layer_spec(w3, 0), _layer_spec(w2, 0),
                  _layer_spec(ln_g, layer), _layer_spec(ln_b, layer),
                  _layer_spec(gate_w, 0), _layer_spec(ple_w, 0)],
        out_specs=pl.BlockSpec((tm, D), lambda i: (i, 0)),
        out_shape=jax.ShapeDtypeStruct((T, D), F32),
        scratch_shapes=[pltpu.VMEM((tm, D), F32)],
        compiler_params=pltpu.CompilerParams(
            dimension_semantics=("arbitrary",), vmem_limit_bytes=VMEM_LIMIT_BYTES),
        name="dense_ffn",
    )(x, p, w1, w3, w2, ln_g, ln_b, gate_w, ple_w)


def _router_kernel(x_ref, wt_ref, idx_ref, gate_ref, cnt_ref, carry_ref):
    i = pl.program_id(0)
    tr = x_ref.shape[0]

    @pl.when(i == 0)
    def _():
        carry_ref[...] = jnp.zeros_like(carry_ref)

    x = x_ref[...]
    wt = wt_ref[...]
    xh = x.astype(BF16)
    xl = (x - xh.astype(F32)).astype(BF16)
    wh = wt.astype(BF16)
    wl = (wt - wh.astype(F32)).astype(BF16)
    nt_dims = (((1,), (1,)), ((), ()))
    dg = lambda a, b: lax.dot_general(a, b, nt_dims, preferred_element_type=F32)
    with_xh = dg(jnp.concatenate([wh, wl], axis=0), xh)
    logits = with_xh[:N_EXPERTS] + (dg(wh, xl) + with_xh[N_EXPERTS:])

    eid = lax.broadcasted_iota(jnp.int32, logits.shape, 0)
    m1 = jnp.max(logits, axis=0, keepdims=True)
    i1 = jnp.min(jnp.where(logits == m1, eid, N_EXPERTS), axis=0, keepdims=True)
    rest = jnp.where(eid == i1, -jnp.inf, logits)
    m2 = jnp.max(rest, axis=0, keepdims=True)
    i2 = jnp.min(jnp.where(rest == m2, eid, N_EXPERTS), axis=0, keepdims=True)
    e2 = jnp.exp(m2 - m1)
    g1 = 1.0 / (1.0 + e2)
    g2 = e2 / (1.0 + e2)

    oh1 = (eid == i1).astype(F32)
    oh2 = (eid == i2).astype(F32)
    chosen = oh1 + oh2
    r_i = lax.broadcasted_iota(jnp.int32, (tr, tr), 0)
    c_i = lax.broadcasted_iota(jnp.int32, (tr, tr), 1)
    before = (r_i < c_i).astype(BF16)
    excl = _dot(chosen.astype(BF16), before) + carry_ref[:, 0:1]
    rank1 = jnp.sum(oh1 * excl, axis=0, keepdims=True)
    rank2 = jnp.sum(oh2 * excl, axis=0, keepdims=True)

    total = carry_ref[:, 0:1] + jnp.sum(chosen, axis=1, keepdims=True)
    carry_ref[...] = jnp.broadcast_to(total, carry_ref.shape)
    cnt_ref[...] = jnp.broadcast_to(total, cnt_ref.shape).astype(jnp.int32)

    zi = jnp.zeros((SUBLANES - 4, tr), jnp.int32)
    idx_ref[...] = jnp.concatenate(
        [i1, i2, rank1.astype(jnp.int32), rank2.astype(jnp.int32), zi], axis=0)
    gate_ref[...] = jnp.concatenate([g1, g2, jnp.zeros((SUBLANES - 2, tr), F32)], axis=0)


def _router(x, router_w):
    T, D = x.shape
    tr = TR_ROUTE
    return pl.pallas_call(
        _router_kernel,
        grid=(T // tr,),
        in_specs=[pl.BlockSpec((tr, D), lambda i: (i, 0)), _const_spec((N_EXPERTS, D))],
        out_specs=[pl.BlockSpec((SUBLANES, tr), lambda i: (0, i)),
                   pl.BlockSpec((SUBLANES, tr), lambda i: (0, i)),
                   pl.BlockSpec((N_EXPERTS, LANES), lambda i: (0, 0))],
        out_shape=[jax.ShapeDtypeStruct((SUBLANES, T), jnp.int32),
                   jax.ShapeDtypeStruct((SUBLANES, T), F32),
                   jax.ShapeDtypeStruct((N_EXPERTS, LANES), jnp.int32)],
        scratch_shapes=[pltpu.VMEM((N_EXPERTS, LANES), F32)],
        compiler_params=pltpu.CompilerParams(
            dimension_semantics=("arbitrary",), vmem_limit_bytes=VMEM_LIMIT_BYTES),
        name="router",
    )(x, router_w.T)


def _store_token_tiles(dst_ref, value):
    rows = value.shape[0]
    for c in range(ROW_CHUNKS):
        dst_ref[pl.ds(c, rows, stride=ROW_CHUNKS), :] = value[:, c * LANES:(c + 1) * LANES]


def _load_token_tile_chunk(src_ref, rows, c):
    return src_ref[pl.ds(c, rows, stride=ROW_CHUNKS), :]


def _load_token_tiles(src_ref, rows):
    return jnp.concatenate([_load_token_tile_chunk(src_ref, rows, c) for c in range(ROW_CHUNKS)], axis=-1)


def _tile_rows(row):
    return pl.ds(pl.multiple_of(row * ROW_CHUNKS, ROW_CHUNKS), ROW_CHUNKS)


def _dispatch_kernel(dest_ref, fill_ref, x_ref, xs_hbm, xt_ref, zero_ref, sems, fill_sem):
    i = pl.program_id(0)
    n = pl.num_programs(0)
    tc = x_ref.shape[0]
    n_tok = n * tc
    base = i * tc
    slot = i % 2
    stage = xt_ref.at[slot]

    @pl.when(i == 0)
    def _():
        zero_ref[...] = jnp.zeros_like(zero_ref)
        tile_rows = zero_ref.shape[0]

        def fill(j):
            start = pl.multiple_of(fill_ref[j] * tile_rows, tile_rows)
            return pltpu.make_async_copy(zero_ref, xs_hbm.at[pl.ds(start, tile_rows), :], fill_sem)

        for j in range(fill_ref.shape[0]):
            @pl.when(fill_ref[j] >= 0)
            def _():
                fill(j).start()

        for j in range(fill_ref.shape[0]):
            @pl.when(fill_ref[j] >= 0)
            def _():
                fill(j).wait()

    _store_token_tiles(stage, x_ref[...])

    def copy(k, r):
        d = dest_ref[k * n_tok + base + r]
        return pltpu.make_async_copy(stage.at[_tile_rows(r), :], xs_hbm.at[_tile_rows(d), :], sems.at[slot])

    def issue(rb, c):
        for j in range(ISSUE_UNROLL):
            r = rb * ISSUE_UNROLL + j
            for k in range(TOP_K):
                copy(k, r).start(priority=k)
        return c

    lax.fori_loop(0, tc // ISSUE_UNROLL, issue, 0)

    def wait_step(s):
        for _ in range(TOP_K):
            pltpu.make_async_copy(xt_ref.at[s], xs_hbm.at[pl.ds(0, tc * ROW_CHUNKS), :], sems.at[s]).wait()

    @pl.when(i > 0)
    def _():
        wait_step(1 - slot)

    @pl.when(i == n - 1)
    def _():
        wait_step(slot)


def _dispatch(x, dest, fill_tiles, n_rows):
    T, D = x.shape
    tc = TC_DISPATCH
    assert T % tc == 0 and tc % ISSUE_UNROLL == 0 and D == ROW_CHUNKS * LANES and ROW_CHUNKS == SUBLANES
    return pl.pallas_call(
        _dispatch_kernel,
        grid_spec=pltpu.PrefetchScalarGridSpec(
            num_scalar_prefetch=2, grid=(T // tc,),
            in_specs=[pl.BlockSpec((tc, D), lambda i, d, ft: (i, 0))],
            out_specs=pl.BlockSpec(memory_space=pl.ANY),
            scratch_shapes=[pltpu.VMEM((2, tc * ROW_CHUNKS, LANES), F32),
                            pltpu.VMEM((TM_GROUP * ROW_CHUNKS, LANES), F32),
                            pltpu.SemaphoreType.DMA((2,)), pltpu.SemaphoreType.DMA(())]),
        out_shape=jax.ShapeDtypeStruct((n_rows * ROW_CHUNKS, LANES), F32),
        compiler_params=pltpu.CompilerParams(dimension_semantics=("arbitrary",)),
        name="dispatch",
    )(dest, fill_tiles, x)


def _expert_ffn_kernel(nf, te_ref, valid_ref, nt_ref, xs_ref, w1_ref, w3_ref, w2_ref, o_ref, xb_ref, acc_ref):
    i = pl.program_id(0)
    f = pl.program_id(1)
    tm = acc_ref.shape[0]

    def run(m, first, last):
        if first:
            for c in range(ROW_CHUNKS):
                chunk = _load_token_tile_chunk(xs_ref.at[pl.ds(0, m * ROW_CHUNKS), :], m, c)
                xb_ref[0:m, c * LANES:(c + 1) * LANES] = chunk.astype(BF16)

        xb = xb_ref[0:m, :]
        a = _dot(xb, w1_ref[...])
        b = _dot(xb, w3_ref[...])
        h = (a * _sigmoid(a) * b).astype(BF16)
        part = _dot(h, w2_ref[...])

        if last:
            total = part if first else acc_ref[0:m, :] + part
            _store_token_tiles(o_ref.at[pl.ds(0, m * ROW_CHUNKS), :], total)
            if m < tm:
                o_ref[m * ROW_CHUNKS:, :] = jnp.zeros(((tm - m) * ROW_CHUNKS, LANES), F32)
        elif first:
            acc_ref[0:m, :] = part
        else:
            acc_ref[0:m, :] += part

    live = i < nt_ref[0]
    few = valid_ref[i] <= tm // 2
    for m, size_matches in ((tm, jnp.logical_not(few)), (tm // 2, few)):
        for chunk in range(nf):
            @pl.when(live & size_matches & (f == chunk))
            def _():
                run(m, chunk == 0, chunk == nf - 1)

    @pl.when(jnp.logical_not(live) & (f == 0))
    def _():
        o_ref[...] = jnp.zeros_like(o_ref)


def _expert_ffn(xs, te, valid, nt, w1, w3, w2):
    E, D, FF = w1.shape
    tm, cf = TM_GROUP, FF_CHUNK_MOE
    nf = FF // cf
    assert E == N_EXPERTS and FF == nf * cf and tm % (4 * SUBLANES) == 0
    max_tiles = xs.shape[0] // (tm * ROW_CHUNKS)

    def tile(i, nt_ref):
        return jnp.minimum(i, nt_ref[0] - 1)

    def chunk(i, f, nt_ref):
        return jnp.where(i < nt_ref[0], f, nf - 1)

    def w13_map(i, f, te, valid, nt):
        return (te[tile(i, nt)], 0, chunk(i, f, nt))

    def w2_map(i, f, te, valid, nt):
        return (te[tile(i, nt)], chunk(i, f, nt), 0)

    return pl.pallas_call(
        functools.partial(_expert_ffn_kernel, nf),
        grid_spec=pltpu.PrefetchScalarGridSpec(
            num_scalar_prefetch=3, grid=(max_tiles, nf),
            in_specs=[pl.BlockSpec((tm * ROW_CHUNKS, LANES), lambda i, f, te, valid, nt: (tile(i, nt), 0)),
                      pl.BlockSpec((None, D, cf), w13_map),
                      pl.BlockSpec((None, D, cf), w13_map),
                      pl.BlockSpec((None, cf, D), w2_map)],
            out_specs=pl.BlockSpec((tm * ROW_CHUNKS, LANES), lambda i, f, te, valid, nt: (i, 0)),
            scratch_shapes=[pltpu.VMEM((tm, D), BF16), pltpu.VMEM((tm, D), F32)]),
        out_shape=jax.ShapeDtypeStruct(xs.shape, F32),
        compiler_params=pltpu.CompilerParams(
            dimension_semantics=("arbitrary", "arbitrary"), vmem_limit_bytes=VMEM_LIMIT_BYTES),
        name="expert_ffn",
    )(te, valid, nt, xs, w1, w3, w2)


def _sorted_layout(cnt, n_assign):
    tm = TM_GROUP
    max_tiles = n_assign // tm + N_EXPERTS
    tiles_e = (cnt + tm - 1) // tm
    tile_end = jnp.cumsum(tiles_e)
    row_start = (tile_end - tiles_e) * tm
    nt = tile_end[-1]
    tile_ids = jnp.minimum(jnp.arange(max_tiles, dtype=jnp.int32), nt - 1)
    te = jnp.sum((tile_end[None, :] <= tile_ids[:, None]).astype(jnp.int32), axis=1)
    te = jnp.minimum(te, N_EXPERTS - 1)
    owner = te[:, None] == jnp.arange(N_EXPERTS, dtype=jnp.int32)[None, :]
    rows_left = cnt[None, :] - (tile_ids[:, None] - (tile_end - tiles_e)[None, :]) * tm
    valid = jnp.clip(jnp.sum(jnp.where(owner, rows_left, 0), axis=1), 0, tm).astype(jnp.int32)
    last_tile = jnp.where(tiles_e > 0, tile_end - 1, -1)
    tail = nt + jnp.arange(N_EXPERTS, dtype=jnp.int32)
    tail = jnp.where(tail < max_tiles, tail, -1)
    fill_tiles = jnp.concatenate([last_tile, tail]).astype(jnp.int32)
    return row_start, te, valid, nt.reshape(1).astype(jnp.int32), fill_tiles, max_tiles * tm


def _combine_kernel(alpha, dest_ref, x_ref, p_ref, gate_ref, ys_hbm, ln_g_ref, ln_b_ref, gate_w_f32, ple_w_f32,
                    o_ref, yb_ref, gate_w_ref, ple_w_ref, sems):
    i = pl.program_id(0)
    n = pl.num_programs(0)
    tc = x_ref.shape[0]
    n_tok = n * tc
    slot = i % 2

    @pl.when(i == 0)
    def _():
        gate_w_ref[...] = gate_w_f32[...].astype(BF16)
        ple_w_ref[...] = ple_w_f32[...].astype(BF16)

    def row_copy(step, s, k, r):
        d = dest_ref[k * n_tok + step * tc + r]
        return pltpu.make_async_copy(ys_hbm.at[_tile_rows(d), :], yb_ref.at[s, k, _tile_rows(r), :], sems.at[s])

    def wait_rows(s):
        for k in range(TOP_K):
            pltpu.make_async_copy(ys_hbm.at[pl.ds(0, tc * ROW_CHUNKS), :], yb_ref.at[s, k], sems.at[s]).wait()

    @pl.when(i == 0)
    def _():
        def issue(rb, c):
            for j in range(ISSUE_UNROLL):
                for k in range(TOP_K):
                    row_copy(0, 0, k, rb * ISSUE_UNROLL + j).start(priority=k)
            return c

        lax.fori_loop(0, tc // ISSUE_UNROLL, issue, 0)

    wait_rows(slot)

    nxt = jnp.minimum(i + 1, n - 1)
    for r in range(tc):
        for k in range(TOP_K):
            row_copy(nxt, 1 - slot, k, r).start(priority=k)

    g = gate_ref[...]
    f = (g[:, 0:1] * _load_token_tiles(yb_ref.at[slot, 0], tc)
         + g[:, 1:2] * _load_token_tiles(yb_ref.at[slot, 1], tc))
    o_ref[...] = _norm_and_embed(alpha, x_ref[...], f, p_ref[...], ln_g_ref[...], ln_b_ref[...],
                                 gate_w_ref[...], ple_w_ref[...])

    @pl.when(i == n - 1)
    def _():
        wait_rows(1 - slot)


def _combine_layer(alpha, layer, x, p, gates, dest, ys, ln_g, ln_b, gate_w, ple_w):
    T, D = x.shape
    tc = TC_COMBINE
    assert T % tc == 0 and tc % ISSUE_UNROLL == 0
    consts = [_per_layer(c) for c in (ln_g, ln_b, gate_w, ple_w)]
    return pl.pallas_call(
        functools.partial(_combine_kernel, alpha),
        grid_spec=pltpu.PrefetchScalarGridSpec(
            num_scalar_prefetch=1, grid=(T // tc,),
            in_specs=[pl.BlockSpec((tc, D), lambda i, d: (i, 0)),
                      pl.BlockSpec((None, tc, D_PLE), lambda i, d: (layer, i, 0)),
                      pl.BlockSpec((tc, TOP_K), lambda i, d: (i, 0)),
                      pl.BlockSpec(memory_space=pl.ANY)]
                     + [_layer_spec(c, layer) for c in consts],
            out_specs=pl.BlockSpec((tc, D), lambda i, d: (i, 0)),
            scratch_shapes=[pltpu.VMEM((2, TOP_K, tc * ROW_CHUNKS, LANES), F32),
                            pltpu.VMEM((D, D), BF16), pltpu.VMEM((D_PLE, D), BF16),
                            pltpu.SemaphoreType.DMA((2,))]),
        out_shape=jax.ShapeDtypeStruct((T, D), F32),
        compiler_params=pltpu.CompilerParams(
            dimension_semantics=("arbitrary",), vmem_limit_bytes=VMEM_LIMIT_BYTES),
        name="combine",
    )(dest, x, p, gates, ys, *consts)


def _moe_layer(alpha, layer, x, p, router_w, w1, w3, w2, ln_g, ln_b, gate_w, ple_w):
    T, D = x.shape
    route_i, route_g, counts = _router(x, router_w)
    idx = route_i[0:TOP_K]
    rank = route_i[TOP_K:2 * TOP_K]
    gates = route_g[0:TOP_K].T

    start, te, valid, nt, fill_tiles, n_rows = _sorted_layout(counts[:, 0], TOP_K * T)
    start_of = sum(jnp.where(idx == e, start[e], 0) for e in range(N_EXPERTS))
    dest = (start_of + rank).astype(jnp.int32).reshape(-1)

    xs = _dispatch(x, dest, fill_tiles, n_rows)
    ys = _expert_ffn(xs, te, valid, nt, w1, w3, w2)
    return _combine_layer(alpha, layer, x, p, gates, dest, ys, ln_g, ln_b, gate_w, ple_w)


def kernel(x, p, w_in, pool_w, pool_scale, conv_w, conv_b, conv_ln_g, conv_ln_b, conv_pw, w_out, ln1_g, ln1_b,
           dense_w1, dense_w3, dense_w2, router_w, exp_w1, exp_w3, exp_w2, ln2_g, ln2_b, ple_gate_w, ple_w):
    depth = w_in.shape[0]
    alpha = (2.0 * depth) ** 0.25
    B, S, D = x.shape
    pt = p.reshape(depth, B * S, D_PLE)
    one_group = lambda m: m.reshape(m.shape[0], 1, *m.shape[1:])
    for i in range(depth):
        j = i // 2
        is_dense = i % 2 == 0
        if is_dense:
            side = [(one_group(dense_w1), j), (one_group(dense_w3), j), (one_group(dense_w2), j),
                    (one_group(ple_gate_w), i), (one_group(ple_w), i)]
        else:
            side = [(exp_w1, j), (exp_w3, j), (exp_w2, j)]
        x, side_bf16 = _mixer(alpha, i, x, w_in, pool_w, pool_scale, conv_w, conv_b, conv_ln_g, conv_ln_b,
                              conv_pw, w_out, ln1_g, ln1_b, side)
        xt = x.reshape(B * S, D)
        if is_dense:
            w1b, w3b, w2b, gate_b, ple_b = side_bf16
            xt = _dense_layer(alpha, i, xt, pt, w1b, w3b, w2b, ln2_g, ln2_b, gate_b, ple_b)
        else:
            xt = _moe_layer(alpha, i, xt, pt, router_w[j], *side_bf16, ln2_g, ln2_b, ple_gate_w, ple_w)
        x = xt.reshape(B, S, D)
    return x
```

```python
import functools

import jax
import jax.numpy as jnp
from jax import lax
from jax.experimental import pallas as pl
from jax.experimental.pallas import tpu as pltpu

D_MODEL = 1024
D_PLE = 256
D_POOL = 512
D_CONV = D_MODEL - D_POOL
POOL_WINDOWS = (2, 4, 8, 16)
POOL_GROUP_DIM = D_POOL // len(POOL_WINDOWS)
CONV_WIDTH = 31
D_IN_PROJ = D_POOL + 2 * D_CONV
N_EXPERTS = 8
TOP_K = 2
LN_EPS = 1e-5

F32 = jnp.float32
BF16 = jnp.bfloat16

SUBLANES = 8
LANES = 128
VMEM_LIMIT_BYTES = 56 * 1024 * 1024

HALO = 32
TS_MIX = 512
RB_CONV = 128
IN_PROJ_CHUNK = 256
MIX_ROW_GROUPS = 2
TM_DENSE = 512
FF_CHUNK_DENSE = 1024
TR_ROUTE = 512
TM_GROUP = 512
FF_CHUNK_MOE = 1792
TC_DISPATCH = 512
TC_COMBINE = 256
ROW_CHUNKS = D_MODEL // LANES
ISSUE_UNROLL = 8


def _sigmoid(z):
    return 1.0 / (1.0 + jnp.exp(-z))


def _layer_norm(h, g, b):
    mu = jnp.mean(h, axis=-1, keepdims=True)
    c = h - mu
    var = jnp.mean(c * c, axis=-1, keepdims=True)
    return c * lax.rsqrt(var + LN_EPS) * g + b


def _dot(a, b):
    return jnp.dot(a, b, preferred_element_type=F32)


def _side_cast_copies(load, layers, q, hbm, bufs, sems):
    copies = []
    for a in range(len(hbm)):
        rows = bufs[a].shape[0]
        per_group = hbm[a].shape[-2] // rows
        g = q // per_group
        r0 = pl.multiple_of((q % per_group) * rows, rows)
        if load:
            copies.append(pltpu.make_async_copy(hbm[a].at[layers[a], g, pl.ds(r0, rows), :], bufs[a], sems.at[a]))
        else:
            copies.append(pltpu.make_async_copy(bufs[a], hbm[a].at[g, pl.ds(r0, rows), :], sems.at[a]))
    return copies


def _mixer_kernel(alpha, tiles_per_seq, cast_layers, n_cast_chunks, x_ref, w_in_f32, pool_w_f32, pool_scale_ref,
                  conv_w_ref, conv_b_ref, cln_g_ref, cln_b_ref, conv_pw_f32, w_out_f32, ln_g_ref, ln_b_ref, *rest):
    n_side = len(cast_layers)
    side_f32, rest = rest[:n_side], rest[n_side:]
    o_ref, side_bf16, rest = rest[0], rest[1:1 + n_side], rest[1 + n_side:]
    (p_scr, v_scr, u_scr, y_scr, ypool_scr, xprev_scr, w_in_ref, pool_w_ref, conv_pw_ref, w_out_ref) = rest[:10]
    cast_scratch = rest[10:]
    ts = x_ref.shape[0]
    t = pl.program_id(0)

    if n_side:
        in_bufs, out_bufs = cast_scratch[:n_side], cast_scratch[n_side:2 * n_side]
        in_sems, out_sems = cast_scratch[2 * n_side:]

        def chunk_loads(q):
            return _side_cast_copies(True, cast_layers, q, side_f32, in_bufs, in_sems)

        def chunk_stores(q):
            return _side_cast_copies(False, cast_layers, q, side_bf16, out_bufs, out_sems)

        @pl.when(t == 0)
        def _():
            for load in chunk_loads(0):
                load.start()

    @pl.when(t == 0)
    def _():
        w_in_ref[...] = w_in_f32[...].astype(BF16)
        pool_w_ref[...] = pool_w_f32[...].astype(BF16)
        conv_pw_ref[...] = conv_pw_f32[...].astype(BF16)
        w_out_ref[...] = w_out_f32[...].astype(BF16)
        p_scr[...] = jnp.zeros_like(p_scr)
        v_scr[...] = jnp.zeros_like(v_scr)

    def step_body(finish_prev, begin_this):
        n_conv_blocks = ts // RB_CONV
        in_proj_chunks = [(c0, min(c0 + IN_PROJ_CHUNK, D_IN_PROJ)) for c0 in range(0, D_IN_PROJ, IN_PROJ_CHUNK)]
        if begin_this:
            x = x_ref[...]
            xb = x.astype(BF16)

        blocks_per_group = n_conv_blocks // MIX_ROW_GROUPS
        for rb in range(n_conv_blocks):
            if finish_prev:
                r0 = rb * RB_CONV
                cols = []
                for lc in range(D_CONV // LANES):
                    lanes = slice(lc * LANES, (lc + 1) * LANES)
                    acc = jnp.zeros((RB_CONV, LANES), F32)
                    for k in range(CONV_WIDTH):
                        first = r0 + HALO - (CONV_WIDTH - 1 - k)
                        acc = acc + v_scr[lc, first:first + RB_CONV, :] * conv_w_ref[k:k + 1, lanes]
                    cols.append(acc)
                y = jnp.concatenate(cols, axis=-1) + conv_b_ref[...]
                z = _layer_norm(y, cln_g_ref[...], cln_b_ref[...])
                y_scr[r0:r0 + RB_CONV, :] = (z * _sigmoid(z)).astype(BF16)
            if begin_this and rb < len(in_proj_chunks):
                c0, c1 = in_proj_chunks[rb]
                u_scr[:, c0:c1] = _dot(xb, w_in_ref[:, c0:c1])
            if finish_prev and (rb + 1) % blocks_per_group == 0:
                rows = slice((rb + 1 - blocks_per_group) * RB_CONV, (rb + 1) * RB_CONV)
                y_conv = _dot(y_scr[rows, :], conv_pw_ref[...])
                heads = jnp.concatenate([ypool_scr[rows, :], y_conv.astype(BF16)], axis=-1)
                mix = _dot(heads, w_out_ref[...])
                o_ref[rows, :] = _layer_norm(alpha * xprev_scr[rows, :] + mix, ln_g_ref[...], ln_b_ref[...])
        if not begin_this:
            return
        for c0, c1 in in_proj_chunks[n_conv_blocks:]:
            u_scr[:, c0:c1] = _dot(xb, w_in_ref[:, c0:c1])

        tile_in_seq = t % tiles_per_seq
        starts_seq = tile_in_seq == 0
        p_scr[:, 0:HALO, :] = jnp.where(starts_seq, 0.0, p_scr[:, ts:ts + HALO, :])
        v_scr[:, 0:HALO, :] = jnp.where(starts_seq, 0.0, v_scr[:, ts:ts + HALO, :])
        for c in range(D_CONV // LANES):
            val = u_scr[:, D_POOL + c * LANES:D_POOL + (c + 1) * LANES]
            gate = u_scr[:, D_POOL + D_CONV + c * LANES:D_POOL + D_CONV + (c + 1) * LANES]
            v_scr[c, HALO:, :] = val * _sigmoid(gate)

        pos = tile_in_seq * ts + lax.broadcasted_iota(jnp.int32, (ts, POOL_GROUP_DIM), 0)
        for g, w in enumerate(POOL_WINDOWS):
            lanes = slice(g * POOL_GROUP_DIM, (g + 1) * POOL_GROUP_DIM)
            cur = u_scr[:, lanes]
            p_scr[g, HALO:, :] = cur
            win = cur
            for back in range(1, w):
                win = win + p_scr[g, HALO - back:HALO - back + ts, :]
            cnt = jnp.minimum(pos + 1, w).astype(F32)
            d = win / cnt - cur
            yg = _dot(d.astype(BF16), pool_w_ref[g]) * pool_scale_ref[:, lanes]
            ypool_scr[:, lanes] = yg.astype(BF16)

        xprev_scr[...] = x

    last = pl.num_programs(0) - 1
    for finish_prev, begin_this, cond in ((False, True, t == 0), (True, True, (t > 0) & (t < last)),
                                          (True, False, t == last)):
        @pl.when(cond)
        def _():
            step_body(finish_prev, begin_this)

    if n_side:
        @pl.when(t < n_cast_chunks)
        def _():
            for load in chunk_loads(t):
                load.wait()

            @pl.when(t > 0)
            def _():
                for store in chunk_stores(t - 1):
                    store.wait()

            for src, dst in zip(in_bufs, out_bufs):
                dst[...] = src[...].astype(BF16)
            for store in chunk_stores(t):
                store.start()

            @pl.when(t + 1 < n_cast_chunks)
            def _():
                for load in chunk_loads(t + 1):
                    load.start()

        @pl.when(t == n_cast_chunks)
        def _():
            for store in chunk_stores(t - 1):
                store.wait()


def _const_spec(shape):
    nd = len(shape)
    return pl.BlockSpec(shape, lambda *_: (0,) * nd, pipeline_mode=pl.Buffered(1))


def _per_layer(v):
    return v.reshape(v.shape[0], 1, v.shape[1]) if v.ndim == 2 else v


def _layer_spec(v, layer, **kwargs):
    nd = v.ndim
    return pl.BlockSpec((None,) + v.shape[1:], lambda *_: (layer,) + (0,) * (nd - 1),
                        pipeline_mode=pl.Buffered(1), **kwargs)


def _mixer(alpha, layer, x, w_in, pool_w, pool_scale, conv_w, conv_b, cln_g, cln_b, conv_pw, w_out, ln_g, ln_b,
           side_weights):
    B, S, D = x.shape
    ts = TS_MIX
    n_tiles = B * S // ts
    assert D == D_MODEL and S % ts == 0 and POOL_GROUP_DIM == LANES and D_CONV % LANES == 0
    assert ts % (RB_CONV * MIX_ROW_GROUPS) == 0 and HALO >= max(CONV_WIDTH, *POOL_WINDOWS) - 1
    consts = [_per_layer(c) for c in (w_in, pool_w, pool_scale, conv_w, conv_b, cln_g, cln_b, conv_pw, w_out,
                                      ln_g, ln_b)]
    x_spec = pl.BlockSpec((ts, D), lambda t: (jnp.minimum(t, n_tiles - 1), 0))
    o_spec = pl.BlockSpec((ts, D), lambda t: (jnp.maximum(t - 1, 0), 0))
    o_shape = jax.ShapeDtypeStruct((B * S, D), F32)
    scratch = [pltpu.VMEM((D_POOL // LANES, HALO + ts, LANES), F32),
               pltpu.VMEM((D_CONV // LANES, HALO + ts, LANES), F32),
               pltpu.VMEM((ts, D_IN_PROJ), F32),
               pltpu.VMEM((ts, D_CONV), BF16),
               pltpu.VMEM((ts, D_POOL), BF16),
               pltpu.VMEM((ts, D), F32),
               pltpu.VMEM(w_in.shape[1:], BF16), pltpu.VMEM(pool_w.shape[1:], BF16),
               pltpu.VMEM(conv_pw.shape[1:], BF16), pltpu.VMEM(w_out.shape[1:], BF16)]
    params = pltpu.CompilerParams(dimension_semantics=("arbitrary",), vmem_limit_bytes=VMEM_LIMIT_BYTES)
    in_specs = [x_spec] + [_layer_spec(c, layer) for c in consts]

    mats = [m for m, _ in side_weights]
    cast_layers = tuple(l for _, l in side_weights)
    hbm = pl.BlockSpec(memory_space=pl.ANY)
    bf16_rows = 2 * SUBLANES
    n_chunks = max(n for n in range(1, n_tiles + 1)
                   if all((m.shape[1] * m.shape[2]) % n == 0
                          and m.shape[2] % ((m.shape[1] * m.shape[2]) // n) == 0
                          and ((m.shape[1] * m.shape[2]) // n) % bf16_rows == 0 for m in mats))
    chunk_rows = [m.shape[1] * m.shape[2] // n_chunks for m in mats]
    cast_scratch = ([pltpu.VMEM((rows, m.shape[3]), F32) for m, rows in zip(mats, chunk_rows)]
                    + [pltpu.VMEM((rows, m.shape[3]), BF16) for m, rows in zip(mats, chunk_rows)]
                    + [pltpu.SemaphoreType.DMA((len(mats),)), pltpu.SemaphoreType.DMA((len(mats),))])
    out, *mats_bf16 = pl.pallas_call(
        functools.partial(_mixer_kernel, alpha, S // ts, cast_layers, n_chunks),
        grid=(n_tiles + 1,), in_specs=in_specs + [hbm] * len(mats),
        out_specs=[o_spec] + [hbm] * len(mats),
        out_shape=[o_shape] + [jax.ShapeDtypeStruct(m.shape[1:], BF16) for m in mats],
        scratch_shapes=scratch + cast_scratch, compiler_params=params, name="mixer",
    )(x.reshape(B * S, D), *consts, *mats)
    return out.reshape(B, S, D), mats_bf16


def _norm_and_embed(alpha, x, f, p, ln_g, ln_b, gate_w, ple_w):
    h = _layer_norm(alpha * x + f, ln_g, ln_b)
    gate = _sigmoid(_dot(h.astype(BF16), gate_w))
    return h + gate * _dot(p.astype(BF16), ple_w)


def _dense_ffn_kernel(alpha, x_ref, p_ref, w1_ref, w3_ref, w2_ref, ln_g_ref, ln_b_ref, gate_w_ref, ple_w_ref,
                      o_ref, acc_ref):
    x = x_ref[...]
    xb = x.astype(BF16)
    ff = w1_ref.shape[1]
    for c0 in range(0, ff, FF_CHUNK_DENSE):
        c1 = min(c0 + FF_CHUNK_DENSE, ff)
        a = _dot(xb, w1_ref[:, c0:c1])
        b = _dot(xb, w3_ref[:, c0:c1])
        h = (a * _sigmoid(a) * b).astype(BF16)
        part = _dot(h, w2_ref[c0:c1, :])
        if c0 == 0:
            acc_ref[...] = part
        else:
            acc_ref[...] += part
    o_ref[...] = _norm_and_embed(alpha, x, acc_ref[...], p_ref[...], ln_g_ref[...], ln_b_ref[...],
                                 gate_w_ref[...], ple_w_ref[...])


def _dense_layer(alpha, layer, x, p, w1, w3, w2, ln_g, ln_b, gate_w, ple_w):
    T, D = x.shape
    tm = TM_DENSE
    assert T % tm == 0
    ln_g, ln_b = _per_layer(ln_g), _per_layer(ln_b)
    return pl.pallas_call(
        functools.partial(_dense_ffn_kernel, alpha),
        grid=(T // tm,),
        in_specs=[pl.BlockSpec((tm, D), lambda i: (i, 0)),
                  pl.BlockSpec((None, tm, D_PLE), lambda i: (layer, i, 0)),
                  _layer_spec(w1, 0), _layer_spec(w3, 0), _layer_spec(w2, 0),
                  _layer_spec(ln_g, layer), _layer_spec(ln_b, layer),
                  _layer_spec(gate_w, 0), _layer_spec(ple_w, 0)],
        out_specs=pl.BlockSpec((tm, D), lambda i: (i, 0)),
        out_shape=jax.ShapeDtypeStruct((T, D), F32),
        scratch_shapes=[pltpu.VMEM((tm, D), F32)],
        compiler_params=pltpu.CompilerParams(
            dimension_semantics=("arbitrary",), vmem_limit_bytes=VMEM_LIMIT_BYTES),
        name="dense_ffn",
    )(x, p, w1, w3, w2, ln_g, ln_b, gate_w, ple_w)


def _router_kernel(x_ref, wt_ref, idx_ref, gate_ref, cnt_ref, carry_ref):
    i = pl.program_id(0)
    tr = x_ref.shape[0]

    @pl.when(i == 0)
    def _():
        carry_ref[...] = jnp.zeros_like(carry_ref)

    x = x_ref[...]
    wt = wt_ref[...]
    xh = x.astype(BF16)
    xl = (x - xh.astype(F32)).astype(BF16)
    wh = wt.astype(BF16)
    wl = (wt - wh.astype(F32)).astype(BF16)
    nt_dims = (((1,), (1,)), ((), ()))
    dg = lambda a, b: lax.dot_general(a, b, nt_dims, preferred_element_type=F32)
    with_xh = dg(jnp.concatenate([wh, wl], axis=0), xh)
    logits = with_xh[:N_EXPERTS] + (dg(wh, xl) + with_xh[N_EXPERTS:])

    eid = lax.broadcasted_iota(jnp.int32, logits.shape, 0)
    m1 = jnp.max(logits, axis=0, keepdims=True)
    i1 = jnp.min(jnp.where(logits == m1, eid, N_EXPERTS), axis=0, keepdims=True)
    rest = jnp.where(eid == i1, -jnp.inf, logits)
    m2 = jnp.max(rest, axis=0, keepdims=True)
    i2 = jnp.min(jnp.where(rest == m2, eid, N_EXPERTS), axis=0, keepdims=True)
    e2 = jnp.exp(m2 - m1)
    g1 = 1.0 / (1.0 + e2)
    g2 = e2 / (1.0 + e2)

    oh1 = (eid == i1).astype(F32)
    oh2 = (eid == i2).astype(F32)
    chosen = oh1 + oh2
    r_i = lax.broadcasted_iota(jnp.int32, (tr, tr), 0)
    c_i = lax.broadcasted_iota(jnp.int32, (tr, tr), 1)
    before = (r_i < c_i).astype(BF16)
    excl = _dot(chosen.astype(BF16), before) + carry_ref[:, 0:1]
    rank1 = jnp.sum(oh1 * excl, axis=0, keepdims=True)
    rank2 = jnp.sum(oh2 * excl, axis=0, keepdims=True)

    total = carry_ref[:, 0:1] + jnp.sum(chosen, axis=1, keepdims=True)
    carry_ref[...] = jnp.broadcast_to(total, carry_ref.shape)
    cnt_ref[...] = jnp.broadcast_to(total, cnt_ref.shape).astype(jnp.int32)

    zi = jnp.zeros((SUBLANES - 4, tr), jnp.int32)
    idx_ref[...] = jnp.concatenate(
        [i1, i2, rank1.astype(jnp.int32), rank2.astype(jnp.int32), zi], axis=0)
    gate_ref[...] = jnp.concatenate([g1, g2, jnp.zeros((SUBLANES - 2, tr), F32)], axis=0)


def _router(x, router_w):
    T, D = x.shape
    tr = TR_ROUTE
    return pl.pallas_call(
        _router_kernel,
        grid=(T // tr,),
        in_specs=[pl.BlockSpec((tr, D), lambda i: (i, 0)), _const_spec((N_EXPERTS, D))],
        out_specs=[pl.BlockSpec((SUBLANES, tr), lambda i: (0, i)),
                   pl.BlockSpec((SUBLANES, tr), lambda i: (0, i)),
                   pl.BlockSpec((N_EXPERTS, LANES), lambda i: (0, 0))],
        out_shape=[jax.ShapeDtypeStruct((SUBLANES, T), jnp.int32),
                   jax.ShapeDtypeStruct((SUBLANES, T), F32),
                   jax.ShapeDtypeStruct((N_EXPERTS, LANES), jnp.int32)],
        scratch_shapes=[pltpu.VMEM((N_EXPERTS, LANES), F32)],
        compiler_params=pltpu.CompilerParams(
            dimension_semantics=("arbitrary",), vmem_limit_bytes=VMEM_LIMIT_BYTES),
        name="router",
    )(x, router_w.T)


def _store_token_tiles(dst_ref, value):
    rows = value.shape[0]
    for c in range(ROW_CHUNKS):
        dst_ref[pl.ds(c, rows, stride=ROW_CHUNKS), :] = value[:, c * LANES:(c + 1) * LANES]


def _load_token_tile_chunk(src_ref, rows, c):
    return src_ref[pl.ds(c, rows, stride=ROW_CHUNKS), :]


def _load_token_tiles(src_ref, rows):
    return jnp.concatenate([_load_token_tile_chunk(src_ref, rows, c) for c in range(ROW_CHUNKS)], axis=-1)


def _tile_rows(row):
    return pl.ds(pl.multiple_of(row * ROW_CHUNKS, ROW_CHUNKS), ROW_CHUNKS)


def _dispatch_kernel(dest_ref, fill_ref, x_ref, xs_hbm, xt_ref, zero_ref, sems, fill_sem):
    i = pl.program_id(0)
    n = pl.num_programs(0)
    tc = x_ref.shape[0]
    n_tok = n * tc
    base = i * tc
    slot = i % 2
    stage = xt_ref.at[slot]

    @pl.when(i == 0)
    def _():
        zero_ref[...] = jnp.zeros_like(zero_ref)
        tile_rows = zero_ref.shape[0]

        def fill(j):
            start = pl.multiple_of(fill_ref[j] * tile_rows, tile_rows)
            return pltpu.make_async_copy(zero_ref, xs_hbm.at[pl.ds(start, tile_rows), :], fill_sem)

        for j in range(fill_ref.shape[0]):
            @pl.when(fill_ref[j] >= 0)
            def _():
                fill(j).start()

        for j in range(fill_ref.shape[0]):
            @pl.when(fill_ref[j] >= 0)
            def _():
                fill(j).wait()

    _store_token_tiles(stage, x_ref[...])

    def copy(k, r):
        d = dest_ref[k * n_tok + base + r]
        return pltpu.make_async_copy(stage.at[_tile_rows(r), :], xs_hbm.at[_tile_rows(d), :], sems.at[slot])

    def issue(rb, c):
        for j in range(ISSUE_UNROLL):
            r = rb * ISSUE_UNROLL + j
            for k in range(TOP_K):
                copy(k, r).start(priority=k)
        return c

    lax.fori_loop(0, tc // ISSUE_UNROLL, issue, 0)

    def wait_step(s):
        for _ in range(TOP_K):
            pltpu.make_async_copy(xt_ref.at[s], xs_hbm.at[pl.ds(0, tc * ROW_CHUNKS), :], sems.at[s]).wait()

    @pl.when(i > 0)
    def _():
        wait_step(1 - slot)

    @pl.when(i == n - 1)
    def _():
        wait_step(slot)


def _dispatch(x, dest, fill_tiles, n_rows):
    T, D = x.shape
    tc = TC_DISPATCH
    assert T % tc == 0 and tc % ISSUE_UNROLL == 0 and D == ROW_CHUNKS * LANES and ROW_CHUNKS == SUBLANES
    return pl.pallas_call(
        _dispatch_kernel,
        grid_spec=pltpu.PrefetchScalarGridSpec(
            num_scalar_prefetch=2, grid=(T // tc,),
            in_specs=[pl.BlockSpec((tc, D), lambda i, d, ft: (i, 0))],
            out_specs=pl.BlockSpec(memory_space=pl.ANY),
            scratch_shapes=[pltpu.VMEM((2, tc * ROW_CHUNKS, LANES), F32),
                            pltpu.VMEM((TM_GROUP * ROW_CHUNKS, LANES), F32),
                            pltpu.SemaphoreType.DMA((2,)), pltpu.SemaphoreType.DMA(())]),
        out_shape=jax.ShapeDtypeStruct((n_rows * ROW_CHUNKS, LANES), F32),
        compiler_params=pltpu.CompilerParams(dimension_semantics=("arbitrary",)),
        name="dispatch",
    )(dest, fill_tiles, x)


def _expert_ffn_kernel(nf, te_ref, valid_ref, nt_ref, xs_ref, w1_ref, w3_ref, w2_ref, o_ref, xb_ref, acc_ref):
    i = pl.program_id(0)
    f = pl.program_id(1)
    tm = acc_ref.shape[0]

    def run(m, first, last):
        if first:
            for c in range(ROW_CHUNKS):
                chunk = _load_token_tile_chunk(xs_ref.at[pl.ds(0, m * ROW_CHUNKS), :], m, c)
                xb_ref[0:m, c * LANES:(c + 1) * LANES] = chunk.astype(BF16)

        xb = xb_ref[0:m, :]
        a = _dot(xb, w1_ref[...])
        b = _dot(xb, w3_ref[...])
        h = (a * _sigmoid(a) * b).astype(BF16)
        part = _dot(h, w2_ref[...])

        if last:
            total = part if first else acc_ref[0:m, :] + part
            _store_token_tiles(o_ref.at[pl.ds(0, m * ROW_CHUNKS), :], total)
            if m < tm:
                o_ref[m * ROW_CHUNKS:, :] = jnp.zeros(((tm - m) * ROW_CHUNKS, LANES), F32)
        elif first:
            acc_ref[0:m, :] = part
        else:
            acc_ref[0:m, :] += part

    live = i < nt_ref[0]
    few = valid_ref[i] <= tm // 2
    for m, size_matches in ((tm, jnp.logical_not(few)), (tm // 2, few)):
        for chunk in range(nf):
            @pl.when(live & size_matches & (f == chunk))
            def _():
                run(m, chunk == 0, chunk == nf - 1)

    @pl.when(jnp.logical_not(live) & (f == 0))
    def _():
        o_ref[...] = jnp.zeros_like(o_ref)


def _expert_ffn(xs, te, valid, nt, w1, w3, w2):
    E, D, FF = w1.shape
    tm, cf = TM_GROUP, FF_CHUNK_MOE
    nf = FF // cf
    assert E == N_EXPERTS and FF == nf * cf and tm % (4 * SUBLANES) == 0
    max_tiles = xs.shape[0] // (tm * ROW_CHUNKS)

    def tile(i, nt_ref):
        return jnp.minimum(i, nt_ref[0] - 1)

    def chunk(i, f, nt_ref):
        return jnp.where(i < nt_ref[0], f, nf - 1)

    def w13_map(i, f, te, valid, nt):
        return (te[tile(i, nt)], 0, chunk(i, f, nt))

    def w2_map(i, f, te, valid, nt):
        return (te[tile(i, nt)], chunk(i, f, nt), 0)

    return pl.pallas_call(
        functools.partial(_expert_ffn_kernel, nf),
        grid_spec=pltpu.PrefetchScalarGridSpec(
            num_scalar_prefetch=3, grid=(max_tiles, nf),
            in_specs=[pl.BlockSpec((tm * ROW_CHUNKS, LANES), lambda i, f, te, valid, nt: (tile(i, nt), 0)),
                      pl.BlockSpec((None, D, cf), w13_map),
                      pl.BlockSpec((None, D, cf), w13_map),
                      pl.BlockSpec((None, cf, D), w2_map)],
            out_specs=pl.BlockSpec((tm * ROW_CHUNKS, LANES), lambda i, f, te, valid, nt: (i, 0)),
            scratch_shapes=[pltpu.VMEM((tm, D), BF16), pltpu.VMEM((tm, D), F32)]),
        out_shape=jax.ShapeDtypeStruct(xs.shape, F32),
        compiler_params=pltpu.CompilerParams(
            dimension_semantics=("arbitrary", "arbitrary"), vmem_limit_bytes=VMEM_LIMIT_BYTES),
        name="expert_ffn",
    )(te, valid, nt, xs, w1, w3, w2)


def _sorted_layout(cnt, n_assign):
    tm = TM_GROUP
    max_tiles = n_assign // tm + N_EXPERTS
    tiles_e = (cnt + tm - 1) // tm
    tile_end = jnp.cumsum(tiles_e)
    row_start = (tile_end - tiles_e) * tm
    nt = tile_end[-1]
    tile_ids = jnp.minimum(jnp.arange(max_tiles, dtype=jnp.int32), nt - 1)
    te = jnp.sum((tile_end[None, :] <= tile_ids[:, None]).astype(jnp.int32), axis=1)
    te = jnp.minimum(te, N_EXPERTS - 1)
    owner = te[:, None] == jnp.arange(N_EXPERTS, dtype=jnp.int32)[None, :]
    rows_left = cnt[None, :] - (tile_ids[:, None] - (tile_end - tiles_e)[None, :]) * tm
    valid = jnp.clip(jnp.sum(jnp.where(owner, rows_left, 0), axis=1), 0, tm).astype(jnp.int32)
    last_tile = jnp.where(tiles_e > 0, tile_end - 1, -1)
    tail = nt + jnp.arange(N_EXPERTS, dtype=jnp.int32)
    tail = jnp.where(tail < max_tiles, tail, -1)
    fill_tiles = jnp.concatenate([last_tile, tail]).astype(jnp.int32)
    return row_start, te, valid, nt.reshape(1).astype(jnp.int32), fill_tiles, max_tiles * tm


def _combine_kernel(alpha, dest_ref, x_ref, p_ref, gate_ref, ys_hbm, ln_g_ref, ln_b_ref, gate_w_f32, ple_w_f32,
                    o_ref, yb_ref, gate_w_ref, ple_w_ref, sems):
    i = pl.program_id(0)
    n = pl.num_programs(0)
    tc = x_ref.shape[0]
    n_tok = n * tc
    slot = i % 2

    @pl.when(i == 0)
    def _():
        gate_w_ref[...] = gate_w_f32[...].astype(BF16)
        ple_w_ref[...] = ple_w_f32[...].astype(BF16)

    def row_copy(step, s, k, r):
        d = dest_ref[k * n_tok + step * tc + r]
        return pltpu.make_async_copy(ys_hbm.at[_tile_rows(d), :], yb_ref.at[s, k, _tile_rows(r), :], sems.at[s])

    def wait_rows(s):
        for k in range(TOP_K):
            pltpu.make_async_copy(ys_hbm.at[pl.ds(0, tc * ROW_CHUNKS), :], yb_ref.at[s, k], sems.at[s]).wait()

    @pl.when(i == 0)
    def _():
        def issue(rb, c):
            for j in range(ISSUE_UNROLL):
                for k in range(TOP_K):
                    row_copy(0, 0, k, rb * ISSUE_UNROLL + j).start(priority=k)
            return c

        lax.fori_loop(0, tc // ISSUE_UNROLL, issue, 0)

    wait_rows(slot)

    nxt = jnp.minimum(i + 1, n - 1)
    for r in range(tc):
        for k in range(TOP_K):
            row_copy(nxt, 1 - slot, k, r).start(priority=k)

    g = gate_ref[...]
    f = (g[:, 0:1] * _load_token_tiles(yb_ref.at[slot, 0], tc)
         + g[:, 1:2] * _load_token_tiles(yb_ref.at[slot, 1], tc))
    o_ref[...] = _norm_and_embed(alpha, x_ref[...], f, p_ref[...], ln_g_ref[...], ln_b_ref[...],
                                 gate_w_ref[...], ple_w_ref[...])

    @pl.when(i == n - 1)
    def _():
        wait_rows(1 - slot)


def _combine_layer(alpha, layer, x, p, gates, dest, ys, ln_g, ln_b, gate_w, ple_w):
    T, D = x.shape
    tc = TC_COMBINE
    assert T % tc == 0 and tc % ISSUE_UNROLL == 0
    consts = [_per_layer(c) for c in (ln_g, ln_b, gate_w, ple_w)]
    return pl.pallas_call(
        functools.partial(_combine_kernel, alpha),
        grid_spec=pltpu.PrefetchScalarGridSpec(
            num_scalar_prefetch=1, grid=(T // tc,),
            in_specs=[pl.BlockSpec((tc, D), lambda i, d: (i, 0)),
                      pl.BlockSpec((None, tc, D_PLE), lambda i, d: (layer, i, 0)),
                      pl.BlockSpec((tc, TOP_K), lambda i, d: (i, 0)),
                      pl.BlockSpec(memory_space=pl.ANY)]
                     + [_layer_spec(c, layer) for c in consts],
            out_specs=pl.BlockSpec((tc, D), lambda i, d: (i, 0)),
            scratch_shapes=[pltpu.VMEM((2, TOP_K, tc * ROW_CHUNKS, LANES), F32),
                            pltpu.VMEM((D, D), BF16), pltpu.VMEM((D_PLE, D), BF16),
                            pltpu.SemaphoreType.DMA((2,))]),
        out_shape=jax.ShapeDtypeStruct((T, D), F32),
        compiler_params=pltpu.CompilerParams(
            dimension_semantics=("arbitrary",), vmem_limit_bytes=VMEM_LIMIT_BYTES),
        name="combine",
    )(dest, x, p, gates, ys, *consts)


def _moe_layer(alpha, layer, x, p, router_w, w1, w3, w2, ln_g, ln_b, gate_w, ple_w):
    T, D = x.shape
    route_i, route_g, counts = _router(x, router_w)
    idx = route_i[0:TOP_K]
    rank = route_i[TOP_K:2 * TOP_K]
    gates = route_g[0:TOP_K].T

    start, te, valid, nt, fill_tiles, n_rows = _sorted_layout(counts[:, 0], TOP_K * T)
    start_of = sum(jnp.where(idx == e, start[e], 0) for e in range(N_EXPERTS))
    dest = (start_of + rank).astype(jnp.int32).reshape(-1)

    xs = _dispatch(x, dest, fill_tiles, n_rows)
    ys = _expert_ffn(xs, te, valid, nt, w1, w3, w2)
    return _combine_layer(alpha, layer, x, p, gates, dest, ys, ln_g, ln_b, gate_w, ple_w)


def kernel(x, p, w_in, pool_w, pool_scale, conv_w, conv_b, conv_ln_g, conv_ln_b, conv_pw, w_out, ln1_g, ln1_b,
           dense_w1, dense_w3, dense_w2, router_w, exp_w1, exp_w3, exp_w2, ln2_g, ln2_b, ple_gate_w, ple_w):
    depth = w_in.shape[0]
    alpha = (2.0 * depth) ** 0.25
    B, S, D = x.shape
    pt = p.reshape(depth, B * S, D_PLE)
    one_group = lambda m: m.reshape(m.shape[0], 1, *m.shape[1:])
    exp_w2_early = None
    for i in range(depth):
        j = i // 2
        is_dense = i % 2 == 0
        if is_dense:
            side = [(one_group(dense_w1), j), (one_group(dense_w3), j), (one_group(dense_w2), j),
                    (one_group(ple_gate_w), i), (one_group(ple_w), i)]
            if i + 1 < depth:
                side.append((exp_w2, (i + 1) // 2))
        else:
            side = [(exp_w1, j), (exp_w3, j)] + ([] if exp_w2_early is not None else [(exp_w2, j)])
        x, side_bf16 = _mixer(alpha, i, x, w_in, pool_w, pool_scale, conv_w, conv_b, conv_ln_g, conv_ln_b,
                              conv_pw, w_out, ln1_g, ln1_b, side)
        xt = x.reshape(B * S, D)
        if is_dense:
            w1b, w3b, w2b, gate_b, ple_b = side_bf16[:5]
            exp_w2_early = side_bf16[5] if len(side_bf16) > 5 else None
            xt = _dense_layer(alpha, i, xt, pt, w1b, w3b, w2b, ln2_g, ln2_b, gate_b, ple_b)
        else:
            experts_bf16 = list(side_bf16) + ([exp_w2_early] if exp_w2_early is not None else [])
            exp_w2_early = None
            xt = _moe_layer(alpha, i, xt, pt, router_w[j], *experts_bf16, ln2_g, ln2_b, ple_gate_w, ple_w)
        x = xt.reshape(B, S, D)
    return x
```
